```python
import jax, jax.numpy as jnp
from jax import lax
import numpy as np

D_MODEL = 2048
BATCH = 2
SEQ = 8192
DEPTH = 1
DEC_BATCH = 16
DEC_SEQ = 16
PAST_LEN = 4096

CHUNK = 64
N_HEADS = 16
N_KV_HEADS = 4
GROUP = N_HEADS // N_KV_HEADS
HEAD_DIM = 64
ROT_DIM = HEAD_DIM // 4
ROPE_THETA = 500000.0
WINDOW = 128
WIN_CHUNKS = WINDOW // CHUNK
D_CONV = 1024
CONV_W = 3
D_FF = 5632
EPS = 1e-6
NEG = -1e30
Q_W = N_HEADS * HEAD_DIM
KV_W = N_KV_HEADS * HEAD_DIM
D_IN = Q_W + 2 * KV_W + 3 * D_CONV + 2 * D_MODEL

kernel_name = "hybrid_swa_sink_shortconv_convffn_step"


def rms_norm(x, g):
    xf = x.astype(jnp.float32)
    y = xf * lax.rsqrt(jnp.mean(xf * xf, axis=-1, keepdims=True) + EPS)
    return (y * g.astype(jnp.float32)).astype(x.dtype)


def rope(x, pos):
    half = ROT_DIM // 2
    inv = jnp.power(jnp.float32(ROPE_THETA), -jnp.arange(half, dtype=jnp.float32) * (2.0 / ROT_DIM))
    ang = pos.astype(jnp.float32)[:, None] * inv[None, :]
    cos = jnp.cos(ang)[:, None, :]
    sin = jnp.sin(ang)[:, None, :]
    xf = x.astype(jnp.float32)
    x1 = xf[..., :half]
    x2 = xf[..., half:ROT_DIM]
    out = jnp.concatenate([x1 * cos - x2 * sin, x2 * cos + x1 * sin, xf[..., ROT_DIM:]], axis=-1)
    return out.astype(x.dtype)


def sink_softmax(s, sink):
    sk = sink.astype(jnp.float32)[:, :, None, None]
    m = jnp.maximum(jnp.max(s, axis=-1, keepdims=True), sk)
    p = jnp.exp(s - m)
    return p / (jnp.sum(p, axis=-1, keepdims=True) + jnp.exp(sk - m))


def banded_window_attention(q, k, v, sink):
    b, s = q.shape[:2]
    nc = s // CHUNK
    scale = HEAD_DIM ** -0.5
    qc = q.reshape(b, nc, CHUNK, N_KV_HEADS, GROUP, HEAD_DIM)
    pad = ((0, 0), (WIN_CHUNKS, 0), (0, 0), (0, 0), (0, 0))
    kp = jnp.pad(k.reshape(b, nc, CHUNK, N_KV_HEADS, HEAD_DIM), pad)
    vp = jnp.pad(v.reshape(b, nc, CHUNK, N_KV_HEADS, HEAD_DIM), pad)
    kb = jnp.concatenate([kp[:, j:j + nc] for j in range(WIN_CHUNKS + 1)], axis=2)
    vb = jnp.concatenate([vp[:, j:j + nc] for j in range(WIN_CHUNKS + 1)], axis=2)
    chunk_id = jnp.arange(nc)[:, None] - WIN_CHUNKS + jnp.arange(WIN_CHUNKS + 1)[None, :]
    valid = jnp.repeat(chunk_id >= 0, CHUNK, axis=1)
    sc = jnp.einsum('bnqhgd,bnkhd->bnhgqk', qc, kb, preferred_element_type=jnp.float32) * scale
    sc = jnp.where(valid[None, :, None, None, None, :], sc, NEG)
    p = sink_softmax(sc, sink).astype(v.dtype)
    o = jnp.einsum('bnhgqk,bnkhd->bnqhgd', p, vb)
    return o.reshape(b, s, Q_W)


def cached_window_attention(q, k, v, cache_k, cache_v, sink):
    b, t = q.shape[:2]
    scale = HEAD_DIM ** -0.5
    ka = jnp.concatenate([cache_k.astype(k.dtype), k], axis=1)
    va = jnp.concatenate([cache_v.astype(v.dtype), v], axis=1)
    qg = q.reshape(b, t, N_KV_HEADS, GROUP, HEAD_DIM)
    sc = jnp.einsum('bqhgd,bkhd->bhgqk', qg, ka, preferred_element_type=jnp.float32) * scale
    p = sink_softmax(sc, sink).astype(v.dtype)
    o = jnp.einsum('bhgqk,bkhd->bqhgd', p, va)
    return o.reshape(b, t, Q_W)


def causal_dwconv(u, state, w):
    s = u.shape[1]
    ext = jnp.concatenate([state.astype(u.dtype), u], axis=1)
    y = w[0] * ext[:, 0:s]
    for j in range(1, CONV_W):
        y = y + w[j] * ext[:, j:j + s]
    return y, ext[:, ext.shape[1] - (CONV_W - 1):]


def split_proj(proj):
    o = Q_W
    idx = [o, o + KV_W, o + 2 * KV_W, o + 2 * KV_W + D_CONV, o + 2 * KV_W + 2 * D_CONV,
           o + 2 * KV_W + 3 * D_CONV, o + 2 * KV_W + 3 * D_CONV + D_MODEL]
    return jnp.split(proj, idx, axis=-1)


def layer(x, pos, cache_k, cache_v, conv_state, ffn_state, g_mix_pre, w_in, sinks, conv_w,
          w_attn_o, w_conv_o, w_mix_out, g_mix_post, g_ffn_pre, w_up, conv_ffn_w, w_down, g_ffn_post):
    b, s = x.shape[:2]
    xn = rms_norm(x, g_mix_pre)
    q, k, v, xin, b_gate, c_gate, ga, gc = split_proj(xn @ w_in)
    q = rope(q.reshape(b, s, N_HEADS, HEAD_DIM), pos)
    k = rope(k.reshape(b, s, N_KV_HEADS, HEAD_DIM), pos)
    v = v.reshape(b, s, N_KV_HEADS, HEAD_DIM)
    sink = sinks.reshape(N_KV_HEADS, GROUP)
    if cache_k is None:
        o_attn = banded_window_attention(q, k, v, sink)
        keep = min(WINDOW, s)
        new_k = k[:, s - keep:]
        new_v = v[:, s - keep:]
        conv_state = jnp.zeros((b, CONV_W - 1, D_CONV), x.dtype)
        ffn_state = jnp.zeros((b, CONV_W - 1, D_FF), x.dtype)
    else:
        o_attn = cached_window_attention(q, k, v, cache_k, cache_v, sink)
        new_k = k
        new_v = v
    cy, new_conv = causal_dwconv(c_gate * xin, conv_state, conv_w)
    y_conv = b_gate * cy
    mixed = jax.nn.sigmoid(ga) * (o_attn @ w_attn_o) + jax.nn.sigmoid(gc) * (y_conv @ w_conv_o)
    h = x + rms_norm(mixed @ w_mix_out, g_mix_post)
    a, g = jnp.split(rms_norm(h, g_ffn_pre) @ w_up, 2, axis=-1)
    ua, new_ffn = causal_dwconv(a, ffn_state, conv_ffn_w)
    out = h + rms_norm((jax.nn.gelu(ua, approximate=True) * g) @ w_down, g_ffn_post)
    return out, new_k, new_v, new_conv, new_ffn


def setup_inputs(seed: int = 0) -> dict:
    key = jax.random.key(seed)
    ks = jax.random.split(key, 24)
    f32 = jnp.float32
    kv_rows = min(WINDOW, PAST_LEN)

    def nrm(k, shape, scale):
        return jax.random.normal(k, shape, f32) * scale

    def gain(k):
        return 1.0 + 0.05 * jax.random.normal(k, (DEPTH, D_MODEL), f32)

    return {
        "x_prompt": nrm(ks[0], (BATCH, SEQ, D_MODEL), 1.0),
        "x_sample": nrm(ks[1], (DEC_BATCH, DEC_SEQ, D_MODEL), 1.0),
        "cache_k": nrm(ks[2], (DEPTH, DEC_BATCH, kv_rows, N_KV_HEADS, HEAD_DIM), 1.0),
        "cache_v": nrm(ks[3], (DEPTH, DEC_BATCH, kv_rows, N_KV_HEADS, HEAD_DIM), 1.0),
        "state_conv_mix": nrm(ks[4], (DEPTH, DEC_BATCH, CONV_W - 1, D_CONV), 1.0),
        "state_conv_ffn": nrm(ks[5], (DEPTH, DEC_BATCH, CONV_W - 1, D_FF), 1.0),
        "norm_mix_pre": gain(ks[6]),
        "w_in": nrm(ks[7], (DEPTH, D_MODEL, D_IN), D_MODEL ** -0.5),
        "attn_sinks": nrm(ks[8], (DEPTH, N_HEADS), 1.0),
        "conv_mix_w": nrm(ks[9], (DEPTH, CONV_W, D_CONV), CONV_W ** -0.5),
        "w_attn_o": nrm(ks[10], (DEPTH, Q_W, D_MODEL), Q_W ** -0.5),
        "w_conv_o": nrm(ks[11], (DEPTH, D_CONV, D_MODEL), D_CONV ** -0.5),
        "w_mix_out": nrm(ks[12], (DEPTH, D_MODEL, D_MODEL), D_MODEL ** -0.5),
        "norm_mix_post": gain(ks[13]),
        "norm_ffn_pre": gain(ks[14]),
        "w_ffn_up": nrm(ks[15], (DEPTH, D_MODEL, 2 * D_FF), D_MODEL ** -0.5),
        "conv_ffn_w": nrm(ks[16], (DEPTH, CONV_W, D_FF), CONV_W ** -0.5),
        "w_ffn_down": nrm(ks[17], (DEPTH, D_FF, D_MODEL), D_FF ** -0.5),
        "norm_ffn_post": gain(ks[18]),
    }


def reference(x_prompt, x_sample, cache_k, cache_v, state_conv_mix, state_conv_ffn,
              norm_mix_pre, w_in, attn_sinks, conv_mix_w, w_attn_o, w_conv_o, w_mix_out,
              norm_mix_post, norm_ffn_pre, w_ffn_up, conv_ffn_w, w_ffn_down, norm_ffn_post):
    pos_p = jnp.arange(x_prompt.shape[1], dtype=jnp.float32)
    pos_s = PAST_LEN + jnp.arange(x_sample.shape[1], dtype=jnp.float32)
    hp, hs = x_prompt, x_sample
    kp_l, vp_l, cmp_l, cfp_l, ks_l, vs_l, cms_l, cfs_l = [], [], [], [], [], [], [], []
    for l in range(DEPTH):
        w = (norm_mix_pre[l], w_in[l], attn_sinks[l], conv_mix_w[l], w_attn_o[l], w_conv_o[l],
             w_mix_out[l], norm_mix_post[l], norm_ffn_pre[l], w_ffn_up[l], conv_ffn_w[l],
             w_ffn_down[l], norm_ffn_post[l])
        hp, kp, vp, cmp_, cfp = layer(hp, pos_p, None, None, None, None, *w)
        hs, kn, vn, cms, cfs = layer(hs, pos_s, cache_k[l], cache_v[l], state_conv_mix[l],
                                     state_conv_ffn[l], *w)
        kp_l.append(kp); vp_l.append(vp); cmp_l.append(cmp_); cfp_l.append(cfp)
        ks_l.append(kn); vs_l.append(vn); cms_l.append(cms); cfs_l.append(cfs)
    new_k_prompt = jnp.stack(kp_l)
    new_v_prompt = jnp.stack(vp_l)
    conv_mix_prompt = jnp.stack(cmp_l)
    conv_ffn_prompt = jnp.stack(cfp_l)
    new_k_sample = jnp.stack(ks_l)
    new_v_sample = jnp.stack(vs_l)
    conv_mix_sample = jnp.stack(cms_l)
    conv_ffn_sample = jnp.stack(cfs_l)
    return (hp, hs, new_k_prompt, new_v_prompt, conv_mix_prompt, conv_ffn_prompt,
            new_k_sample, new_v_sample, conv_mix_sample, conv_ffn_sample)
```

```python
import functools

import jax
import jax.numpy as jnp
from jax import lax
from jax.experimental import pallas as pl
from jax.experimental.pallas import tpu as pltpu

F32 = jnp.float32
BF16 = jnp.bfloat16

D_MODEL = 2048
CHUNK = 64
N_HEADS = 16
N_KV_HEADS = 4
GROUP = N_HEADS // N_KV_HEADS
HEAD_DIM = 64
ROT_DIM = HEAD_DIM // 4
ROPE_THETA = 500000.0
WINDOW = 128
D_CONV = 1024
CONV_W = 3
D_FF = 5632
EPS = 1e-6
NEG = -1e30
PAST_LEN = 4096
Q_W = N_HEADS * HEAD_DIM
KV_W = N_KV_HEADS * HEAD_DIM
D_IN = Q_W + 2 * KV_W + 3 * D_CONV + 2 * D_MODEL

LANES = 128
SUBLANES = 8
VMEM_LIMIT_CAP = 60 * 1024 * 1024
VMEM_COMPILER_SLACK = 6 * 1024 * 1024

TN_IN = 512
TN_FF = 512
ATT_BLK = 2 * CHUNK
HALO = SUBLANES

_T_Q = (0, 2)
_T_KV = (2, 3)
_T_XIN = (3, 5)
_T_B = (5, 7)
_T_C = (7, 9)
_T_GA = (9, 13)
_T_GC = (13, 17)


def _vmem_limit(nbytes):
    return int(min(nbytes + VMEM_COMPILER_SLACK, VMEM_LIMIT_CAP))


def _params(n_axes, nbytes):
    return pltpu.CompilerParams(
        dimension_semantics=("arbitrary",) * n_axes,
        vmem_limit_bytes=_vmem_limit(nbytes),
    )


def _rms_norm(x, g):
    return x * lax.rsqrt(jnp.mean(x * x, axis=-1, keepdims=True) + EPS) * g


def _rope128(x, cos, sin):
    dh = lax.broadcasted_iota(jnp.int32, x.shape, 1) & (HEAD_DIM - 1)
    half = ROT_DIM // 2
    partner = jnp.where(dh < half, pltpu.roll(x, LANES - half, axis=1), pltpu.roll(x, half, axis=1))
    return x * cos + partner * sin


def _shifted_rows(a, prev1_row0, prev2_row0, prev2_row1, period):
    assert period & (period - 1) == 0
    rows = lax.broadcasted_iota(jnp.int32, (a.shape[0], 1), 0) & (period - 1)
    p1 = jnp.where(rows == 0, prev1_row0, pltpu.roll(a, 1, axis=0))
    p2 = jnp.where(rows == 0, prev2_row0, jnp.where(rows == 1, prev2_row1, pltpu.roll(a, 2, axis=0)))
    return p1, p2


def _inproj_body(x_ref, g_ref, cos_ref, sin_ref, w_ref,
                 q_ref, k_ref, v_ref, u_ref, b_ref, ga_ref, gc_ref, xn_sc, xin_sc):
    j = pl.program_id(1)

    @pl.when(j == 0)
    def _():
        xn_sc[...] = _rms_norm(x_ref[...], g_ref[...]).astype(BF16)

    acc = jnp.dot(xn_sc[...], w_ref[...], preferred_element_type=F32)
    cos = cos_ref[...]
    sin = sin_ref[...]

    @pl.when(j < _T_Q[1])
    def _():
        for blk in range(TN_IN // LANES):
            sl = slice(blk * LANES, (blk + 1) * LANES)
            q_ref[:, sl] = (_rope128(acc[:, sl], cos, sin) * (HEAD_DIM ** -0.5)).astype(BF16)

    @pl.when(j == _T_KV[0])
    def _():
        for blk in range(KV_W // LANES):
            sl = slice(blk * LANES, (blk + 1) * LANES)
            k_ref[:, sl] = _rope128(acc[:, sl], cos, sin)
        v_ref[...] = acc[:, KV_W:2 * KV_W]

    @pl.when((j >= _T_XIN[0]) & (j < _T_XIN[1]))
    def _():
        xin_sc[j - _T_XIN[0]] = acc

    @pl.when((j >= _T_B[0]) & (j < _T_B[1]))
    def _():
        b_ref[...] = acc.astype(BF16)

    @pl.when((j >= _T_C[0]) & (j < _T_C[1]))
    def _():
        u_ref[...] = acc * xin_sc[j - _T_C[0]]

    @pl.when((j >= _T_GA[0]) & (j < _T_GA[1]))
    def _():
        ga_ref[...] = acc.astype(BF16)

    @pl.when(j >= _T_GC[0])
    def _():
        gc_ref[...] = acc.astype(BF16)


def _seg_map(seg):
    lo, hi = seg
    return lambda i, j: (i, jnp.clip(j - lo, 0, hi - lo - 1))


def _inproj(x, g, cos, sin, w, tm):
    t = x.shape[0]
    n_pos = cos.shape[0] // tm
    grid = (t // tm, D_IN // TN_IN)
    out_shape = (
        jax.ShapeDtypeStruct((t, Q_W), BF16),
        jax.ShapeDtypeStruct((t, KV_W), F32),
        jax.ShapeDtypeStruct((t, KV_W), F32),
        jax.ShapeDtypeStruct((t, D_CONV), F32),
        jax.ShapeDtypeStruct((t, D_CONV), BF16),
        jax.ShapeDtypeStruct((t, D_MODEL), BF16),
        jax.ShapeDtypeStruct((t, D_MODEL), BF16),
    )
    in_specs = [
        pl.BlockSpec((tm, D_MODEL), lambda i, j: (i, 0)),
        pl.BlockSpec((1, D_MODEL), lambda i, j: (0, 0)),
        pl.BlockSpec((tm, LANES), lambda i, j: (i % n_pos, 0)),
        pl.BlockSpec((tm, LANES), lambda i, j: (i % n_pos, 0)),
        pl.BlockSpec((D_MODEL, TN_IN), lambda i, j: (0, j)),
    ]
    out_specs = (
        pl.BlockSpec((tm, TN_IN), _seg_map(_T_Q)),
        pl.BlockSpec((tm, KV_W), lambda i, j: (i, 0)),
        pl.BlockSpec((tm, KV_W), lambda i, j: (i, 0)),
        pl.BlockSpec((tm, TN_IN), _seg_map(_T_C)),
        pl.BlockSpec((tm, TN_IN), _seg_map(_T_B)),
        pl.BlockSpec((tm, TN_IN), _seg_map(_T_GA)),
        pl.BlockSpec((tm, TN_IN), _seg_map(_T_GC)),
    )
    nbytes = (2 * tm * D_MODEL * 4 + tm * D_MODEL * 2 + tm * D_CONV * 4 + 2 * D_MODEL * TN_IN * 2
              + 2 * tm * (TN_IN * 2 * 4 + KV_W * 4 * 2 + TN_IN * 4) + 4 * tm * LANES * 4
              + 2 * tm * TN_IN * 4)
    return pl.pallas_call(
        _inproj_body,
        grid=grid,
        in_specs=in_specs,
        out_specs=out_specs,
        out_shape=out_shape,
        scratch_shapes=[
            pltpu.VMEM((tm, D_MODEL), BF16),
            pltpu.VMEM((D_CONV // TN_IN, tm, TN_IN), F32),
        ],
        compiler_params=_params(2, nbytes),
        name="inproj",
    )(x, g, cos, sin, w)


def _head_pair_operands(x256, h):
    blk = x256[:, (h // 2) * LANES:(h // 2 + 1) * LANES]
    lo_lanes = lax.broadcasted_iota(jnp.int32, blk.shape, 1) < HEAD_DIM
    if h % 2 == 0:
        lo = jnp.where(lo_lanes, blk, 0.0)
        hi = pltpu.roll(lo, HEAD_DIM, axis=1)
    else:
        hi = jnp.where(lo_lanes, 0.0, blk)
        lo = pltpu.roll(hi, HEAD_DIM, axis=1)
    return jnp.concatenate([lo, hi], axis=0).astype(BF16)


def _attend(q_ref, q_rows, k_all, v_all, valid, sink_ref, o_ref):
    n_keys = k_all.shape[0]
    for h in range(N_KV_HEADS):
        kcat = _head_pair_operands(k_all, h)
        vcat = _head_pair_operands(v_all, h)
        for p in range(2 * h, 2 * h + 2):
            lanes = slice(p * LANES, (p + 1) * LANES)
            sc = lax.dot_general(q_ref[q_rows, lanes], kcat, (((1,), (1,)), ((), ())),
                                 preferred_element_type=F32)
            sc = jnp.where(valid, sc, NEG)
            probs = []
            for hh in range(2):
                s_h = sc[:, hh * n_keys:(hh + 1) * n_keys]
                sink = sink_ref[2 * p + hh]
                m = jnp.maximum(jnp.max(s_h, axis=-1, keepdims=True), sink)
                e = jnp.exp(s_h - m)
                den = jnp.sum(e, axis=-1, keepdims=True) + jnp.exp(sink - m)
                probs.append((e * (1.0 / den)).astype(BF16))
            pcat = jnp.concatenate(probs, axis=1)
            o_ref[q_rows, lanes] = jnp.dot(pcat, vcat, preferred_element_type=F32).astype(BF16)


def _attn_prompt_body(sink_ref, q_ref, kc_ref, kp_ref, vc_ref, vp_ref, o_ref, *, tiles_per_seq):
    first = (pl.program_id(0) % tiles_per_seq) == 0
    n_sub = q_ref.shape[0] // ATT_BLK
    n_keys = 2 * ATT_BLK
    shape = (ATT_BLK, 2 * n_keys)
    q_chunk = lax.broadcasted_iota(jnp.int32, shape, 0) // CHUNK
    k_chunk = (lax.broadcasted_iota(jnp.int32, shape, 1) & (n_keys - 1)) // CHUNK
    band = (k_chunk >= q_chunk) & (k_chunk <= q_chunk + WINDOW // CHUNK)
    first_valid = band & (k_chunk >= jnp.where(first, ATT_BLK // CHUNK, 0))
    for s in range(n_sub):
        rows = slice(s * ATT_BLK, (s + 1) * ATT_BLK)
        if s == 0:
            k_prev, v_prev, valid = kp_ref[...], vp_ref[...], first_valid
        else:
            prev = slice((s - 1) * ATT_BLK, s * ATT_BLK)
            k_prev, v_prev, valid = kc_ref[prev, :], vc_ref[prev, :], band
        k_all = jnp.concatenate([k_prev, kc_ref[rows, :]], axis=0)
        v_all = jnp.concatenate([v_prev, vc_ref[rows, :]], axis=0)
        _attend(q_ref, rows, k_all, v_all, valid, sink_ref, o_ref)


def _attn_prompt(sinks, q, k, v, seq, tq):
    t = q.shape[0]
    ratio = tq // ATT_BLK
    cur = lambda i: (i, 0)
    prev = lambda i: (jnp.maximum(i * ratio - 1, 0), 0)
    nbytes = 2 * (2 * tq * Q_W * 2 + 2 * (tq + ATT_BLK) * KV_W * 4) + 8 * ATT_BLK * 4 * ATT_BLK * 4
    return pl.pallas_call(
        functools.partial(_attn_prompt_body, tiles_per_seq=seq // tq),
        grid=(t // tq,),
        in_specs=[
            pl.BlockSpec(memory_space=pltpu.SMEM),
            pl.BlockSpec((tq, Q_W), cur),
            pl.BlockSpec((tq, KV_W), cur),
            pl.BlockSpec((ATT_BLK, KV_W), prev),
            pl.BlockSpec((tq, KV_W), cur),
            pl.BlockSpec((ATT_BLK, KV_W), prev),
        ],
        out_specs=pl.BlockSpec((tq, Q_W), cur),
        out_shape=jax.ShapeDtypeStruct((t, Q_W), BF16),
        compiler_params=_params(1, nbytes),
        name="attn_prompt",
    )(sinks, q, k, k, v, v)


def _attn_sample_body(sink_ref, q_ref, k_ref, v_ref, o_ref, *, n_valid):
    n_keys = k_ref.shape[0]
    shape = (q_ref.shape[0], 2 * n_keys)
    valid = (lax.broadcasted_iota(jnp.int32, shape, 1) & (n_keys - 1)) < n_valid
    _attend(q_ref, slice(None), k_ref[...], v_ref[...], valid, sink_ref, o_ref)


def _attn_sample(sinks, q, k_all, v_all, n_valid, tq):
    t = q.shape[0]
    n_keys = k_all.shape[0] // (t // tq)
    nbytes = 2 * (2 * tq * Q_W * 2 + 2 * n_keys * KV_W * 4) + 8 * tq * 2 * n_keys * 4 + 4 * n_keys * LANES * 4
    return pl.pallas_call(
        functools.partial(_attn_sample_body, n_valid=n_valid),
        grid=(t // tq,),
        in_specs=[
            pl.BlockSpec(memory_space=pltpu.SMEM),
            pl.BlockSpec((tq, Q_W), lambda i: (i, 0)),
            pl.BlockSpec((n_keys, KV_W), lambda i: (i, 0)),
            pl.BlockSpec((n_keys, KV_W), lambda i: (i, 0)),
        ],
        out_specs=pl.BlockSpec((tq, Q_W), lambda i: (i, 0)),
        out_shape=jax.ShapeDtypeStruct((t, Q_W), BF16),
        compiler_params=_params(1, nbytes),
        name="attn_sample",
    )(sinks, q, k_all, v_all)


def _mix_body(*refs, tiles_per_seq, hist_period):
    if hist_period:
        (x_ref, oa_ref, u_ref, b_ref, ga_ref, gc_ref, p1_ref, p2_ref,
         cw_ref, wao_ref, wco_ref, wmo_ref, g_ref, h_ref) = refs
    else:
        (x_ref, oa_ref, u_ref, b_ref, ga_ref, gc_ref,
         cw_ref, wao_ref, wco_ref, wmo_ref, g_ref, h_ref, carry_sc) = refs
    u = u_ref[...]
    tm = u.shape[0]
    if hist_period:
        p1, p2 = _shifted_rows(u, p1_ref[...], p2_ref[...], p2_ref[...], hist_period)
    else:
        @pl.when(pl.program_id(0) % tiles_per_seq == 0)
        def _():
            carry_sc[...] = jnp.zeros_like(carry_sc)

        last1 = carry_sc[HALO - 1:HALO, :]
        last2 = carry_sc[HALO - 2:HALO - 1, :]
        p1, p2 = _shifted_rows(u, last1, last2, last1, tm)
        carry_sc[...] = u[tm - HALO:tm, :]
    cy = cw_ref[0:1, :] * p2 + cw_ref[1:2, :] * p1 + cw_ref[2:3, :] * u
    y_conv = (b_ref[...].astype(F32) * cy).astype(BF16)
    attn_o = jnp.dot(oa_ref[...], wao_ref[...], preferred_element_type=F32)
    conv_o = jnp.dot(y_conv, wco_ref[...], preferred_element_type=F32)
    mixed = (jax.nn.sigmoid(ga_ref[...].astype(F32)) * attn_o
             + jax.nn.sigmoid(gc_ref[...].astype(F32)) * conv_o)
    mo = jnp.dot(mixed.astype(BF16), wmo_ref[...], preferred_element_type=F32)
    h_ref[...] = x_ref[...] + _rms_norm(mo, g_ref[...])


def _mix(x, oa, u, b, ga, gc, hist, cw, wao, wco, wmo, g, seq, tm):
    t = x.shape[0]
    row = lambda i: (i, 0)
    fixed = lambda i: (0, 0)
    resident = pl.Buffered(1)
    act_specs = [
        pl.BlockSpec((tm, D_MODEL), row),
        pl.BlockSpec((tm, Q_W), row),
        pl.BlockSpec((tm, D_CONV), row),
        pl.BlockSpec((tm, D_CONV), row),
        pl.BlockSpec((tm, D_MODEL), row),
        pl.BlockSpec((tm, D_MODEL), row),
    ]
    acts = [x, oa, u, b, ga, gc]
    scratch = []
    if hist is None:
        hist_period = 0
        scratch.append(pltpu.VMEM((HALO, D_CONV), F32))
    else:
        hist_period = seq
        act_specs += [pl.BlockSpec((tm, D_CONV), row)] * 2
        acts += list(hist)
    w_specs = [
        pl.BlockSpec((CONV_W, D_CONV), fixed),
        pl.BlockSpec((Q_W, D_MODEL), fixed, pipeline_mode=resident),
        pl.BlockSpec((D_CONV, D_MODEL), fixed, pipeline_mode=resident),
        pl.BlockSpec((D_MODEL, D_MODEL), fixed, pipeline_mode=resident),
        pl.BlockSpec((1, D_MODEL), fixed),
    ]
    row_bytes = D_MODEL * 4 * 2 + Q_W * 2 + D_CONV * (4 + 2) + D_MODEL * 2 * 2
    if hist is not None:
        row_bytes += 2 * D_CONV * 4
    nbytes = (2 * tm * row_bytes + (Q_W + D_CONV + D_MODEL) * D_MODEL * 2 + 6 * tm * D_MODEL * 4)
    return pl.pallas_call(
        functools.partial(_mix_body, tiles_per_seq=max(seq // tm, 1), hist_period=hist_period),
        grid=(t // tm,),
        in_specs=act_specs + w_specs,
        out_specs=pl.BlockSpec((tm, D_MODEL), row),
        out_shape=jax.ShapeDtypeStruct((t, D_MODEL), F32),
        scratch_shapes=scratch,
        compiler_params=_params(1, nbytes),
        name="mix",
    )(*acts, cw, wao, wco, wmo, g)


def _ffn_up_body(*refs, tiles_per_seq, hist_period):
    if hist_period:
        h_ref, g_ref, wa_ref, wg_ref, cw_ref, p1_ref, p2_ref, f_ref, a_ref, xn_sc = refs
    else:
        h_ref, g_ref, wa_ref, wg_ref, cw_ref, f_ref, tail_ref, xn_sc, carry_sc = refs
    i = pl.program_id(0)
    j = pl.program_id(1)

    @pl.when(j == 0)
    def _():
        xn_sc[...] = _rms_norm(h_ref[...], g_ref[...]).astype(BF16)

    xn = xn_sc[...]
    a = jnp.dot(xn, wa_ref[...], preferred_element_type=F32)
    gate = jnp.dot(xn, wg_ref[...], preferred_element_type=F32)
    tm = a.shape[0]
    if hist_period:
        p1, p2 = _shifted_rows(a, p1_ref[...], p2_ref[...], p2_ref[...], hist_period)
        a_ref[...] = a
    else:
        @pl.when(i % tiles_per_seq == 0)
        def _():
            carry_sc[j] = jnp.zeros(carry_sc.shape[1:], F32)

        last1 = carry_sc[j, HALO - 1:HALO, :]
        last2 = carry_sc[j, HALO - 2:HALO - 1, :]
        p1, p2 = _shifted_rows(a, last1, last2, last1, tm)
        carry_sc[j] = a[tm - HALO:tm, :]
        tail_ref[...] = a[tm - HALO:tm, :]
    ua = cw_ref[0:1, :] * p2 + cw_ref[1:2, :] * p1 + cw_ref[2:3, :] * a
    f_ref[...] = (jax.nn.gelu(ua, approximate=True) * gate).astype(BF16)


def _ffn_up(h, g, w_up, cw, hist, seq, tm):
    t = h.shape[0]
    n_j = D_FF // TN_FF
    grid = (t // tm, n_j)
    in_specs = [
        pl.BlockSpec((tm, D_MODEL), lambda i, j: (i, 0)),
        pl.BlockSpec((1, D_MODEL), lambda i, j: (0, 0)),
        pl.BlockSpec((D_MODEL, TN_FF), lambda i, j: (0, j)),
        pl.BlockSpec((D_MODEL, TN_FF), lambda i, j: (0, j + n_j)),
        pl.BlockSpec((CONV_W, TN_FF), lambda i, j: (0, j)),
    ]
    args = [h, g, w_up, w_up, cw]
    scratch = [pltpu.VMEM((tm, D_MODEL), BF16)]
    f_spec = pl.BlockSpec((tm, TN_FF), lambda i, j: (i, j))
    f_shape = jax.ShapeDtypeStruct((t, D_FF), BF16)
    if hist is None:
        hist_period = 0
        scratch.append(pltpu.VMEM((n_j, HALO, TN_FF), F32))
        out_specs = (f_spec, pl.BlockSpec((None, HALO, TN_FF), lambda i, j: (i, 0, j)))
        out_shape = (f_shape, jax.ShapeDtypeStruct((t // tm, HALO, D_FF), F32))
    else:
        hist_period = seq
        in_specs += [pl.BlockSpec((tm, TN_FF), lambda i, j: (i, j))] * 2
        args += list(hist)
        out_specs = (f_spec, pl.BlockSpec((tm, TN_FF), lambda i, j: (i, j)))
        out_shape = (f_shape, jax.ShapeDtypeStruct((t, D_FF), F32))
    nbytes = (2 * tm * D_MODEL * 4 + tm * D_MODEL * 2 + 4 * D_MODEL * TN_FF * 2
              + 2 * tm * TN_FF * (2 + 4 * 3) + 8 * tm * TN_FF * 4)
    return pl.pallas_call(
        functools.partial(_ffn_up_body, tiles_per_seq=max(seq // tm, 1), hist_period=hist_period),
        grid=grid,
        in_specs=in_specs,
        out_specs=out_specs,
        out_shape=out_shape,
        scratch_shapes=scratch,
        compiler_params=_params(2, nbytes),
        name="ffn_up",
    )(*args)


def _ffn_down_body(f_ref, w_ref, h_ref, g_ref, o_ref):
    acc = jnp.dot(f_ref[...], w_ref[...], preferred_element_type=F32)
    o_ref[...] = h_ref[...] + _rms_norm(acc, g_ref[...])


def _ffn_down(f, w, h, g, tm):
    t = f.shape[0]
    nbytes = 2 * tm * (D_FF * 2 + D_MODEL * 4 * 2) + D_FF * D_MODEL * 2 + 2 * tm * D_MODEL * 4
    return pl.pallas_call(
        _ffn_down_body,
        grid=(t // tm,),
        in_specs=[
            pl.BlockSpec((tm, D_FF), lambda i: (i, 0)),
            pl.BlockSpec((D_FF, D_MODEL), lambda i: (0, 0), pipeline_mode=pl.Buffered(1)),
            pl.BlockSpec((tm, D_MODEL), lambda i: (i, 0)),
            pl.BlockSpec((1, D_MODEL), lambda i: (0, 0)),
        ],
        out_specs=pl.BlockSpec((tm, D_MODEL), lambda i: (i, 0)),
        out_shape=jax.ShapeDtypeStruct((t, D_MODEL), F32),
        compiler_params=_params(1, nbytes),
        name="ffn_down",
    )(f, w, h, g)


def _rope_tables(pos):
    half = ROT_DIM // 2
    inv = jnp.power(jnp.float32(ROPE_THETA), -jnp.arange(half, dtype=F32) * (2.0 / ROT_DIM))
    ang = pos.astype(F32)[:, None] * inv[None, :]
    cos = jnp.cos(ang)
    sin = jnp.sin(ang)
    rest = HEAD_DIM - ROT_DIM
    cos_h = jnp.concatenate([cos, cos, jnp.ones((pos.shape[0], rest), F32)], axis=1)
    sin_h = jnp.concatenate([-sin, sin, jnp.zeros((pos.shape[0], rest), F32)], axis=1)
    reps = LANES // HEAD_DIM
    return jnp.tile(cos_h, (1, reps)), jnp.tile(sin_h, (1, reps))


def _row_history(state, seq):
    b, _, c = state.shape
    p1 = jnp.concatenate([state[:, 1:2], jnp.zeros((b, seq - 1, c), F32)], axis=1)
    p2 = jnp.concatenate([state[:, 0:1], state[:, 1:2], jnp.zeros((b, seq - 2, c), F32)], axis=1)
    return p1.reshape(b * seq, c), p2.reshape(b * seq, c)


def kernel(x_prompt, x_sample, cache_k, cache_v, state_conv_mix, state_conv_ffn, norm_mix_pre, w_in,
           attn_sinks, conv_mix_w, w_attn_o, w_conv_o, w_mix_out, norm_mix_post, norm_ffn_pre,
           w_ffn_up, conv_ffn_w, w_ffn_down, norm_ffn_post):
    assert w_in.shape == (1, D_MODEL, D_IN), "one layer only"
    batch, seq, _ = x_prompt.shape
    dec_batch, dec_seq, _ = x_sample.shape
    assert seq % 1024 == 0 and (dec_batch * dec_seq) % 256 == 0 and dec_seq % 16 == 0
    assert cache_k.shape[2] == WINDOW

    g_pre = norm_mix_pre[0][None]
    g_post = norm_mix_post[0][None]
    g_ffn_pre = norm_ffn_pre[0][None]
    g_ffn_post = norm_ffn_post[0][None]
    w_in_b = w_in[0].astype(BF16)
    wao = w_attn_o[0].astype(BF16)
    wco = w_conv_o[0].astype(BF16)
    wmo = w_mix_out[0].astype(BF16)
    w_up = w_ffn_up[0].astype(BF16)
    w_down = w_ffn_down[0].astype(BF16)
    sinks = attn_sinks[0]
    cw_mix = conv_mix_w[0]
    cw_ffn = conv_ffn_w[0]

    tp = batch * seq
    xp = x_prompt.reshape(tp, D_MODEL)
    cos_p, sin_p = _rope_tables(jnp.arange(seq, dtype=F32))
    q, k, v, u, b, ga, gc = _inproj(xp, g_pre, cos_p, sin_p, w_in_b, tm=1024)
    oa = _attn_prompt(sinks, q, k, v, seq, tq=256)
    hp = _mix(xp, oa, u, b, ga, gc, None, cw_mix, wao, wco, wmo, g_post, seq, tm=256)
    f, a_tail = _ffn_up(hp, g_ffn_pre, w_up, cw_ffn, None, seq, tm=1024)
    yp = _ffn_down(f, w_down, hp, g_ffn_post, tm=256)

    keep = min(WINDOW, seq)
    new_k_prompt = k.reshape(batch, seq, N_KV_HEADS, HEAD_DIM)[:, seq - keep:][None]
    new_v_prompt = v.reshape(batch, seq, N_KV_HEADS, HEAD_DIM)[:, seq - keep:][None]
    conv_mix_prompt = u.reshape(batch, seq, D_CONV)[:, seq - (CONV_W - 1):][None]
    a_tail = a_tail.reshape(batch, seq // 1024, HALO, D_FF)
    conv_ffn_prompt = a_tail[:, -1, HALO - (CONV_W - 1):][None]

    ts = dec_batch * dec_seq
    xs = x_sample.reshape(ts, D_MODEL)
    pos_s = PAST_LEN + jnp.arange(dec_seq, dtype=F32)
    cos_s, sin_s = _rope_tables(jnp.tile(pos_s, dec_batch))
    qs, ks, vs, us, bs, gas, gcs = _inproj(xs, g_pre, cos_s, sin_s, w_in_b, tm=ts)
    n_valid = WINDOW + dec_seq
    n_keys = 2 * WINDOW
    pad = jnp.zeros((dec_batch, n_keys - n_valid, KV_W), F32)
    k_all = jnp.concatenate([cache_k[0].reshape(dec_batch, WINDOW, KV_W),
                             ks.reshape(dec_batch, dec_seq, KV_W), pad], axis=1)
    v_all = jnp.concatenate([cache_v[0].reshape(dec_batch, WINDOW, KV_W),
                             vs.reshape(dec_batch, dec_seq, KV_W), pad], axis=1)
    oas = _attn_sample(sinks, qs, k_all.reshape(dec_batch * n_keys, KV_W),
                       v_all.reshape(dec_batch * n_keys, KV_W), n_valid, tq=dec_seq)
    hs = _mix(xs, oas, us, bs, gas, gcs, _row_history(state_conv_mix[0], dec_seq),
              cw_mix, wao, wco, wmo, g_post, dec_seq, tm=ts)
    fs, a_s = _ffn_up(hs, g_ffn_pre, w_up, cw_ffn, _row_history(state_conv_ffn[0], dec_seq), dec_seq, tm=ts)
    ys = _ffn_down(fs, w_down, hs, g_ffn_post, tm=ts)

    new_k_sample = ks.reshape(dec_batch, dec_seq, N_KV_HEADS, HEAD_DIM)[None]
    new_v_sample = vs.reshape(dec_batch, dec_seq, N_KV_HEADS, HEAD_DIM)[None]
    conv_mix_sample = us.reshape(dec_batch, dec_seq, D_CONV)[:, dec_seq - (CONV_W - 1):][None]
    conv_ffn_sample = a_s.reshape(dec_batch, dec_seq, D_FF)[:, dec_seq - (CONV_W - 1):][None]

    return (yp.reshape(batch, seq, D_MODEL), ys.reshape(dec_batch, dec_seq, D_MODEL),
            new_k_prompt, new_v_prompt, conv_mix_prompt, conv_ffn_prompt,
            new_k_sample, new_v_sample, conv_mix_sample, conv_ffn_sample)
```

```python
import functools

import jax
import jax.numpy as jnp
from jax import lax
from jax.experimental import pallas as pl
from jax.experimental.pallas import tpu as pltpu

F32 = jnp.float32
BF16 = jnp.bfloat16

D_MODEL = 2048
CHUNK = 64
N_HEADS = 16
N_KV_HEADS = 4
GROUP = N_HEADS // N_KV_HEADS
HEAD_DIM = 64
ROT_DIM = HEAD_DIM // 4
ROPE_THETA = 500000.0
WINDOW = 128
D_CONV = 1024
CONV_W = 3
D_FF = 5632
EPS = 1e-6
NEG = -1e30
PAST_LEN = 4096
Q_W = N_HEADS * HEAD_DIM
KV_W = N_KV_HEADS * HEAD_DIM
D_IN = Q_W + 2 * KV_W + 3 * D_CONV + 2 * D_MODEL

LANES = 128
SUBLANES = 8
VMEM_LIMIT_CAP = 60 * 1024 * 1024
VMEM_COMPILER_SLACK = 6 * 1024 * 1024

TN_IN = 512
TN_FF = 512
ATT_BLK = 2 * CHUNK
HALO = SUBLANES
ROW_BLK = 256

_T_Q = (0, 2)
_T_KV = (2, 3)
_T_XIN = (3, 5)
_T_B = (5, 7)
_T_C = (7, 9)
_T_GA = (9, 13)
_T_GC = (13, 17)


def _vmem_limit(nbytes):
    return int(min(nbytes + VMEM_COMPILER_SLACK, VMEM_LIMIT_CAP))


def _params(n_axes, nbytes):
    return pltpu.CompilerParams(
        dimension_semantics=("arbitrary",) * n_axes,
        vmem_limit_bytes=_vmem_limit(nbytes),
    )


def _rms_norm(x, g):
    return x * lax.rsqrt(jnp.mean(x * x, axis=-1, keepdims=True) + EPS) * g


def _rope128(x, cos, sin):
    dh = lax.broadcasted_iota(jnp.int32, x.shape, 1) & (HEAD_DIM - 1)
    half = ROT_DIM // 2
    partner = jnp.where(dh < half, pltpu.roll(x, LANES - half, axis=1), pltpu.roll(x, half, axis=1))
    return x * cos + partner * sin


def _shifted_rows(a, prev1_row0, prev2_row0, prev2_row1, period):
    assert period & (period - 1) == 0
    rows = lax.broadcasted_iota(jnp.int32, (a.shape[0], 1), 0) & (period - 1)
    p1 = jnp.where(rows == 0, prev1_row0, pltpu.roll(a, 1, axis=0))
    p2 = jnp.where(rows == 0, prev2_row0, jnp.where(rows == 1, prev2_row1, pltpu.roll(a, 2, axis=0)))
    return p1, p2


def _inproj_body(x_ref, g_ref, cos_ref, sin_ref, w_ref,
                 q_ref, k_ref, v_ref, u_ref, b_ref, ga_ref, gc_ref, xn_sc, xin_sc):
    j = pl.program_id(1)
    tm = xn_sc.shape[0]
    rb = min(tm, ROW_BLK)

    @pl.when(j == 0)
    def _():
        xn_sc[...] = _rms_norm(x_ref[...], g_ref[...]).astype(BF16)

    def column_tile(finish):
        for r in range(tm // rb):
            rows = slice(r * rb, (r + 1) * rb)
            finish(rows, jnp.dot(xn_sc[rows, :], w_ref[...], preferred_element_type=F32))

    def rope_blocks(rows, acc, n_blocks):
        cos = cos_ref[rows, :]
        sin = sin_ref[rows, :]
        return [_rope128(acc[:, blk * LANES:(blk + 1) * LANES], cos, sin) for blk in range(n_blocks)]

    @pl.when(j < _T_Q[1])
    def _():
        def finish(rows, acc):
            for blk, val in enumerate(rope_blocks(rows, acc, TN_IN // LANES)):
                q_ref[rows, blk * LANES:(blk + 1) * LANES] = (val * (HEAD_DIM ** -0.5)).astype(BF16)
        column_tile(finish)

    @pl.when(j == _T_KV[0])
    def _():
        def finish(rows, acc):
            for blk, val in enumerate(rope_blocks(rows, acc, KV_W // LANES)):
                k_ref[rows, blk * LANES:(blk + 1) * LANES] = val
            v_ref[rows, :] = acc[:, KV_W:2 * KV_W]
        column_tile(finish)

    @pl.when((j >= _T_XIN[0]) & (j < _T_XIN[1]))
    def _():
        def finish(rows, acc):
            xin_sc[j - _T_XIN[0], rows, :] = acc
        column_tile(finish)

    @pl.when((j >= _T_B[0]) & (j < _T_B[1]))
    def _():
        def finish(rows, acc):
            b_ref[rows, :] = acc.astype(BF16)
        column_tile(finish)

    @pl.when((j >= _T_C[0]) & (j < _T_C[1]))
    def _():
        def finish(rows, acc):
            u_ref[rows, :] = acc * xin_sc[j - _T_C[0], rows, :]
        column_tile(finish)

    @pl.when((j >= _T_GA[0]) & (j < _T_GA[1]))
    def _():
        def finish(rows, acc):
            ga_ref[rows, :] = acc.astype(BF16)
        column_tile(finish)

    @pl.when(j >= _T_GC[0])
    def _():
        def finish(rows, acc):
            gc_ref[rows, :] = acc.astype(BF16)
        column_tile(finish)


def _seg_map(seg):
    lo, hi = seg
    return lambda i, j: (i, jnp.clip(j - lo, 0, hi - lo - 1))


def _inproj(x, g, cos, sin, w, tm):
    t = x.shape[0]
    n_pos = cos.shape[0] // tm
    grid = (t // tm, D_IN // TN_IN)
    out_shape = (
        jax.ShapeDtypeStruct((t, Q_W), BF16),
        jax.ShapeDtypeStruct((t, KV_W), F32),
        jax.ShapeDtypeStruct((t, KV_W), F32),
        jax.ShapeDtypeStruct((t, D_CONV), F32),
        jax.ShapeDtypeStruct((t, D_CONV), BF16),
        jax.ShapeDtypeStruct((t, D_MODEL), BF16),
        jax.ShapeDtypeStruct((t, D_MODEL), BF16),
    )
    in_specs = [
        pl.BlockSpec((tm, D_MODEL), lambda i, j: (i, 0)),
        pl.BlockSpec((1, D_MODEL), lambda i, j: (0, 0)),
        pl.BlockSpec((tm, LANES), lambda i, j: (i % n_pos, 0)),
        pl.BlockSpec((tm, LANES), lambda i, j: (i % n_pos, 0)),
        pl.BlockSpec((D_MODEL, TN_IN), lambda i, j: (0, j)),
    ]
    out_specs = (
        pl.BlockSpec((tm, TN_IN), _seg_map(_T_Q)),
        pl.BlockSpec((tm, KV_W), lambda i, j: (i, 0)),
        pl.BlockSpec((tm, KV_W), lambda i, j: (i, 0)),
        pl.BlockSpec((tm, TN_IN), _seg_map(_T_C)),
        pl.BlockSpec((tm, TN_IN), _seg_map(_T_B)),
        pl.BlockSpec((tm, TN_IN), _seg_map(_T_GA)),
        pl.BlockSpec((tm, TN_IN), _seg_map(_T_GC)),
    )
    nbytes = (2 * tm * D_MODEL * 4 + tm * D_MODEL * 2 + tm * D_CONV * 4 + 2 * D_MODEL * TN_IN * 2
              + 2 * tm * (TN_IN * 2 * 4 + KV_W * 4 * 2 + TN_IN * 4) + 4 * tm * LANES * 4
              + 2 * tm * TN_IN * 4)
    return pl.pallas_call(
        _inproj_body,
        grid=grid,
        in_specs=in_specs,
        out_specs=out_specs,
        out_shape=out_shape,
        scratch_shapes=[
            pltpu.VMEM((tm, D_MODEL), BF16),
            pltpu.VMEM((D_CONV // TN_IN, tm, TN_IN), F32),
        ],
        compiler_params=_params(2, nbytes),
        name="inproj",
    )(x, g, cos, sin, w)


def _head_pair_operands(x256, h):
    blk = x256[:, (h // 2) * LANES:(h // 2 + 1) * LANES]
    lo_lanes = lax.broadcasted_iota(jnp.int32, blk.shape, 1) < HEAD_DIM
    if h % 2 == 0:
        lo = jnp.where(lo_lanes, blk, 0.0)
        hi = pltpu.roll(lo, HEAD_DIM, axis=1)
    else:
        hi = jnp.where(lo_lanes, 0.0, blk)
        lo = pltpu.roll(hi, HEAD_DIM, axis=1)
    return jnp.concatenate([lo, hi], axis=0).astype(BF16)


def _attend(q_ref, q_rows, k_all, v_all, valid, sink_ref, o_ref):
    n_keys = k_all.shape[0]
    for h in range(N_KV_HEADS):
        kcat = _head_pair_operands(k_all, h)
        vcat = _head_pair_operands(v_all, h)
        for p in range(2 * h, 2 * h + 2):
            lanes = slice(p * LANES, (p + 1) * LANES)
            sc = lax.dot_general(q_ref[q_rows, lanes], kcat, (((1,), (1,)), ((), ())),
                                 preferred_element_type=F32)
            sc = jnp.where(valid, sc, NEG)
            probs = []
            for hh in range(2):
                s_h = sc[:, hh * n_keys:(hh + 1) * n_keys]
                sink = sink_ref[2 * p + hh]
                m = jnp.maximum(jnp.max(s_h, axis=-1, keepdims=True), sink)
                e = jnp.exp(s_h - m)
                den = jnp.sum(e, axis=-1, keepdims=True) + jnp.exp(sink - m)
                probs.append((e * (1.0 / den)).astype(BF16))
            pcat = jnp.concatenate(probs, axis=1)
            o_ref[q_rows, lanes] = jnp.dot(pcat, vcat, preferred_element_type=F32).astype(BF16)


def _attn_prompt_body(sink_ref, q_ref, kc_ref, kp_ref, vc_ref, vp_ref, o_ref, *, tiles_per_seq):
    first = (pl.program_id(0) % tiles_per_seq) == 0
    n_sub = q_ref.shape[0] // ATT_BLK
    n_keys = 2 * ATT_BLK
    shape = (ATT_BLK, 2 * n_keys)
    q_chunk = lax.broadcasted_iota(jnp.int32, shape, 0) // CHUNK
    k_chunk = (lax.broadcasted_iota(jnp.int32, shape, 1) & (n_keys - 1)) // CHUNK
    band = (k_chunk >= q_chunk) & (k_chunk <= q_chunk + WINDOW // CHUNK)
    first_valid = band & (k_chunk >= jnp.where(first, ATT_BLK // CHUNK, 0))
    for s in range(n_sub):
        rows = slice(s * ATT_BLK, (s + 1) * ATT_BLK)
        if s == 0:
            k_prev, v_prev, valid = kp_ref[...], vp_ref[...], first_valid
        else:
            prev = slice((s - 1) * ATT_BLK, s * ATT_BLK)
            k_prev, v_prev, valid = kc_ref[prev, :], vc_ref[prev, :], band
        k_all = jnp.concatenate([k_prev, kc_ref[rows, :]], axis=0)
        v_all = jnp.concatenate([v_prev, vc_ref[rows, :]], axis=0)
        _attend(q_ref, rows, k_all, v_all, valid, sink_ref, o_ref)


def _attn_prompt(sinks, q, k, v, seq, tq):
    t = q.shape[0]
    ratio = tq // ATT_BLK
    cur = lambda i: (i, 0)
    prev = lambda i: (jnp.maximum(i * ratio - 1, 0), 0)
    nbytes = 2 * (2 * tq * Q_W * 2 + 2 * (tq + ATT_BLK) * KV_W * 4) + 8 * ATT_BLK * 4 * ATT_BLK * 4
    return pl.pallas_call(
        functools.partial(_attn_prompt_body, tiles_per_seq=seq // tq),
        grid=(t // tq,),
        in_specs=[
            pl.BlockSpec(memory_space=pltpu.SMEM),
            pl.BlockSpec((tq, Q_W), cur),
            pl.BlockSpec((tq, KV_W), cur),
            pl.BlockSpec((ATT_BLK, KV_W), prev),
            pl.BlockSpec((tq, KV_W), cur),
            pl.BlockSpec((ATT_BLK, KV_W), prev),
        ],
        out_specs=pl.BlockSpec((tq, Q_W), cur),
        out_shape=jax.ShapeDtypeStruct((t, Q_W), BF16),
        compiler_params=_params(1, nbytes),
        name="attn_prompt",
    )(sinks, q, k, k, v, v)


def _attn_sample_body(sink_ref, q_ref, k_ref, v_ref, o_ref, *, n_valid):
    n_keys = k_ref.shape[0]
    shape = (q_ref.shape[0], 2 * n_keys)
    valid = (lax.broadcasted_iota(jnp.int32, shape, 1) & (n_keys - 1)) < n_valid
    _attend(q_ref, slice(None), k_ref[...], v_ref[...], valid, sink_ref, o_ref)


def _attn_sample(sinks, q, k_all, v_all, n_valid, tq):
    t = q.shape[0]
    n_keys = k_all.shape[0] // (t // tq)
    nbytes = 2 * (2 * tq * Q_W * 2 + 2 * n_keys * KV_W * 4) + 8 * tq * 2 * n_keys * 4 + 4 * n_keys * LANES * 4
    return pl.pallas_call(
        functools.partial(_attn_sample_body, n_valid=n_valid),
        grid=(t // tq,),
        in_specs=[
            pl.BlockSpec(memory_space=pltpu.SMEM),
            pl.BlockSpec((tq, Q_W), lambda i: (i, 0)),
            pl.BlockSpec((n_keys, KV_W), lambda i: (i, 0)),
            pl.BlockSpec((n_keys, KV_W), lambda i: (i, 0)),
        ],
        out_specs=pl.BlockSpec((tq, Q_W), lambda i: (i, 0)),
        out_shape=jax.ShapeDtypeStruct((t, Q_W), BF16),
        compiler_params=_params(1, nbytes),
        name="attn_sample",
    )(sinks, q, k_all, v_all)


def _mix_body(*refs, tiles_per_seq, hist_period):
    if hist_period:
        (x_ref, oa_ref, u_ref, b_ref, ga_ref, gc_ref, p1_ref, p2_ref,
         cw_ref, wao_ref, wco_ref, wmo_ref, g_ref, h_ref) = refs
    else:
        (x_ref, oa_ref, u_ref, b_ref, ga_ref, gc_ref,
         cw_ref, wao_ref, wco_ref, wmo_ref, g_ref, h_ref, carry_sc) = refs
    u = u_ref[...]
    tm = u.shape[0]
    if hist_period:
        p1, p2 = _shifted_rows(u, p1_ref[...], p2_ref[...], p2_ref[...], hist_period)
    else:
        @pl.when(pl.program_id(0) % tiles_per_seq == 0)
        def _():
            carry_sc[...] = jnp.zeros_like(carry_sc)

        last1 = carry_sc[HALO - 1:HALO, :]
        last2 = carry_sc[HALO - 2:HALO - 1, :]
        p1, p2 = _shifted_rows(u, last1, last2, last1, tm)
        carry_sc[...] = u[tm - HALO:tm, :]
    cy = cw_ref[0:1, :] * p2 + cw_ref[1:2, :] * p1 + cw_ref[2:3, :] * u
    y_conv = (b_ref[...].astype(F32) * cy).astype(BF16)
    attn_o = jnp.dot(oa_ref[...], wao_ref[...], preferred_element_type=F32)
    conv_o = jnp.dot(y_conv, wco_ref[...], preferred_element_type=F32)
    mixed = (jax.nn.sigmoid(ga_ref[...].astype(F32)) * attn_o
             + jax.nn.sigmoid(gc_ref[...].astype(F32)) * conv_o)
    mo = jnp.dot(mixed.astype(BF16), wmo_ref[...], preferred_element_type=F32)
    h_ref[...] = x_ref[...] + _rms_norm(mo, g_ref[...])


def _mix(x, oa, u, b, ga, gc, hist, cw, wao, wco, wmo, g, seq, tm):
    t = x.shape[0]
    row = lambda i: (i, 0)
    fixed = lambda i: (0, 0)
    resident = pl.Buffered(1)
    act_specs = [
        pl.BlockSpec((tm, D_MODEL), row),
        pl.BlockSpec((tm, Q_W), row),
        pl.BlockSpec((tm, D_CONV), row),
        pl.BlockSpec((tm, D_CONV), row),
        pl.BlockSpec((tm, D_MODEL), row),
        pl.BlockSpec((tm, D_MODEL), row),
    ]
    acts = [x, oa, u, b, ga, gc]
    scratch = []
    if hist is None:
        hist_period = 0
        scratch.append(pltpu.VMEM((HALO, D_CONV), F32))
    else:
        hist_period = seq
        act_specs += [pl.BlockSpec((tm, D_CONV), row)] * 2
        acts += list(hist)
    w_specs = [
        pl.BlockSpec((CONV_W, D_CONV), fixed),
        pl.BlockSpec((Q_W, D_MODEL), fixed, pipeline_mode=resident),
        pl.BlockSpec((D_CONV, D_MODEL), fixed, pipeline_mode=resident),
        pl.BlockSpec((D_MODEL, D_MODEL), fixed, pipeline_mode=resident),
        pl.BlockSpec((1, D_MODEL), fixed),
    ]
    row_bytes = D_MODEL * 4 * 2 + Q_W * 2 + D_CONV * (4 + 2) + D_MODEL * 2 * 2
    if hist is not None:
        row_bytes += 2 * D_CONV * 4
    nbytes = (2 * tm * row_bytes + (Q_W + D_CONV + D_MODEL) * D_MODEL * 2 + 6 * tm * D_MODEL * 4)
    return pl.pallas_call(
        functools.partial(_mix_body, tiles_per_seq=max(seq // tm, 1), hist_period=hist_period),
        grid=(t // tm,),
        in_specs=act_specs + w_specs,
        out_specs=pl.BlockSpec((tm, D_MODEL), row),
        out_shape=jax.ShapeDtypeStruct((t, D_MODEL), F32),
        scratch_shapes=scratch,
        compiler_params=_params(1, nbytes),
        name="mix",
    )(*acts, cw, wao, wco, wmo, g)


def _ffn_up_body(*refs, tiles_per_seq, hist_period):
    if hist_period:
        h_ref, g_ref, wa_ref, wg_ref, cw_ref, p1_ref, p2_ref, f_ref, a_ref, xn_sc = refs
    else:
        h_ref, g_ref, wa_ref, wg_ref, cw_ref, f_ref, tail_ref, xn_sc, carry_sc = refs
    i = pl.program_id(0)
    j = pl.program_id(1)

    @pl.when(j == 0)
    def _():
        xn_sc[...] = _rms_norm(h_ref[...], g_ref[...]).astype(BF16)

    if not hist_period:
        @pl.when(i % tiles_per_seq == 0)
        def _():
            carry_sc[j] = jnp.zeros(carry_sc.shape[1:], F32)

        tail = carry_sc[j]
    tm = xn_sc.shape[0]
    rb = min(tm, ROW_BLK)
    for r in range(tm // rb):
        rows = slice(r * rb, (r + 1) * rb)
        xn = xn_sc[rows, :]
        a = jnp.dot(xn, wa_ref[...], preferred_element_type=F32)
        gate = jnp.dot(xn, wg_ref[...], preferred_element_type=F32)
        if hist_period:
            p1, p2 = _shifted_rows(a, p1_ref[rows, :], p2_ref[rows, :], p2_ref[rows, :], hist_period)
            a_ref[rows, :] = a
        else:
            last1 = tail[HALO - 1:HALO, :]
            p1, p2 = _shifted_rows(a, last1, tail[HALO - 2:HALO - 1, :], last1, rb)
            tail = a[rb - HALO:rb, :]
        ua = cw_ref[0:1, :] * p2 + cw_ref[1:2, :] * p1 + cw_ref[2:3, :] * a
        f_ref[rows, :] = (jax.nn.gelu(ua, approximate=True) * gate).astype(BF16)
    if not hist_period:
        carry_sc[j] = tail
        tail_ref[...] = tail


def _ffn_up(h, g, w_up, cw, hist, seq, tm):
    t = h.shape[0]
    n_j = D_FF // TN_FF
    grid = (t // tm, n_j)
    in_specs = [
        pl.BlockSpec((tm, D_MODEL), lambda i, j: (i, 0)),
        pl.BlockSpec((1, D_MODEL), lambda i, j: (0, 0)),
        pl.BlockSpec((D_MODEL, TN_FF), lambda i, j: (0, j)),
        pl.BlockSpec((D_MODEL, TN_FF), lambda i, j: (0, j + n_j)),
        pl.BlockSpec((CONV_W, TN_FF), lambda i, j: (0, j)),
    ]
    args = [h, g, w_up, w_up, cw]
    scratch = [pltpu.VMEM((tm, D_MODEL), BF16)]
    f_spec = pl.BlockSpec((tm, TN_FF), lambda i, j: (i, j))
    f_shape = jax.ShapeDtypeStruct((t, D_FF), BF16)
    if hist is None:
        hist_period = 0
        scratch.append(pltpu.VMEM((n_j, HALO, TN_FF), F32))
        out_specs = (f_spec, pl.BlockSpec((None, HALO, TN_FF), lambda i, j: (i, 0, j)))
        out_shape = (f_shape, jax.ShapeDtypeStruct((t // tm, HALO, D_FF), F32))
    else:
        hist_period = seq
        in_specs += [pl.BlockSpec((tm, TN_FF), lambda i, j: (i, j))] * 2
        args += list(hist)
        out_specs = (f_spec, pl.BlockSpec((tm, TN_FF), lambda i, j: (i, j)))
        out_shape = (f_shape, jax.ShapeDtypeStruct((t, D_FF), F32))
    nbytes = (2 * tm * D_MODEL * 4 + tm * D_MODEL * 2 + 4 * D_MODEL * TN_FF * 2
              + 2 * tm * TN_FF * (2 + 4 * 3) + 8 * tm * TN_FF * 4)
    return pl.pallas_call(
        functools.partial(_ffn_up_body, tiles_per_seq=max(seq // tm, 1), hist_period=hist_period),
        grid=grid,
        in_specs=in_specs,
        out_specs=out_specs,
        out_shape=out_shape,
        scratch_shapes=scratch,
        compiler_params=_params(2, nbytes),
        name="ffn_up",
    )(*args)


def _ffn_down_body(f_ref, w_ref, h_ref, g_ref, o_ref):
    acc = jnp.dot(f_ref[...], w_ref[...], preferred_element_type=F32)
    o_ref[...] = h_ref[...] + _rms_norm(acc, g_ref[...])


def _ffn_down(f, w, h, g, tm):
    t = f.shape[0]
    nbytes = 2 * tm * (D_FF * 2 + D_MODEL * 4 * 2) + D_FF * D_MODEL * 2 + 2 * tm * D_MODEL * 4
    return pl.pallas_call(
        _ffn_down_body,
        grid=(t // tm,),
        in_specs=[
            pl.BlockSpec((tm, D_FF), lambda i: (i, 0)),
            pl.BlockSpec((D_FF, D_MODEL), lambda i: (0, 0), pipeline_mode=pl.Buffered(1)),
            pl.BlockSpec((tm, D_MODEL), lambda i: (i, 0)),
            pl.BlockSpec((1, D_MODEL), lambda i: (0, 0)),
        ],
        out_specs=pl.BlockSpec((tm, D_MODEL), lambda i: (i, 0)),
        out_shape=jax.ShapeDtypeStruct((t, D_MODEL), F32),
        compiler_params=_params(1, nbytes),
        name="ffn_down",
    )(f, w, h, g)


def _rope_tables(pos):
    half = ROT_DIM // 2
    inv = jnp.power(jnp.float32(ROPE_THETA), -jnp.arange(half, dtype=F32) * (2.0 / ROT_DIM))
    ang = pos.astype(F32)[:, None] * inv[None, :]
    cos = jnp.cos(ang)
    sin = jnp.sin(ang)
    rest = HEAD_DIM - ROT_DIM
    cos_h = jnp.concatenate([cos, cos, jnp.ones((pos.shape[0], rest), F32)], axis=1)
    sin_h = jnp.concatenate([-sin, sin, jnp.zeros((pos.shape[0], rest), F32)], axis=1)
    reps = LANES // HEAD_DIM
    return jnp.tile(cos_h, (1, reps)), jnp.tile(sin_h, (1, reps))


def _row_history(state, seq):
    b, _, c = state.shape
    p1 = jnp.concatenate([state[:, 1:2], jnp.zeros((b, seq - 1, c), F32)], axis=1)
    p2 = jnp.concatenate([state[:, 0:1], state[:, 1:2], jnp.zeros((b, seq - 2, c), F32)], axis=1)
    return p1.reshape(b * seq, c), p2.reshape(b * seq, c)


def kernel(x_prompt, x_sample, cache_k, cache_v, state_conv_mix, state_conv_ffn, norm_mix_pre, w_in,
           attn_sinks, conv_mix_w, w_attn_o, w_conv_o, w_mix_out, norm_mix_post, norm_ffn_pre,
           w_ffn_up, conv_ffn_w, w_ffn_down, norm_ffn_post):
    assert w_in.shape == (1, D_MODEL, D_IN), "one layer only"
    batch, seq, _ = x_prompt.shape
    dec_batch, dec_seq, _ = x_sample.shape
    assert seq % 1024 == 0 and (dec_batch * dec_seq) % 256 == 0 and dec_seq % 16 == 0
    assert cache_k.shape[2] == WINDOW

    g_pre = norm_mix_pre[0][None]
    g_post = norm_mix_post[0][None]
    g_ffn_pre = norm_ffn_pre[0][None]
    g_ffn_post = norm_ffn_post[0][None]
    w_in_b = w_in[0].astype(BF16)
    wao = w_attn_o[0].astype(BF16)
    wco = w_conv_o[0].astype(BF16)
    wmo = w_mix_out[0].astype(BF16)
    w_up = w_ffn_up[0].astype(BF16)
    w_down = w_ffn_down[0].astype(BF16)
    sinks = attn_sinks[0]
    cw_mix = conv_mix_w[0]
    cw_ffn = conv_ffn_w[0]

    tp = batch * seq
    xp = x_prompt.reshape(tp, D_MODEL)
    cos_p, sin_p = _rope_tables(jnp.arange(seq, dtype=F32))
    q, k, v, u, b, ga, gc = _inproj(xp, g_pre, cos_p, sin_p, w_in_b, tm=1024)
    oa = _attn_prompt(sinks, q, k, v, seq, tq=256)
    hp = _mix(xp, oa, u, b, ga, gc, None, cw_mix, wao, wco, wmo, g_post, seq, tm=256)
    f, a_tail = _ffn_up(hp, g_ffn_pre, w_up, cw_ffn, None, seq, tm=1024)
    yp = _ffn_down(f, w_down, hp, g_ffn_post, tm=256)

    keep = min(WINDOW, seq)
    new_k_prompt = k.reshape(batch, seq, N_KV_HEADS, HEAD_DIM)[:, seq - keep:][None]
    new_v_prompt = v.reshape(batch, seq, N_KV_HEADS, HEAD_DIM)[:, seq - keep:][None]
    conv_mix_prompt = u.reshape(batch, seq, D_CONV)[:, seq - (CONV_W - 1):][None]
    a_tail = a_tail.reshape(batch, seq // 1024, HALO, D_FF)
    conv_ffn_prompt = a_tail[:, -1, HALO - (CONV_W - 1):][None]

    ts = dec_batch * dec_seq
    xs = x_sample.reshape(ts, D_MODEL)
    pos_s = PAST_LEN + jnp.arange(dec_seq, dtype=F32)
    cos_s, sin_s = _rope_tables(jnp.tile(pos_s, dec_batch))
    qs, ks, vs, us, bs, gas, gcs = _inproj(xs, g_pre, cos_s, sin_s, w_in_b, tm=ts)
    n_valid = WINDOW + dec_seq
    n_keys = 2 * WINDOW
    pad = jnp.zeros((dec_batch, n_keys - n_valid, KV_W), F32)
    k_all = jnp.concatenate([cache_k[0].reshape(dec_batch, WINDOW, KV_W),
                             ks.reshape(dec_batch, dec_seq, KV_W), pad], axis=1)
    v_all = jnp.concatenate([cache_v[0].reshape(dec_batch, WINDOW, KV_W),
                             vs.reshape(dec_batch, dec_seq, KV_W), pad], axis=1)
    oas = _attn_sample(sinks, qs, k_all.reshape(dec_batch * n_keys, KV_W),
                       v_all.reshape(dec_batch * n_keys, KV_W), n_valid, tq=dec_seq)
    hs = _mix(xs, oas, us, bs, gas, gcs, _row_history(state_conv_mix[0], dec_seq),
              cw_mix, wao, wco, wmo, g_post, dec_seq, tm=ts)
    fs, a_s = _ffn_up(hs, g_ffn_pre, w_up, cw_ffn, _row_history(state_conv_ffn[0], dec_seq), dec_seq, tm=ts)
    ys = _ffn_down(fs, w_down, hs, g_ffn_post, tm=ts)

    new_k_sample = ks.reshape(dec_batch, dec_seq, N_KV_HEADS, HEAD_DIM)[None]
    new_v_sample = vs.reshape(dec_batch, dec_seq, N_KV_HEADS, HEAD_DIM)[None]
    conv_mix_sample = us.reshape(dec_batch, dec_seq, D_CONV)[:, dec_seq - (CONV_W - 1):][None]
    conv_ffn_sample = a_s.reshape(dec_batch, dec_seq, D_FF)[:, dec_seq - (CONV_W - 1):][None]

    return (yp.reshape(batch, seq, D_MODEL), ys.reshape(dec_batch, dec_seq, D_MODEL),
            new_k_prompt, new_v_prompt, conv_mix_prompt, conv_ffn_prompt,
            new_k_sample, new_v_sample, conv_mix_sample, conv_ffn_sample)
```

```python
import functools

import jax
import jax.numpy as jnp
from jax import lax
from jax.experimental import pallas as pl
from jax.experimental.pallas import tpu as pltpu

F32 = jnp.float32
BF16 = jnp.bfloat16

D_MODEL = 2048
CHUNK = 64
N_HEADS = 16
N_KV_HEADS = 4
GROUP = N_HEADS // N_KV_HEADS
HEAD_DIM = 64
ROT_DIM = HEAD_DIM // 4
ROPE_THETA = 500000.0
WINDOW = 128
D_CONV = 1024
CONV_W = 3
D_FF = 5632
EPS = 1e-6
NEG = -1e30
PAST_LEN = 4096
Q_W = N_HEADS * HEAD_DIM
KV_W = N_KV_HEADS * HEAD_DIM
D_IN = Q_W + 2 * KV_W + 3 * D_CONV + 2 * D_MODEL

LANES = 128
SUBLANES = 8
VMEM_LIMIT_CAP = 60 * 1024 * 1024
VMEM_COMPILER_SLACK = 6 * 1024 * 1024

TN_IN = 512
TN_FF = 512
ATT_BLK = 2 * CHUNK
HALO = SUBLANES
ROW_BLK = 256

_T_Q = (0, 2)
_T_KV = (2, 3)
_T_XIN = (3, 5)
_T_B = (5, 7)
_T_C = (7, 9)
_T_GA = (9, 13)
_T_GC = (13, 17)


def _vmem_limit(nbytes):
    return int(min(nbytes + VMEM_COMPILER_SLACK, VMEM_LIMIT_CAP))


def _params(n_axes, nbytes):
    return pltpu.CompilerParams(
        dimension_semantics=("arbitrary",) * n_axes,
        vmem_limit_bytes=_vmem_limit(nbytes),
    )


def _rms_norm(x, g):
    return x * lax.rsqrt(jnp.mean(x * x, axis=-1, keepdims=True) + EPS) * g


def _rope128(x, cos, sin):
    dh = lax.broadcasted_iota(jnp.int32, x.shape, 1) & (HEAD_DIM - 1)
    half = ROT_DIM // 2
    partner = jnp.where(dh < half, pltpu.roll(x, LANES - half, axis=1), pltpu.roll(x, half, axis=1))
    return x * cos + partner * sin


def _shifted_rows(a, prev1_row0, prev2_row0, prev2_row1, period):
    assert period & (period - 1) == 0
    rows = lax.broadcasted_iota(jnp.int32, (a.shape[0], 1), 0) & (period - 1)
    p1 = jnp.where(rows == 0, prev1_row0, pltpu.roll(a, 1, axis=0))
    p2 = jnp.where(rows == 0, prev2_row0, jnp.where(rows == 1, prev2_row1, pltpu.roll(a, 2, axis=0)))
    return p1, p2


def _inproj_body(x_ref, g_ref, cos_ref, sin_ref, w_ref,
                 q_ref, k_ref, v_ref, u_ref, b_ref, ga_ref, gc_ref, xn_sc, xin_sc):
    j = pl.program_id(1)
    tm = xn_sc.shape[0]
    rb = min(tm, ROW_BLK)

    @pl.when(j == 0)
    def _():
        xn_sc[...] = _rms_norm(x_ref[...], g_ref[...]).astype(BF16)

    def column_tile(finish):
        for r in range(tm // rb):
            rows = slice(r * rb, (r + 1) * rb)
            finish(rows, jnp.dot(xn_sc[rows, :], w_ref[...], preferred_element_type=F32))

    def rope_blocks(rows, acc, n_blocks):
        cos = cos_ref[rows, :]
        sin = sin_ref[rows, :]
        return [_rope128(acc[:, blk * LANES:(blk + 1) * LANES], cos, sin) for blk in range(n_blocks)]

    @pl.when(j < _T_Q[1])
    def _():
        def finish(rows, acc):
            for blk, val in enumerate(rope_blocks(rows, acc, TN_IN // LANES)):
                q_ref[rows, blk * LANES:(blk + 1) * LANES] = (val * (HEAD_DIM ** -0.5)).astype(BF16)
        column_tile(finish)

    @pl.when(j == _T_KV[0])
    def _():
        def finish(rows, acc):
            for blk, val in enumerate(rope_blocks(rows, acc, KV_W // LANES)):
                k_ref[rows, blk * LANES:(blk + 1) * LANES] = val
            v_ref[rows, :] = acc[:, KV_W:2 * KV_W]
        column_tile(finish)

    @pl.when((j >= _T_XIN[0]) & (j < _T_XIN[1]))
    def _():
        def finish(rows, acc):
            xin_sc[j - _T_XIN[0], rows, :] = acc
        column_tile(finish)

    @pl.when((j >= _T_B[0]) & (j < _T_B[1]))
    def _():
        def finish(rows, acc):
            b_ref[rows, :] = acc.astype(BF16)
        column_tile(finish)

    @pl.when((j >= _T_C[0]) & (j < _T_C[1]))
    def _():
        def finish(rows, acc):
            u_ref[rows, :] = acc * xin_sc[j - _T_C[0], rows, :]
        column_tile(finish)

    @pl.when((j >= _T_GA[0]) & (j < _T_GA[1]))
    def _():
        def finish(rows, acc):
            ga_ref[rows, :] = acc.astype(BF16)
        column_tile(finish)

    @pl.when(j >= _T_GC[0])
    def _():
        def finish(rows, acc):
            gc_ref[rows, :] = acc.astype(BF16)
        column_tile(finish)


def _seg_map(seg):
    lo, hi = seg
    return lambda i, j: (i, jnp.clip(j - lo, 0, hi - lo - 1))


def _inproj(x, g, cos, sin, w, tm):
    t = x.shape[0]
    n_pos = cos.shape[0] // tm
    grid = (t // tm, D_IN // TN_IN)
    out_shape = (
        jax.ShapeDtypeStruct((t, Q_W), BF16),
        jax.ShapeDtypeStruct((t, KV_W), F32),
        jax.ShapeDtypeStruct((t, KV_W), F32),
        jax.ShapeDtypeStruct((t, D_CONV), F32),
        jax.ShapeDtypeStruct((t, D_CONV), BF16),
        jax.ShapeDtypeStruct((t, D_MODEL), BF16),
        jax.ShapeDtypeStruct((t, D_MODEL), BF16),
    )
    in_specs = [
        pl.BlockSpec((tm, D_MODEL), lambda i, j: (i, 0)),
        pl.BlockSpec((1, D_MODEL), lambda i, j: (0, 0)),
        pl.BlockSpec((tm, LANES), lambda i, j: (i % n_pos, 0)),
        pl.BlockSpec((tm, LANES), lambda i, j: (i % n_pos, 0)),
        pl.BlockSpec((D_MODEL, TN_IN), lambda i, j: (0, j)),
    ]
    out_specs = (
        pl.BlockSpec((tm, TN_IN), _seg_map(_T_Q)),
        pl.BlockSpec((tm, KV_W), lambda i, j: (i, 0)),
        pl.BlockSpec((tm, KV_W), lambda i, j: (i, 0)),
        pl.BlockSpec((tm, TN_IN), _seg_map(_T_C)),
        pl.BlockSpec((tm, TN_IN), _seg_map(_T_B)),
        pl.BlockSpec((tm, TN_IN), _seg_map(_T_GA)),
        pl.BlockSpec((tm, TN_IN), _seg_map(_T_GC)),
    )
    nbytes = (2 * tm * D_MODEL * 4 + tm * D_MODEL * 2 + tm * D_CONV * 4 + 2 * D_MODEL * TN_IN * 2
              + 2 * tm * (TN_IN * 2 * 4 + KV_W * 4 * 2 + TN_IN * 4) + 4 * tm * LANES * 4
              + 2 * tm * TN_IN * 4)
    return pl.pallas_call(
        _inproj_body,
        grid=grid,
        in_specs=in_specs,
        out_specs=out_specs,
        out_shape=out_shape,
        scratch_shapes=[
            pltpu.VMEM((tm, D_MODEL), BF16),
            pltpu.VMEM((D_CONV // TN_IN, tm, TN_IN), F32),
        ],
        compiler_params=_params(2, nbytes),
        name="inproj",
    )(x, g, cos, sin, w)


def _head_pair_operands(x256, h):
    blk = x256[:, (h // 2) * LANES:(h // 2 + 1) * LANES]
    lo_lanes = lax.broadcasted_iota(jnp.int32, blk.shape, 1) < HEAD_DIM
    if h % 2 == 0:
        lo = jnp.where(lo_lanes, blk, 0.0)
        hi = pltpu.roll(lo, HEAD_DIM, axis=1)
    else:
        hi = jnp.where(lo_lanes, 0.0, blk)
        lo = pltpu.roll(hi, HEAD_DIM, axis=1)
    return jnp.concatenate([lo, hi], axis=0).astype(BF16)


def _attend(q_ref, o_ref, sink_ref, blocks, s_sc, m_sc, p_sc, v_sc):
    n_keys = blocks[0][1].shape[0]
    n_rows = s_sc.shape[1]
    key_row = lax.broadcasted_iota(jnp.int32, (2 * n_keys, LANES), 0)
    key_lane = lax.broadcasted_iota(jnp.int32, (2 * n_keys, LANES), 1)
    ones_cat = jnp.where((key_row >= n_keys) == (key_lane >= HEAD_DIM), 1.0, 0.0).astype(BF16)
    lo_lanes = lax.broadcasted_iota(jnp.int32, (n_rows, LANES), 1) < HEAD_DIM
    halves = [slice(hh * n_keys, (hh + 1) * n_keys) for hh in range(2)]

    tiles = []
    n_v = 0
    for q_rows, k_all, v_all, valid in blocks:
        for h in range(N_KV_HEADS):
            kcat = _head_pair_operands(k_all, h)
            v_sc[n_v] = _head_pair_operands(v_all, h)
            for p in range(2 * h, 2 * h + 2):
                lanes = slice(p * LANES, (p + 1) * LANES)
                sc = lax.dot_general(q_ref[q_rows, lanes], kcat, (((1,), (1,)), ((), ())),
                                     preferred_element_type=F32)
                s_sc[len(tiles)] = jnp.where(valid, sc, NEG)
                tiles.append((q_rows, lanes, p, n_v))
            n_v += 1
    for t, (_, _, p, _) in enumerate(tiles):
        for hh in range(2):
            m = jnp.maximum(jnp.max(s_sc[t, :, halves[hh]], axis=-1, keepdims=True), sink_ref[2 * p + hh])
            m_sc[2 * t + hh] = jnp.broadcast_to(m, (n_rows, LANES))
    for t in range(len(tiles)):
        p_sc[t] = jnp.concatenate(
            [jnp.exp(s_sc[t, :, halves[hh]] - jnp.tile(m_sc[2 * t + hh], (1, n_keys // LANES))).astype(BF16)
             for hh in range(2)], axis=1)
    for t, (q_rows, lanes, p, vi) in enumerate(tiles):
        e = p_sc[t]
        num = jnp.dot(e, v_sc[vi], preferred_element_type=F32)
        den = jnp.dot(e, ones_cat, preferred_element_type=F32)
        sink_term = jnp.where(lo_lanes, jnp.exp(sink_ref[2 * p] - m_sc[2 * t]),
                              jnp.exp(sink_ref[2 * p + 1] - m_sc[2 * t + 1]))
        o_ref[q_rows, lanes] = (num / (den + sink_term)).astype(BF16)


def _attn_scratch(n_blocks, n_rows, n_keys):
    n_tiles = n_blocks * N_HEADS // 2
    return [
        pltpu.VMEM((n_tiles, n_rows, 2 * n_keys), F32),
        pltpu.VMEM((2 * n_tiles, n_rows, LANES), F32),
        pltpu.VMEM((n_tiles, n_rows, 2 * n_keys), BF16),
        pltpu.VMEM((n_blocks * N_KV_HEADS, 2 * n_keys, LANES), BF16),
    ]


def _attn_scratch_bytes(n_blocks, n_rows, n_keys):
    n_tiles = n_blocks * N_HEADS // 2
    return (n_tiles * n_rows * 2 * n_keys * 6 + 2 * n_tiles * n_rows * LANES * 4
            + n_blocks * N_KV_HEADS * 2 * n_keys * LANES * 2)


def _attn_prompt_body(sink_ref, q_ref, kc_ref, kp_ref, vc_ref, vp_ref, o_ref, *scratch, tiles_per_seq):
    first = (pl.program_id(0) % tiles_per_seq) == 0
    n_sub = q_ref.shape[0] // ATT_BLK
    n_keys = 2 * ATT_BLK
    shape = (ATT_BLK, 2 * n_keys)
    q_chunk = lax.broadcasted_iota(jnp.int32, shape, 0) // CHUNK
    k_chunk = (lax.broadcasted_iota(jnp.int32, shape, 1) & (n_keys - 1)) // CHUNK
    band = (k_chunk >= q_chunk) & (k_chunk <= q_chunk + WINDOW // CHUNK)
    first_valid = band & (k_chunk >= jnp.where(first, ATT_BLK // CHUNK, 0))
    blocks = []
    for s in range(n_sub):
        rows = slice(s * ATT_BLK, (s + 1) * ATT_BLK)
        if s == 0:
            k_prev, v_prev, valid = kp_ref[...], vp_ref[...], first_valid
        else:
            prev = slice((s - 1) * ATT_BLK, s * ATT_BLK)
            k_prev, v_prev, valid = kc_ref[prev, :], vc_ref[prev, :], band
        k_all = jnp.concatenate([k_prev, kc_ref[rows, :]], axis=0)
        v_all = jnp.concatenate([v_prev, vc_ref[rows, :]], axis=0)
        blocks.append((rows, k_all, v_all, valid))
    _attend(q_ref, o_ref, sink_ref, blocks, *scratch)


def _attn_prompt(sinks, q, k, v, seq, tq):
    t = q.shape[0]
    ratio = tq // ATT_BLK
    cur = lambda i: (i, 0)
    prev = lambda i: (jnp.maximum(i * ratio - 1, 0), 0)
    nbytes = (2 * (2 * tq * Q_W * 2 + 2 * (tq + ATT_BLK) * KV_W * 4)
              + _attn_scratch_bytes(ratio, ATT_BLK, 2 * ATT_BLK))
    return pl.pallas_call(
        functools.partial(_attn_prompt_body, tiles_per_seq=seq // tq),
        grid=(t // tq,),
        in_specs=[
            pl.BlockSpec(memory_space=pltpu.SMEM),
            pl.BlockSpec((tq, Q_W), cur),
            pl.BlockSpec((tq, KV_W), cur),
            pl.BlockSpec((ATT_BLK, KV_W), prev),
            pl.BlockSpec((tq, KV_W), cur),
            pl.BlockSpec((ATT_BLK, KV_W), prev),
        ],
        out_specs=pl.BlockSpec((tq, Q_W), cur),
        out_shape=jax.ShapeDtypeStruct((t, Q_W), BF16),
        scratch_shapes=_attn_scratch(ratio, ATT_BLK, 2 * ATT_BLK),
        compiler_params=_params(1, nbytes),
        name="attn_prompt",
    )(sinks, q, k, k, v, v)


def _attn_sample_body(sink_ref, q_ref, k_ref, v_ref, o_ref, *scratch, n_valid):
    n_keys = k_ref.shape[0]
    shape = (q_ref.shape[0], 2 * n_keys)
    valid = (lax.broadcasted_iota(jnp.int32, shape, 1) & (n_keys - 1)) < n_valid
    _attend(q_ref, o_ref, sink_ref, [(slice(None), k_ref[...], v_ref[...], valid)], *scratch)


def _attn_sample(sinks, q, k_all, v_all, n_valid, tq):
    t = q.shape[0]
    n_keys = k_all.shape[0] // (t // tq)
    nbytes = 2 * (2 * tq * Q_W * 2 + 2 * n_keys * KV_W * 4) + _attn_scratch_bytes(1, tq, n_keys)
    return pl.pallas_call(
        functools.partial(_attn_sample_body, n_valid=n_valid),
        grid=(t // tq,),
        in_specs=[
            pl.BlockSpec(memory_space=pltpu.SMEM),
            pl.BlockSpec((tq, Q_W), lambda i: (i, 0)),
            pl.BlockSpec((n_keys, KV_W), lambda i: (i, 0)),
            pl.BlockSpec((n_keys, KV_W), lambda i: (i, 0)),
        ],
        out_specs=pl.BlockSpec((tq, Q_W), lambda i: (i, 0)),
        out_shape=jax.ShapeDtypeStruct((t, Q_W), BF16),
        scratch_shapes=_attn_scratch(1, tq, n_keys),
        compiler_params=_params(1, nbytes),
        name="attn_sample",
    )(sinks, q, k_all, v_all)


def _mix_body(*refs, tiles_per_seq, hist_period):
    if hist_period:
        (x_ref, oa_ref, u_ref, b_ref, ga_ref, gc_ref, p1_ref, p2_ref,
         cw_ref, wao_ref, wco_ref, wmo_ref, g_ref, h_ref) = refs
    else:
        (x_ref, oa_ref, u_ref, b_ref, ga_ref, gc_ref,
         cw_ref, wao_ref, wco_ref, wmo_ref, g_ref, h_ref, carry_sc) = refs
    u = u_ref[...]
    tm = u.shape[0]
    if hist_period:
        p1, p2 = _shifted_rows(u, p1_ref[...], p2_ref[...], p2_ref[...], hist_period)
    else:
        @pl.when(pl.program_id(0) % tiles_per_seq == 0)
        def _():
            carry_sc[...] = jnp.zeros_like(carry_sc)

        last1 = carry_sc[HALO - 1:HALO, :]
        last2 = carry_sc[HALO - 2:HALO - 1, :]
        p1, p2 = _shifted_rows(u, last1, last2, last1, tm)
        carry_sc[...] = u[tm - HALO:tm, :]
    cy = cw_ref[0:1, :] * p2 + cw_ref[1:2, :] * p1 + cw_ref[2:3, :] * u
    y_conv = (b_ref[...].astype(F32) * cy).astype(BF16)
    attn_o = jnp.dot(oa_ref[...], wao_ref[...], preferred_element_type=F32)
    conv_o = jnp.dot(y_conv, wco_ref[...], preferred_element_type=F32)
    mixed = (jax.nn.sigmoid(ga_ref[...].astype(F32)) * attn_o
             + jax.nn.sigmoid(gc_ref[...].astype(F32)) * conv_o)
    mo = jnp.dot(mixed.astype(BF16), wmo_ref[...], preferred_element_type=F32)
    h_ref[...] = x_ref[...] + _rms_norm(mo, g_ref[...])


def _mix(x, oa, u, b, ga, gc, hist, cw, wao, wco, wmo, g, seq, tm):
    t = x.shape[0]
    row = lambda i: (i, 0)
    fixed = lambda i: (0, 0)
    resident = pl.Buffered(1)
    act_specs = [
        pl.BlockSpec((tm, D_MODEL), row),
        pl.BlockSpec((tm, Q_W), row),
        pl.BlockSpec((tm, D_CONV), row),
        pl.BlockSpec((tm, D_CONV), row),
        pl.BlockSpec((tm, D_MODEL), row),
        pl.BlockSpec((tm, D_MODEL), row),
    ]
    acts = [x, oa, u, b, ga, gc]
    scratch = []
    if hist is None:
        hist_period = 0
        scratch.append(pltpu.VMEM((HALO, D_CONV), F32))
    else:
        hist_period = seq
        act_specs += [pl.BlockSpec((tm, D_CONV), row)] * 2
        acts += list(hist)
    w_specs = [
        pl.BlockSpec((CONV_W, D_CONV), fixed),
        pl.BlockSpec((Q_W, D_MODEL), fixed, pipeline_mode=resident),
        pl.BlockSpec((D_CONV, D_MODEL), fixed, pipeline_mode=resident),
        pl.BlockSpec((D_MODEL, D_MODEL), fixed, pipeline_mode=resident),
        pl.BlockSpec((1, D_MODEL), fixed),
    ]
    row_bytes = D_MODEL * 4 * 2 + Q_W * 2 + D_CONV * (4 + 2) + D_MODEL * 2 * 2
    if hist is not None:
        row_bytes += 2 * D_CONV * 4
    nbytes = (2 * tm * row_bytes + (Q_W + D_CONV + D_MODEL) * D_MODEL * 2 + 6 * tm * D_MODEL * 4)
    return pl.pallas_call(
        functools.partial(_mix_body, tiles_per_seq=max(seq // tm, 1), hist_period=hist_period),
        grid=(t // tm,),
        in_specs=act_specs + w_specs,
        out_specs=pl.BlockSpec((tm, D_MODEL), row),
        out_shape=jax.ShapeDtypeStruct((t, D_MODEL), F32),
        scratch_shapes=scratch,
        compiler_params=_params(1, nbytes),
        name="mix",
    )(*acts, cw, wao, wco, wmo, g)


def _ffn_up_body(*refs, tiles_per_seq, hist_period):
    if hist_period:
        h_ref, g_ref, wa_ref, wg_ref, cw_ref, p1_ref, p2_ref, f_ref, a_ref, xn_sc = refs
    else:
        h_ref, g_ref, wa_ref, wg_ref, cw_ref, f_ref, tail_ref, xn_sc, carry_sc = refs
    i = pl.program_id(0)
    j = pl.program_id(1)

    @pl.when(j == 0)
    def _():
        xn_sc[...] = _rms_norm(h_ref[...], g_ref[...]).astype(BF16)

    if not hist_period:
        @pl.when(i % tiles_per_seq == 0)
        def _():
            carry_sc[j] = jnp.zeros(carry_sc.shape[1:], F32)

        tail = carry_sc[j]
    tm = xn_sc.shape[0]
    rb = min(tm, ROW_BLK)
    for r in range(tm // rb):
        rows = slice(r * rb, (r + 1) * rb)
        xn = xn_sc[rows, :]
        a = jnp.dot(xn, wa_ref[...], preferred_element_type=F32)
        gate = jnp.dot(xn, wg_ref[...], preferred_element_type=F32)
        if hist_period:
            p1, p2 = _shifted_rows(a, p1_ref[rows, :], p2_ref[rows, :], p2_ref[rows, :], hist_period)
            a_ref[rows, :] = a
        else:
            last1 = tail[HALO - 1:HALO, :]
            p1, p2 = _shifted_rows(a, last1, tail[HALO - 2:HALO - 1, :], last1, rb)
            tail = a[rb - HALO:rb, :]
        ua = cw_ref[0:1, :] * p2 + cw_ref[1:2, :] * p1 + cw_ref[2:3, :] * a
        f_ref[rows, :] = (jax.nn.gelu(ua, approximate=True) * gate).astype(BF16)
    if not hist_period:
        carry_sc[j] = tail
        tail_ref[...] = tail


def _ffn_up(h, g, w_up, cw, hist, seq, tm):
    t = h.shape[0]
    n_j = D_FF // TN_FF
    grid = (t // tm, n_j)
    in_specs = [
        pl.BlockSpec((tm, D_MODEL), lambda i, j: (i, 0)),
        pl.BlockSpec((1, D_MODEL), lambda i, j: (0, 0)),
        pl.BlockSpec((D_MODEL, TN_FF), lambda i, j: (0, j)),
        pl.BlockSpec((D_MODEL, TN_FF), lambda i, j: (0, j + n_j)),
        pl.BlockSpec((CONV_W, TN_FF), lambda i, j: (0, j)),
    ]
    args = [h, g, w_up, w_up, cw]
    scratch = [pltpu.VMEM((tm, D_MODEL), BF16)]
    f_spec = pl.BlockSpec((tm, TN_FF), lambda i, j: (i, j))
    f_shape = jax.ShapeDtypeStruct((t, D_FF), BF16)
    if hist is None:
        hist_period = 0
        scratch.append(pltpu.VMEM((n_j, HALO, TN_FF), F32))
        out_specs = (f_spec, pl.BlockSpec((None, HALO, TN_FF), lambda i, j: (i, 0, j)))
        out_shape = (f_shape, jax.ShapeDtypeStruct((t // tm, HALO, D_FF), F32))
    else:
        hist_period = seq
        in_specs += [pl.BlockSpec((tm, TN_FF), lambda i, j: (i, j))] * 2
        args += list(hist)
        out_specs = (f_spec, pl.BlockSpec((tm, TN_FF), lambda i, j: (i, j)))
        out_shape = (f_shape, jax.ShapeDtypeStruct((t, D_FF), F32))
    nbytes = (2 * tm * D_MODEL * 4 + tm * D_MODEL * 2 + 4 * D_MODEL * TN_FF * 2
              + 2 * tm * TN_FF * (2 + 4 * 3) + 8 * tm * TN_FF * 4)
    return pl.pallas_call(
        functools.partial(_ffn_up_body, tiles_per_seq=max(seq // tm, 1), hist_period=hist_period),
        grid=grid,
        in_specs=in_specs,
        out_specs=out_specs,
        out_shape=out_shape,
        scratch_shapes=scratch,
        compiler_params=_params(2, nbytes),
        name="ffn_up",
    )(*args)


def _ffn_down_body(f_ref, w_ref, h_ref, g_ref, o_ref):
    acc = jnp.dot(f_ref[...], w_ref[...], preferred_element_type=F32)
    o_ref[...] = h_ref[...] + _rms_norm(acc, g_ref[...])


def _ffn_down(f, w, h, g, tm):
    t = f.shape[0]
    nbytes = 2 * tm * (D_FF * 2 + D_MODEL * 4 * 2) + D_FF * D_MODEL * 2 + 2 * tm * D_MODEL * 4
    return pl.pallas_call(
        _ffn_down_body,
        grid=(t // tm,),
        in_specs=[
            pl.BlockSpec((tm, D_FF), lambda i: (i, 0)),
            pl.BlockSpec((D_FF, D_MODEL), lambda i: (0, 0), pipeline_mode=pl.Buffered(1)),
            pl.BlockSpec((tm, D_MODEL), lambda i: (i, 0)),
            pl.BlockSpec((1, D_MODEL), lambda i: (0, 0)),
        ],
        out_specs=pl.BlockSpec((tm, D_MODEL), lambda i: (i, 0)),
        out_shape=jax.ShapeDtypeStruct((t, D_MODEL), F32),
        compiler_params=_params(1, nbytes),
        name="ffn_down",
    )(f, w, h, g)


def _rope_tables(pos):
    half = ROT_DIM // 2
    inv = jnp.power(jnp.float32(ROPE_THETA), -jnp.arange(half, dtype=F32) * (2.0 / ROT_DIM))
    ang = pos.astype(F32)[:, None] * inv[None, :]
    cos = jnp.cos(ang)
    sin = jnp.sin(ang)
    rest = HEAD_DIM - ROT_DIM
    cos_h = jnp.concatenate([cos, cos, jnp.ones((pos.shape[0], rest), F32)], axis=1)
    sin_h = jnp.concatenate([-sin, sin, jnp.zeros((pos.shape[0], rest), F32)], axis=1)
    reps = LANES // HEAD_DIM
    return jnp.tile(cos_h, (1, reps)), jnp.tile(sin_h, (1, reps))


def _row_history(state, seq):
    b, _, c = state.shape
    p1 = jnp.concatenate([state[:, 1:2], jnp.zeros((b, seq - 1, c), F32)], axis=1)
    p2 = jnp.concatenate([state[:, 0:1], state[:, 1:2], jnp.zeros((b, seq - 2, c), F32)], axis=1)
    return p1.reshape(b * seq, c), p2.reshape(b * seq, c)


def kernel(x_prompt, x_sample, cache_k, cache_v, state_conv_mix, state_conv_ffn, norm_mix_pre, w_in,
           attn_sinks, conv_mix_w, w_attn_o, w_conv_o, w_mix_out, norm_mix_post, norm_ffn_pre,
           w_ffn_up, conv_ffn_w, w_ffn_down, norm_ffn_post):
    assert w_in.shape == (1, D_MODEL, D_IN), "one layer only"
    batch, seq, _ = x_prompt.shape
    dec_batch, dec_seq, _ = x_sample.shape
    assert seq % 1024 == 0 and (dec_batch * dec_seq) % 256 == 0 and dec_seq % 16 == 0
    assert cache_k.shape[2] == WINDOW

    g_pre = norm_mix_pre[0][None]
    g_post = norm_mix_post[0][None]
    g_ffn_pre = norm_ffn_pre[0][None]
    g_ffn_post = norm_ffn_post[0][None]
    w_in_b = w_in[0].astype(BF16)
    wao = w_attn_o[0].astype(BF16)
    wco = w_conv_o[0].astype(BF16)
    wmo = w_mix_out[0].astype(BF16)
    w_up = w_ffn_up[0].astype(BF16)
    w_down = w_ffn_down[0].astype(BF16)
    sinks = attn_sinks[0]
    cw_mix = conv_mix_w[0]
    cw_ffn = conv_ffn_w[0]

    tp = batch * seq
    xp = x_prompt.reshape(tp, D_MODEL)
    cos_p, sin_p = _rope_tables(jnp.arange(seq, dtype=F32))
    q, k, v, u, b, ga, gc = _inproj(xp, g_pre, cos_p, sin_p, w_in_b, tm=1024)
    oa = _attn_prompt(sinks, q, k, v, seq, tq=256)
    hp = _mix(xp, oa, u, b, ga, gc, None, cw_mix, wao, wco, wmo, g_post, seq, tm=256)
    f, a_tail = _ffn_up(hp, g_ffn_pre, w_up, cw_ffn, None, seq, tm=1024)
    yp = _ffn_down(f, w_down, hp, g_ffn_post, tm=256)

    keep = min(WINDOW, seq)
    new_k_prompt = k.reshape(batch, seq, N_KV_HEADS, HEAD_DIM)[:, seq - keep:][None]
    new_v_prompt = v.reshape(batch, seq, N_KV_HEADS, HEAD_DIM)[:, seq - keep:][None]
    conv_mix_prompt = u.reshape(batch, seq, D_CONV)[:, seq - (CONV_W - 1):][None]
    a_tail = a_tail.reshape(batch, seq // 1024, HALO, D_FF)
    conv_ffn_prompt = a_tail[:, -1, HALO - (CONV_W - 1):][None]

    ts = dec_batch * dec_seq
    xs = x_sample.reshape(ts, D_MODEL)
    pos_s = PAST_LEN + jnp.arange(dec_seq, dtype=F32)
    cos_s, sin_s = _rope_tables(jnp.tile(pos_s, dec_batch))
    qs, ks, vs, us, bs, gas, gcs = _inproj(xs, g_pre, cos_s, sin_s, w_in_b, tm=ts)
    n_valid = WINDOW + dec_seq
    n_keys = 2 * WINDOW
    pad = jnp.zeros((dec_batch, n_keys - n_valid, KV_W), F32)
    k_all = jnp.concatenate([cache_k[0].reshape(dec_batch, WINDOW, KV_W),
                             ks.reshape(dec_batch, dec_seq, KV_W), pad], axis=1)
    v_all = jnp.concatenate([cache_v[0].reshape(dec_batch, WINDOW, KV_W),
                             vs.reshape(dec_batch, dec_seq, KV_W), pad], axis=1)
    oas = _attn_sample(sinks, qs, k_all.reshape(dec_batch * n_keys, KV_W),
                       v_all.reshape(dec_batch * n_keys, KV_W), n_valid, tq=dec_seq)
    hs = _mix(xs, oas, us, bs, gas, gcs, _row_history(state_conv_mix[0], dec_seq),
              cw_mix, wao, wco, wmo, g_post, dec_seq, tm=ts)
    fs, a_s = _ffn_up(hs, g_ffn_pre, w_up, cw_ffn, _row_history(state_conv_ffn[0], dec_seq), dec_seq, tm=ts)
    ys = _ffn_down(fs, w_down, hs, g_ffn_post, tm=ts)

    new_k_sample = ks.reshape(dec_batch, dec_seq, N_KV_HEADS, HEAD_DIM)[None]
    new_v_sample = vs.reshape(dec_batch, dec_seq, N_KV_HEADS, HEAD_DIM)[None]
    conv_mix_sample = us.reshape(dec_batch, dec_seq, D_CONV)[:, dec_seq - (CONV_W - 1):][None]
    conv_ffn_sample = a_s.reshape(dec_batch, dec_seq, D_FF)[:, dec_seq - (CONV_W - 1):][None]

    return (yp.reshape(batch, seq, D_MODEL), ys.reshape(dec_batch, dec_seq, D_MODEL),
            new_k_prompt, new_v_prompt, conv_mix_prompt, conv_ffn_prompt,
            new_k_sample, new_v_sample, conv_mix_sample, conv_ffn_sample)
```

```python
import functools

import jax
import jax.numpy as jnp
from jax import lax
from jax.experimental import pallas as pl
from jax.experimental.pallas import tpu as pltpu

F32 = jnp.float32
BF16 = jnp.bfloat16

D_MODEL = 2048
CHUNK = 64
N_HEADS = 16
N_KV_HEADS = 4
GROUP = N_HEADS // N_KV_HEADS
HEAD_DIM = 64
ROT_DIM = HEAD_DIM // 4
ROPE_THETA = 500000.0
WINDOW = 128
D_CONV = 1024
CONV_W = 3
D_FF = 5632
EPS = 1e-6
NEG = -1e30
PAST_LEN = 4096
Q_W = N_HEADS * HEAD_DIM
KV_W = N_KV_HEADS * HEAD_DIM
D_IN = Q_W + 2 * KV_W + 3 * D_CONV + 2 * D_MODEL

LANES = 128
SUBLANES = 8
VMEM_LIMIT_CAP = 60 * 1024 * 1024
VMEM_COMPILER_SLACK = 6 * 1024 * 1024

TN_IN = 512
TN_FF = 512
ATT_BLK = 2 * CHUNK
HALO = SUBLANES
ROW_BLK = 256

_T_Q = (0, 2)
_T_KV = (2, 3)
_T_XIN = (3, 5)
_T_B = (5, 7)
_T_C = (7, 9)
_T_GA = (9, 13)
_T_GC = (13, 17)


def _vmem_limit(nbytes):
    return int(min(nbytes + VMEM_COMPILER_SLACK, VMEM_LIMIT_CAP))


def _params(n_axes, nbytes):
    return pltpu.CompilerParams(
        dimension_semantics=("arbitrary",) * n_axes,
        vmem_limit_bytes=_vmem_limit(nbytes),
    )


def _rms_norm(x, g):
    return x * lax.rsqrt(jnp.mean(x * x, axis=-1, keepdims=True) + EPS) * g


def _rope128(x, cos, sin):
    dh = lax.broadcasted_iota(jnp.int32, x.shape, 1) & (HEAD_DIM - 1)
    half = ROT_DIM // 2
    partner = jnp.where(dh < half, pltpu.roll(x, LANES - half, axis=1), pltpu.roll(x, half, axis=1))
    return x * cos + partner * sin


def _shifted_rows(a, prev1_row0, prev2_row0, prev2_row1, period):
    assert period & (period - 1) == 0
    rows = lax.broadcasted_iota(jnp.int32, (a.shape[0], 1), 0) & (period - 1)
    p1 = jnp.where(rows == 0, prev1_row0, pltpu.roll(a, 1, axis=0))
    p2 = jnp.where(rows == 0, prev2_row0, jnp.where(rows == 1, prev2_row1, pltpu.roll(a, 2, axis=0)))
    return p1, p2


def _inproj_body(x_ref, g_ref, cos_ref, sin_ref, w_ref,
                 q_ref, k_ref, v_ref, u_ref, b_ref, ga_ref, gc_ref, xn_sc):
    xn_sc[...] = _rms_norm(x_ref[...], g_ref[...]).astype(BF16)
    cos = cos_ref[...]
    sin = sin_ref[...]

    def col(t):
        return jnp.dot(xn_sc[...], w_ref[:, t * TN_IN:(t + 1) * TN_IN], preferred_element_type=F32)

    def blocks(n):
        return [slice(blk * LANES, (blk + 1) * LANES) for blk in range(n)]

    for t in range(*_T_Q):
        acc = col(t)
        base = (t - _T_Q[0]) * TN_IN
        for sl in blocks(TN_IN // LANES):
            q_ref[:, base + sl.start:base + sl.stop] = (
                _rope128(acc[:, sl], cos, sin) * (HEAD_DIM ** -0.5)).astype(BF16)
    acc = col(_T_KV[0])
    for sl in blocks(KV_W // LANES):
        k_ref[:, sl] = _rope128(acc[:, sl], cos, sin)
    v_ref[...] = acc[:, KV_W:2 * KV_W]
    for c in range(_T_C[1] - _T_C[0]):
        cols = slice(c * TN_IN, (c + 1) * TN_IN)
        u_ref[:, cols] = col(_T_C[0] + c) * col(_T_XIN[0] + c)
        b_ref[:, cols] = col(_T_B[0] + c).astype(BF16)
    for c in range(_T_GA[1] - _T_GA[0]):
        cols = slice(c * TN_IN, (c + 1) * TN_IN)
        ga_ref[:, cols] = col(_T_GA[0] + c).astype(BF16)
        gc_ref[:, cols] = col(_T_GC[0] + c).astype(BF16)


def _inproj(x, g, cos, sin, w, tm):
    t = x.shape[0]
    n_pos = cos.shape[0] // tm
    row = lambda i: (i, 0)
    fixed = lambda i: (0, 0)
    pos = lambda i: (i % n_pos, 0)
    widths = (Q_W, KV_W, KV_W, D_CONV, D_CONV, D_MODEL, D_MODEL)
    dtypes = (BF16, F32, F32, F32, BF16, BF16, BF16)
    out_row_bytes = sum(wd * jnp.dtype(dt).itemsize for wd, dt in zip(widths, dtypes))
    nbytes = (D_MODEL * D_IN * 2 + 2 * tm * (D_MODEL * 4 + 2 * LANES * 4 + out_row_bytes)
              + tm * D_MODEL * 2 + 6 * tm * TN_IN * 4)
    return pl.pallas_call(
        _inproj_body,
        grid=(t // tm,),
        in_specs=[
            pl.BlockSpec((tm, D_MODEL), row),
            pl.BlockSpec((1, D_MODEL), fixed),
            pl.BlockSpec((tm, LANES), pos),
            pl.BlockSpec((tm, LANES), pos),
            pl.BlockSpec((D_MODEL, D_IN), fixed, pipeline_mode=pl.Buffered(1)),
        ],
        out_specs=tuple(pl.BlockSpec((tm, wd), row) for wd in widths),
        out_shape=tuple(jax.ShapeDtypeStruct((t, wd), dt) for wd, dt in zip(widths, dtypes)),
        scratch_shapes=[pltpu.VMEM((tm, D_MODEL), BF16)],
        compiler_params=_params(1, nbytes),
        name="inproj",
    )(x, g, cos, sin, w)


def _head_pair_operands(x256, h):
    blk = x256[:, (h // 2) * LANES:(h // 2 + 1) * LANES]
    lo_lanes = lax.broadcasted_iota(jnp.int32, blk.shape, 1) < HEAD_DIM
    if h % 2 == 0:
        lo = jnp.where(lo_lanes, blk, 0.0)
        hi = pltpu.roll(lo, HEAD_DIM, axis=1)
    else:
        hi = jnp.where(lo_lanes, 0.0, blk)
        lo = pltpu.roll(hi, HEAD_DIM, axis=1)
    return jnp.concatenate([lo, hi], axis=0).astype(BF16)


def _attend(q_ref, o_ref, sink_ref, blocks, s_sc, m_sc, p_sc, v_sc):
    n_keys = blocks[0][1].shape[0]
    n_rows = s_sc.shape[1]
    key_row = lax.broadcasted_iota(jnp.int32, (2 * n_keys, LANES), 0)
    key_lane = lax.broadcasted_iota(jnp.int32, (2 * n_keys, LANES), 1)
    ones_cat = jnp.where((key_row >= n_keys) == (key_lane >= HEAD_DIM), 1.0, 0.0).astype(BF16)
    lo_lanes = lax.broadcasted_iota(jnp.int32, (n_rows, LANES), 1) < HEAD_DIM
    halves = [slice(hh * n_keys, (hh + 1) * n_keys) for hh in range(2)]

    tiles = []
    n_v = 0
    for q_rows, k_all, v_all, valid in blocks:
        for h in range(N_KV_HEADS):
            kcat = _head_pair_operands(k_all, h)
            v_sc[n_v] = _head_pair_operands(v_all, h)
            for p in range(2 * h, 2 * h + 2):
                lanes = slice(p * LANES, (p + 1) * LANES)
                sc = lax.dot_general(q_ref[q_rows, lanes], kcat, (((1,), (1,)), ((), ())),
                                     preferred_element_type=F32)
                s_sc[len(tiles)] = jnp.where(valid, sc, NEG)
                tiles.append((q_rows, lanes, p, n_v))
            n_v += 1
    for t, (_, _, p, _) in enumerate(tiles):
        for hh in range(2):
            m = jnp.maximum(jnp.max(s_sc[t, :, halves[hh]], axis=-1, keepdims=True), sink_ref[2 * p + hh])
            m_sc[2 * t + hh] = jnp.broadcast_to(m, (n_rows, LANES))
    for t in range(len(tiles)):
        p_sc[t] = jnp.concatenate(
            [jnp.exp(s_sc[t, :, halves[hh]] - jnp.tile(m_sc[2 * t + hh], (1, n_keys // LANES))).astype(BF16)
             for hh in range(2)], axis=1)
    for t, (q_rows, lanes, p, vi) in enumerate(tiles):
        e = p_sc[t]
        num = jnp.dot(e, v_sc[vi], preferred_element_type=F32)
        den = jnp.dot(e, ones_cat, preferred_element_type=F32)
        sink_term = jnp.where(lo_lanes, jnp.exp(sink_ref[2 * p] - m_sc[2 * t]),
                              jnp.exp(sink_ref[2 * p + 1] - m_sc[2 * t + 1]))
        o_ref[q_rows, lanes] = (num / (den + sink_term)).astype(BF16)


def _attn_scratch(n_blocks, n_rows, n_keys):
    n_tiles = n_blocks * N_HEADS // 2
    return [
        pltpu.VMEM((n_tiles, n_rows, 2 * n_keys), F32),
        pltpu.VMEM((2 * n_tiles, n_rows, LANES), F32),
        pltpu.VMEM((n_tiles, n_rows, 2 * n_keys), BF16),
        pltpu.VMEM((n_blocks * N_KV_HEADS, 2 * n_keys, LANES), BF16),
    ]


def _attn_scratch_bytes(n_blocks, n_rows, n_keys):
    n_tiles = n_blocks * N_HEADS // 2
    return (n_tiles * n_rows * 2 * n_keys * 6 + 2 * n_tiles * n_rows * LANES * 4
            + n_blocks * N_KV_HEADS * 2 * n_keys * LANES * 2)


def _attn_prompt_body(sink_ref, q_ref, kc_ref, kp_ref, vc_ref, vp_ref, o_ref, *scratch, tiles_per_seq):
    first = (pl.program_id(0) % tiles_per_seq) == 0
    n_sub = q_ref.shape[0] // ATT_BLK
    n_keys = 2 * ATT_BLK
    shape = (ATT_BLK, 2 * n_keys)
    q_chunk = lax.broadcasted_iota(jnp.int32, shape, 0) // CHUNK
    k_chunk = (lax.broadcasted_iota(jnp.int32, shape, 1) & (n_keys - 1)) // CHUNK
    band = (k_chunk >= q_chunk) & (k_chunk <= q_chunk + WINDOW // CHUNK)
    first_valid = band & (k_chunk >= jnp.where(first, ATT_BLK // CHUNK, 0))
    blocks = []
    for s in range(n_sub):
        rows = slice(s * ATT_BLK, (s + 1) * ATT_BLK)
        if s == 0:
            k_prev, v_prev, valid = kp_ref[...], vp_ref[...], first_valid
        else:
            prev = slice((s - 1) * ATT_BLK, s * ATT_BLK)
            k_prev, v_prev, valid = kc_ref[prev, :], vc_ref[prev, :], band
        k_all = jnp.concatenate([k_prev, kc_ref[rows, :]], axis=0)
        v_all = jnp.concatenate([v_prev, vc_ref[rows, :]], axis=0)
        blocks.append((rows, k_all, v_all, valid))
    _attend(q_ref, o_ref, sink_ref, blocks, *scratch)


def _attn_prompt(sinks, q, k, v, seq, tq):
    t = q.shape[0]
    ratio = tq // ATT_BLK
    cur = lambda i: (i, 0)
    prev = lambda i: (jnp.maximum(i * ratio - 1, 0), 0)
    nbytes = (2 * (2 * tq * Q_W * 2 + 2 * (tq + ATT_BLK) * KV_W * 4)
              + _attn_scratch_bytes(ratio, ATT_BLK, 2 * ATT_BLK))
    return pl.pallas_call(
        functools.partial(_attn_prompt_body, tiles_per_seq=seq // tq),
        grid=(t // tq,),
        in_specs=[
            pl.BlockSpec(memory_space=pltpu.SMEM),
            pl.BlockSpec((tq, Q_W), cur),
            pl.BlockSpec((tq, KV_W), cur),
            pl.BlockSpec((ATT_BLK, KV_W), prev),
            pl.BlockSpec((tq, KV_W), cur),
            pl.BlockSpec((ATT_BLK, KV_W), prev),
        ],
        out_specs=pl.BlockSpec((tq, Q_W), cur),
        out_shape=jax.ShapeDtypeStruct((t, Q_W), BF16),
        scratch_shapes=_attn_scratch(ratio, ATT_BLK, 2 * ATT_BLK),
        compiler_params=_params(1, nbytes),
        name="attn_prompt",
    )(sinks, q, k, k, v, v)


def _attn_sample_body(sink_ref, q_ref, k_ref, v_ref, o_ref, *scratch, n_valid, n_keys, rows_per_seq):
    shape = (rows_per_seq, 2 * n_keys)
    valid = (lax.broadcasted_iota(jnp.int32, shape, 1) & (n_keys - 1)) < n_valid
    blocks = []
    for b in range(q_ref.shape[0] // rows_per_seq):
        keys = slice(b * n_keys, (b + 1) * n_keys)
        blocks.append((slice(b * rows_per_seq, (b + 1) * rows_per_seq), k_ref[keys, :], v_ref[keys, :], valid))
    _attend(q_ref, o_ref, sink_ref, blocks, *scratch)


def _attn_sample(sinks, q, k_all, v_all, n_valid, n_keys, rows_per_seq, seqs_per_step):
    t = q.shape[0]
    tq = rows_per_seq * seqs_per_step
    tk = n_keys * seqs_per_step
    nbytes = 2 * (2 * tq * Q_W * 2 + 2 * tk * KV_W * 4) + _attn_scratch_bytes(seqs_per_step, rows_per_seq, n_keys)
    return pl.pallas_call(
        functools.partial(_attn_sample_body, n_valid=n_valid, n_keys=n_keys, rows_per_seq=rows_per_seq),
        grid=(t // tq,),
        in_specs=[
            pl.BlockSpec(memory_space=pltpu.SMEM),
            pl.BlockSpec((tq, Q_W), lambda i: (i, 0)),
            pl.BlockSpec((tk, KV_W), lambda i: (i, 0)),
            pl.BlockSpec((tk, KV_W), lambda i: (i, 0)),
        ],
        out_specs=pl.BlockSpec((tq, Q_W), lambda i: (i, 0)),
        out_shape=jax.ShapeDtypeStruct((t, Q_W), BF16),
        scratch_shapes=_attn_scratch(seqs_per_step, rows_per_seq, n_keys),
        compiler_params=_params(1, nbytes),
        name="attn_sample",
    )(sinks, q, k_all, v_all)


def _mix_body(*refs, tiles_per_seq, hist_period):
    if hist_period:
        (x_ref, oa_ref, u_ref, b_ref, ga_ref, gc_ref, p1_ref, p2_ref,
         cw_ref, wao_ref, wco_ref, wmo_ref, g_ref, h_ref) = refs
    else:
        (x_ref, oa_ref, u_ref, b_ref, ga_ref, gc_ref,
         cw_ref, wao_ref, wco_ref, wmo_ref, g_ref, h_ref, carry_sc) = refs
    u = u_ref[...]
    tm = u.shape[0]
    if hist_period:
        p1, p2 = _shifted_rows(u, p1_ref[...], p2_ref[...], p2_ref[...], hist_period)
    else:
        @pl.when(pl.program_id(0) % tiles_per_seq == 0)
        def _():
            carry_sc[...] = jnp.zeros_like(carry_sc)

        last1 = carry_sc[HALO - 1:HALO, :]
        last2 = carry_sc[HALO - 2:HALO - 1, :]
        p1, p2 = _shifted_rows(u, last1, last2, last1, tm)
        carry_sc[...] = u[tm - HALO:tm, :]
    cy = cw_ref[0:1, :] * p2 + cw_ref[1:2, :] * p1 + cw_ref[2:3, :] * u
    y_conv = (b_ref[...].astype(F32) * cy).astype(BF16)
    attn_o = jnp.dot(oa_ref[...], wao_ref[...], preferred_element_type=F32)
    conv_o = jnp.dot(y_conv, wco_ref[...], preferred_element_type=F32)
    mixed = (jax.nn.sigmoid(ga_ref[...].astype(F32)) * attn_o
             + jax.nn.sigmoid(gc_ref[...].astype(F32)) * conv_o)
    mo = jnp.dot(mixed.astype(BF16), wmo_ref[...], preferred_element_type=F32)
    h_ref[...] = x_ref[...] + _rms_norm(mo, g_ref[...])


def _mix(x, oa, u, b, ga, gc, hist, cw, wao, wco, wmo, g, seq, tm):
    t = x.shape[0]
    row = lambda i: (i, 0)
    fixed = lambda i: (0, 0)
    resident = pl.Buffered(1)
    act_specs = [
        pl.BlockSpec((tm, D_MODEL), row),
        pl.BlockSpec((tm, Q_W), row),
        pl.BlockSpec((tm, D_CONV), row),
        pl.BlockSpec((tm, D_CONV), row),
        pl.BlockSpec((tm, D_MODEL), row),
        pl.BlockSpec((tm, D_MODEL), row),
    ]
    acts = [x, oa, u, b, ga, gc]
    scratch = []
    if hist is None:
        hist_period = 0
        scratch.append(pltpu.VMEM((HALO, D_CONV), F32))
    else:
        hist_period = seq
        act_specs += [pl.BlockSpec((tm, D_CONV), row)] * 2
        acts += list(hist)
    w_specs = [
        pl.BlockSpec((CONV_W, D_CONV), fixed),
        pl.BlockSpec((Q_W, D_MODEL), fixed, pipeline_mode=resident),
        pl.BlockSpec((D_CONV, D_MODEL), fixed, pipeline_mode=resident),
        pl.BlockSpec((D_MODEL, D_MODEL), fixed, pipeline_mode=resident),
        pl.BlockSpec((1, D_MODEL), fixed),
    ]
    row_bytes = D_MODEL * 4 * 2 + Q_W * 2 + D_CONV * (4 + 2) + D_MODEL * 2 * 2
    if hist is not None:
        row_bytes += 2 * D_CONV * 4
    nbytes = (2 * tm * row_bytes + (Q_W + D_CONV + D_MODEL) * D_MODEL * 2 + 6 * tm * D_MODEL * 4)
    return pl.pallas_call(
        functools.partial(_mix_body, tiles_per_seq=max(seq // tm, 1), hist_period=hist_period),
        grid=(t // tm,),
        in_specs=act_specs + w_specs,
        out_specs=pl.BlockSpec((tm, D_MODEL), row),
        out_shape=jax.ShapeDtypeStruct((t, D_MODEL), F32),
        scratch_shapes=scratch,
        compiler_params=_params(1, nbytes),
        name="mix",
    )(*acts, cw, wao, wco, wmo, g)


def _ffn_up_body(*refs, tiles_per_seq, hist_period):
    if hist_period:
        h_ref, g_ref, wa_ref, wg_ref, cw_ref, p1_ref, p2_ref, f_ref, a_ref, xn_sc = refs
    else:
        h_ref, g_ref, wa_ref, wg_ref, cw_ref, f_ref, tail_ref, xn_sc, carry_sc = refs
    i = pl.program_id(0)
    j = pl.program_id(1)

    @pl.when(j == 0)
    def _():
        xn_sc[...] = _rms_norm(h_ref[...], g_ref[...]).astype(BF16)

    if not hist_period:
        @pl.when(i % tiles_per_seq == 0)
        def _():
            carry_sc[j] = jnp.zeros(carry_sc.shape[1:], F32)

        tail = carry_sc[j]
    tm = xn_sc.shape[0]
    rb = min(tm, ROW_BLK)
    for r in range(tm // rb):
        rows = slice(r * rb, (r + 1) * rb)
        xn = xn_sc[rows, :]
        a = jnp.dot(xn, wa_ref[...], preferred_element_type=F32)
        gate = jnp.dot(xn, wg_ref[...], preferred_element_type=F32)
        if hist_period:
            p1, p2 = _shifted_rows(a, p1_ref[rows, :], p2_ref[rows, :], p2_ref[rows, :], hist_period)
            a_ref[rows, :] = a
        else:
            last1 = tail[HALO - 1:HALO, :]
            p1, p2 = _shifted_rows(a, last1, tail[HALO - 2:HALO - 1, :], last1, rb)
            tail = a[rb - HALO:rb, :]
        ua = cw_ref[0:1, :] * p2 + cw_ref[1:2, :] * p1 + cw_ref[2:3, :] * a
        f_ref[rows, :] = (jax.nn.gelu(ua, approximate=True) * gate).astype(BF16)
    if not hist_period:
        carry_sc[j] = tail
        tail_ref[...] = tail


def _ffn_up(h, g, w_up, cw, hist, seq, tm):
    t = h.shape[0]
    n_j = D_FF // TN_FF
    grid = (t // tm, n_j)
    in_specs = [
        pl.BlockSpec((tm, D_MODEL), lambda i, j: (i, 0)),
        pl.BlockSpec((1, D_MODEL), lambda i, j: (0, 0)),
        pl.BlockSpec((D_MODEL, TN_FF), lambda i, j: (0, j)),
        pl.BlockSpec((D_MODEL, TN_FF), lambda i, j: (0, j + n_j)),
        pl.BlockSpec((CONV_W, TN_FF), lambda i, j: (0, j)),
    ]
    args = [h, g, w_up, w_up, cw]
    scratch = [pltpu.VMEM((tm, D_MODEL), BF16)]
    f_spec = pl.BlockSpec((tm, TN_FF), lambda i, j: (i, j))
    f_shape = jax.ShapeDtypeStruct((t, D_FF), BF16)
    if hist is None:
        hist_period = 0
        scratch.append(pltpu.VMEM((n_j, HALO, TN_FF), F32))
        out_specs = (f_spec, pl.BlockSpec((None, HALO, TN_FF), lambda i, j: (i, 0, j)))
        out_shape = (f_shape, jax.ShapeDtypeStruct((t // tm, HALO, D_FF), F32))
    else:
        hist_period = seq
        in_specs += [pl.BlockSpec((tm, TN_FF), lambda i, j: (i, j))] * 2
        args += list(hist)
        out_specs = (f_spec, pl.BlockSpec((tm, TN_FF), lambda i, j: (i, j)))
        out_shape = (f_shape, jax.ShapeDtypeStruct((t, D_FF), F32))
    nbytes = (2 * tm * D_MODEL * 4 + tm * D_MODEL * 2 + 4 * D_MODEL * TN_FF * 2
              + 2 * tm * TN_FF * (2 + 4 * 3) + 8 * tm * TN_FF * 4)
    return pl.pallas_call(
        functools.partial(_ffn_up_body, tiles_per_seq=max(seq // tm, 1), hist_period=hist_period),
        grid=grid,
        in_specs=in_specs,
        out_specs=out_specs,
        out_shape=out_shape,
        scratch_shapes=scratch,
        compiler_params=_params(2, nbytes),
        name="ffn_up",
    )(*args)


def _ffn_down_body(f_ref, w_ref, h_ref, g_ref, o_ref):
    acc = jnp.dot(f_ref[...], w_ref[...], preferred_element_type=F32)
    o_ref[...] = h_ref[...] + _rms_norm(acc, g_ref[...])


def _ffn_down(f, w, h, g, tm):
    t = f.shape[0]
    nbytes = 2 * tm * (D_FF * 2 + D_MODEL * 4 * 2) + D_FF * D_MODEL * 2 + 2 * tm * D_MODEL * 4
    return pl.pallas_call(
        _ffn_down_body,
        grid=(t // tm,),
        in_specs=[
            pl.BlockSpec((tm, D_FF), lambda i: (i, 0)),
            pl.BlockSpec((D_FF, D_MODEL), lambda i: (0, 0), pipeline_mode=pl.Buffered(1)),
            pl.BlockSpec((tm, D_MODEL), lambda i: (i, 0)),
            pl.BlockSpec((1, D_MODEL), lambda i: (0, 0)),
        ],
        out_specs=pl.BlockSpec((tm, D_MODEL), lambda i: (i, 0)),
        out_shape=jax.ShapeDtypeStruct((t, D_MODEL), F32),
        compiler_params=_params(1, nbytes),
        name="ffn_down",
    )(f, w, h, g)


def _rope_tables(pos):
    half = ROT_DIM // 2
    inv = jnp.power(jnp.float32(ROPE_THETA), -jnp.arange(half, dtype=F32) * (2.0 / ROT_DIM))
    ang = pos.astype(F32)[:, None] * inv[None, :]
    cos = jnp.cos(ang)
    sin = jnp.sin(ang)
    rest = HEAD_DIM - ROT_DIM
    cos_h = jnp.concatenate([cos, cos, jnp.ones((pos.shape[0], rest), F32)], axis=1)
    sin_h = jnp.concatenate([-sin, sin, jnp.zeros((pos.shape[0], rest), F32)], axis=1)
    reps = LANES // HEAD_DIM
    return jnp.tile(cos_h, (1, reps)), jnp.tile(sin_h, (1, reps))


def _row_history(state, seq):
    b, _, c = state.shape
    p1 = jnp.concatenate([state[:, 1:2], jnp.zeros((b, seq - 1, c), F32)], axis=1)
    p2 = jnp.concatenate([state[:, 0:1], state[:, 1:2], jnp.zeros((b, seq - 2, c), F32)], axis=1)
    return p1.reshape(b * seq, c), p2.reshape(b * seq, c)


def kernel(x_prompt, x_sample, cache_k, cache_v, state_conv_mix, state_conv_ffn, norm_mix_pre, w_in,
           attn_sinks, conv_mix_w, w_attn_o, w_conv_o, w_mix_out, norm_mix_post, norm_ffn_pre,
           w_ffn_up, conv_ffn_w, w_ffn_down, norm_ffn_post):
    assert w_in.shape == (1, D_MODEL, D_IN), "one layer only"
    batch, seq, _ = x_prompt.shape
    dec_batch, dec_seq, _ = x_sample.shape
    assert seq % 1024 == 0 and (dec_batch * dec_seq) % 256 == 0 and dec_seq % 16 == 0
    assert cache_k.shape[2] == WINDOW

    g_pre = norm_mix_pre[0][None]
    g_post = norm_mix_post[0][None]
    g_ffn_pre = norm_ffn_pre[0][None]
    g_ffn_post = norm_ffn_post[0][None]
    w_in_b = w_in[0].astype(BF16)
    wao = w_attn_o[0].astype(BF16)
    wco = w_conv_o[0].astype(BF16)
    wmo = w_mix_out[0].astype(BF16)
    w_up = w_ffn_up[0].astype(BF16)
    w_down = w_ffn_down[0].astype(BF16)
    sinks = attn_sinks[0]
    cw_mix = conv_mix_w[0]
    cw_ffn = conv_ffn_w[0]

    tp = batch * seq
    xp = x_prompt.reshape(tp, D_MODEL)
    cos_p, sin_p = _rope_tables(jnp.arange(seq, dtype=F32))
    q, k, v, u, b, ga, gc = _inproj(xp, g_pre, cos_p, sin_p, w_in_b, tm=256)
    oa = _attn_prompt(sinks, q, k, v, seq, tq=256)
    hp = _mix(xp, oa, u, b, ga, gc, None, cw_mix, wao, wco, wmo, g_post, seq, tm=256)
    f, a_tail = _ffn_up(hp, g_ffn_pre, w_up, cw_ffn, None, seq, tm=1024)
    yp = _ffn_down(f, w_down, hp, g_ffn_post, tm=256)

    keep = min(WINDOW, seq)
    new_k_prompt = k.reshape(batch, seq, KV_W)[:, seq - keep:].reshape(1, batch, keep, N_KV_HEADS, HEAD_DIM)
    new_v_prompt = v.reshape(batch, seq, KV_W)[:, seq - keep:].reshape(1, batch, keep, N_KV_HEADS, HEAD_DIM)
    conv_mix_prompt = u.reshape(batch, seq, D_CONV)[:, seq - (CONV_W - 1):][None]
    a_tail = a_tail.reshape(batch, seq // 1024, HALO, D_FF)
    conv_ffn_prompt = a_tail[:, -1, HALO - (CONV_W - 1):][None]

    ts = dec_batch * dec_seq
    xs = x_sample.reshape(ts, D_MODEL)
    pos_s = PAST_LEN + jnp.arange(dec_seq, dtype=F32)
    cos_s, sin_s = _rope_tables(jnp.tile(pos_s, dec_batch))
    qs, ks, vs, us, bs, gas, gcs = _inproj(xs, g_pre, cos_s, sin_s, w_in_b, tm=ts)
    n_valid = WINDOW + dec_seq
    n_keys = 2 * WINDOW
    pad = jnp.zeros((dec_batch, n_keys - n_valid, KV_W), F32)
    k_all = jnp.concatenate([cache_k[0].reshape(dec_batch, WINDOW, KV_W),
                             ks.reshape(dec_batch, dec_seq, KV_W), pad], axis=1)
    v_all = jnp.concatenate([cache_v[0].reshape(dec_batch, WINDOW, KV_W),
                             vs.reshape(dec_batch, dec_seq, KV_W), pad], axis=1)
    oas = _attn_sample(sinks, qs, k_all.reshape(dec_batch * n_keys, KV_W),
                       v_all.reshape(dec_batch * n_keys, KV_W), n_valid, n_keys, dec_seq, seqs_per_step=4)
    hs = _mix(xs, oas, us, bs, gas, gcs, _row_history(state_conv_mix[0], dec_seq),
              cw_mix, wao, wco, wmo, g_post, dec_seq, tm=ts)
    fs, a_s = _ffn_up(hs, g_ffn_pre, w_up, cw_ffn, _row_history(state_conv_ffn[0], dec_seq), dec_seq, tm=ts)
    ys = _ffn_down(fs, w_down, hs, g_ffn_post, tm=ts)

    new_k_sample = ks.reshape(1, dec_batch, dec_seq, N_KV_HEADS, HEAD_DIM)
    new_v_sample = vs.reshape(1, dec_batch, dec_seq, N_KV_HEADS, HEAD_DIM)
    conv_mix_sample = us.reshape(dec_batch, dec_seq, D_CONV)[:, dec_seq - (CONV_W - 1):][None]
    conv_ffn_sample = a_s.reshape(dec_batch, dec_seq, D_FF)[:, dec_seq - (CONV_W - 1):][None]

    return (yp.reshape(batch, seq, D_MODEL), ys.reshape(dec_batch, dec_seq, D_MODEL),
            new_k_prompt, new_v_prompt, conv_mix_prompt, conv_ffn_prompt,
            new_k_sample, new_v_sample, conv_mix_sample, conv_ffn_sample)
```

```python
import functools

import jax
import jax.numpy as jnp
from jax import lax
from jax.experimental import pallas as pl
from jax.experimental.pallas import tpu as pltpu

F32 = jnp.float32
BF16 = jnp.bfloat16

D_MODEL = 2048
CHUNK = 64
N_HEADS = 16
N_KV_HEADS = 4
GROUP = N_HEADS // N_KV_HEADS
HEAD_DIM = 64
ROT_DIM = HEAD_DIM // 4
ROPE_THETA = 500000.0
WINDOW = 128
D_CONV = 1024
CONV_W = 3
D_FF = 5632
EPS = 1e-6
NEG = -1e30
PAST_LEN = 4096
Q_W = N_HEADS * HEAD_DIM
KV_W = N_KV_HEADS * HEAD_DIM
D_IN = Q_W + 2 * KV_W + 3 * D_CONV + 2 * D_MODEL

LANES = 128
SUBLANES = 8
VMEM_LIMIT_CAP = 60 * 1024 * 1024
VMEM_COMPILER_SLACK = 6 * 1024 * 1024

TN_IN = 512
TN_FF = 256
FF_ROW_BLK = 256
FF_PARTS = 2
FF_UP_TM = 1024
ATT_BLK = 2 * CHUNK
HALO = SUBLANES
ROW_BLK = 256

_T_Q = (0, 2)
_T_KV = (2, 3)
_T_XIN = (3, 5)
_T_B = (5, 7)
_T_C = (7, 9)
_T_GA = (9, 13)
_T_GC = (13, 17)


def _vmem_limit(nbytes):
    return int(min(nbytes + VMEM_COMPILER_SLACK, VMEM_LIMIT_CAP))


def _params(n_axes, nbytes):
    return pltpu.CompilerParams(
        dimension_semantics=("arbitrary",) * n_axes,
        vmem_limit_bytes=_vmem_limit(nbytes),
    )


def _rms_norm(x, g):
    return x * lax.rsqrt(jnp.mean(x * x, axis=-1, keepdims=True) + EPS) * g


def _rope128(x, cos, sin):
    dh = lax.broadcasted_iota(jnp.int32, x.shape, 1) & (HEAD_DIM - 1)
    half = ROT_DIM // 2
    partner = jnp.where(dh < half, pltpu.roll(x, LANES - half, axis=1), pltpu.roll(x, half, axis=1))
    return x * cos + partner * sin


def _shifted_rows(a, prev1_row0, prev2_row0, prev2_row1, period):
    assert period & (period - 1) == 0
    rows = lax.broadcasted_iota(jnp.int32, (a.shape[0], 1), 0) & (period - 1)
    p1 = jnp.where(rows == 0, prev1_row0, pltpu.roll(a, 1, axis=0))
    p2 = jnp.where(rows == 0, prev2_row0, jnp.where(rows == 1, prev2_row1, pltpu.roll(a, 2, axis=0)))
    return p1, p2


def _inproj_body(x_ref, g_ref, cos_ref, sin_ref, w_ref,
                 q_ref, k_ref, v_ref, u_ref, b_ref, ga_ref, gc_ref, xn_sc):
    xn_sc[...] = _rms_norm(x_ref[...], g_ref[...]).astype(BF16)
    cos = cos_ref[...]
    sin = sin_ref[...]

    def col(t):
        return jnp.dot(xn_sc[...], w_ref[:, t * TN_IN:(t + 1) * TN_IN], preferred_element_type=F32)

    def blocks(n):
        return [slice(blk * LANES, (blk + 1) * LANES) for blk in range(n)]

    for t in range(*_T_Q):
        acc = col(t)
        base = (t - _T_Q[0]) * TN_IN
        for sl in blocks(TN_IN // LANES):
            q_ref[:, base + sl.start:base + sl.stop] = (
                _rope128(acc[:, sl], cos, sin) * (HEAD_DIM ** -0.5)).astype(BF16)
    acc = col(_T_KV[0])
    for sl in blocks(KV_W // LANES):
        k_ref[:, sl] = _rope128(acc[:, sl], cos, sin)
    v_ref[...] = acc[:, KV_W:2 * KV_W]
    for c in range(_T_C[1] - _T_C[0]):
        cols = slice(c * TN_IN, (c + 1) * TN_IN)
        u_ref[:, cols] = col(_T_C[0] + c) * col(_T_XIN[0] + c)
        b_ref[:, cols] = col(_T_B[0] + c).astype(BF16)
    for c in range(_T_GA[1] - _T_GA[0]):
        cols = slice(c * TN_IN, (c + 1) * TN_IN)
        ga_ref[:, cols] = col(_T_GA[0] + c).astype(BF16)
        gc_ref[:, cols] = col(_T_GC[0] + c).astype(BF16)


def _inproj(x, g, cos, sin, w, tm):
    t = x.shape[0]
    n_pos = cos.shape[0] // tm
    row = lambda i: (i, 0)
    fixed = lambda i: (0, 0)
    pos = lambda i: (i % n_pos, 0)
    widths = (Q_W, KV_W, KV_W, D_CONV, D_CONV, D_MODEL, D_MODEL)
    dtypes = (BF16, F32, F32, F32, BF16, BF16, BF16)
    out_row_bytes = sum(wd * jnp.dtype(dt).itemsize for wd, dt in zip(widths, dtypes))
    nbytes = (D_MODEL * D_IN * 2 + 2 * tm * (D_MODEL * 4 + 2 * LANES * 4 + out_row_bytes)
              + tm * D_MODEL * 2 + 6 * tm * TN_IN * 4)
    return pl.pallas_call(
        _inproj_body,
        grid=(t // tm,),
        in_specs=[
            pl.BlockSpec((tm, D_MODEL), row),
            pl.BlockSpec((1, D_MODEL), fixed),
            pl.BlockSpec((tm, LANES), pos),
            pl.BlockSpec((tm, LANES), pos),
            pl.BlockSpec((D_MODEL, D_IN), fixed, pipeline_mode=pl.Buffered(1)),
        ],
        out_specs=tuple(pl.BlockSpec((tm, wd), row) for wd in widths),
        out_shape=tuple(jax.ShapeDtypeStruct((t, wd), dt) for wd, dt in zip(widths, dtypes)),
        scratch_shapes=[pltpu.VMEM((tm, D_MODEL), BF16)],
        compiler_params=_params(1, nbytes),
        name="inproj",
    )(x, g, cos, sin, w)


def _head_pair_operands(x256, h):
    blk = x256[:, (h // 2) * LANES:(h // 2 + 1) * LANES]
    lo_lanes = lax.broadcasted_iota(jnp.int32, blk.shape, 1) < HEAD_DIM
    if h % 2 == 0:
        lo = jnp.where(lo_lanes, blk, 0.0)
        hi = pltpu.roll(lo, HEAD_DIM, axis=1)
    else:
        hi = jnp.where(lo_lanes, 0.0, blk)
        lo = pltpu.roll(hi, HEAD_DIM, axis=1)
    return jnp.concatenate([lo, hi], axis=0).astype(BF16)


def _attend(q_ref, o_ref, sink_ref, blocks, s_sc, m_sc, p_sc, v_sc):
    n_keys = blocks[0][1].shape[0]
    n_rows = s_sc.shape[1]
    key_row = lax.broadcasted_iota(jnp.int32, (2 * n_keys, LANES), 0)
    key_lane = lax.broadcasted_iota(jnp.int32, (2 * n_keys, LANES), 1)
    ones_cat = jnp.where((key_row >= n_keys) == (key_lane >= HEAD_DIM), 1.0, 0.0).astype(BF16)
    lo_lanes = lax.broadcasted_iota(jnp.int32, (n_rows, LANES), 1) < HEAD_DIM
    halves = [slice(hh * n_keys, (hh + 1) * n_keys) for hh in range(2)]

    tiles = []
    n_v = 0
    for q_rows, k_all, v_all, valid in blocks:
        for h in range(N_KV_HEADS):
            kcat = _head_pair_operands(k_all, h)
            v_sc[n_v] = jnp.concatenate([_head_pair_operands(v_all, h), ones_cat], axis=1)
            for p in range(2 * h, 2 * h + 2):
                lanes = slice(p * LANES, (p + 1) * LANES)
                sc = lax.dot_general(q_ref[q_rows, lanes], kcat, (((1,), (1,)), ((), ())),
                                     preferred_element_type=F32)
                s_sc[len(tiles)] = jnp.where(valid, sc, NEG)
                tiles.append((q_rows, lanes, p, n_v))
            n_v += 1
    for t, (_, _, p, _) in enumerate(tiles):
        for hh in range(2):
            m = jnp.maximum(jnp.max(s_sc[t, :, halves[hh]], axis=-1, keepdims=True), sink_ref[2 * p + hh])
            m_sc[2 * t + hh] = jnp.broadcast_to(m, (n_rows, LANES))
    for t in range(len(tiles)):
        p_sc[t] = jnp.concatenate(
            [jnp.exp(s_sc[t, :, halves[hh]] - jnp.tile(m_sc[2 * t + hh], (1, n_keys // LANES))).astype(BF16)
             for hh in range(2)], axis=1)
    for t, (q_rows, lanes, p, vi) in enumerate(tiles):
        e = p_sc[t]
        num_den = jnp.dot(e, v_sc[vi], preferred_element_type=F32)
        num = num_den[:, :LANES]
        den = num_den[:, LANES:]
        sink_term = jnp.where(lo_lanes, jnp.exp(sink_ref[2 * p] - m_sc[2 * t]),
                              jnp.exp(sink_ref[2 * p + 1] - m_sc[2 * t + 1]))
        o_ref[q_rows, lanes] = (num / (den + sink_term)).astype(BF16)


def _attn_scratch(n_blocks, n_rows, n_keys):
    n_tiles = n_blocks * N_HEADS // 2
    return [
        pltpu.VMEM((n_tiles, n_rows, 2 * n_keys), F32),
        pltpu.VMEM((2 * n_tiles, n_rows, LANES), F32),
        pltpu.VMEM((n_tiles, n_rows, 2 * n_keys), BF16),
        pltpu.VMEM((n_blocks * N_KV_HEADS, 2 * n_keys, 2 * LANES), BF16),
    ]


def _attn_scratch_bytes(n_blocks, n_rows, n_keys):
    n_tiles = n_blocks * N_HEADS // 2
    return (n_tiles * n_rows * 2 * n_keys * 6 + 2 * n_tiles * n_rows * LANES * 4
            + n_blocks * N_KV_HEADS * 2 * n_keys * 2 * LANES * 2)


def _attn_prompt_body(sink_ref, q_ref, kc_ref, kp_ref, vc_ref, vp_ref, o_ref, *scratch, tiles_per_seq):
    first = (pl.program_id(0) % tiles_per_seq) == 0
    n_sub = q_ref.shape[0] // ATT_BLK
    n_keys = 2 * ATT_BLK
    shape = (ATT_BLK, 2 * n_keys)
    q_chunk = lax.broadcasted_iota(jnp.int32, shape, 0) // CHUNK
    k_chunk = (lax.broadcasted_iota(jnp.int32, shape, 1) & (n_keys - 1)) // CHUNK
    band = (k_chunk >= q_chunk) & (k_chunk <= q_chunk + WINDOW // CHUNK)
    first_valid = band & (k_chunk >= jnp.where(first, ATT_BLK // CHUNK, 0))
    blocks = []
    for s in range(n_sub):
        rows = slice(s * ATT_BLK, (s + 1) * ATT_BLK)
        if s == 0:
            k_prev, v_prev, valid = kp_ref[...], vp_ref[...], first_valid
        else:
            prev = slice((s - 1) * ATT_BLK, s * ATT_BLK)
            k_prev, v_prev, valid = kc_ref[prev, :], vc_ref[prev, :], band
        k_all = jnp.concatenate([k_prev, kc_ref[rows, :]], axis=0)
        v_all = jnp.concatenate([v_prev, vc_ref[rows, :]], axis=0)
        blocks.append((rows, k_all, v_all, valid))
    _attend(q_ref, o_ref, sink_ref, blocks, *scratch)


def _attn_prompt(sinks, q, k, v, seq, tq):
    t = q.shape[0]
    ratio = tq // ATT_BLK
    cur = lambda i: (i, 0)
    prev = lambda i: (jnp.maximum(i * ratio - 1, 0), 0)
    nbytes = (2 * (2 * tq * Q_W * 2 + 2 * (tq + ATT_BLK) * KV_W * 4)
              + _attn_scratch_bytes(ratio, ATT_BLK, 2 * ATT_BLK))
    return pl.pallas_call(
        functools.partial(_attn_prompt_body, tiles_per_seq=seq // tq),
        grid=(t // tq,),
        in_specs=[
            pl.BlockSpec(memory_space=pltpu.SMEM),
            pl.BlockSpec((tq, Q_W), cur),
            pl.BlockSpec((tq, KV_W), cur),
            pl.BlockSpec((ATT_BLK, KV_W), prev),
            pl.BlockSpec((tq, KV_W), cur),
            pl.BlockSpec((ATT_BLK, KV_W), prev),
        ],
        out_specs=pl.BlockSpec((tq, Q_W), cur),
        out_shape=jax.ShapeDtypeStruct((t, Q_W), BF16),
        scratch_shapes=_attn_scratch(ratio, ATT_BLK, 2 * ATT_BLK),
        compiler_params=_params(1, nbytes),
        name="attn_prompt",
    )(sinks, q, k, k, v, v)


def _attn_sample_body(sink_ref, q_ref, k_ref, v_ref, o_ref, *scratch, n_valid, n_keys, rows_per_seq):
    shape = (rows_per_seq, 2 * n_keys)
    valid = (lax.broadcasted_iota(jnp.int32, shape, 1) & (n_keys - 1)) < n_valid
    blocks = []
    for b in range(q_ref.shape[0] // rows_per_seq):
        keys = slice(b * n_keys, (b + 1) * n_keys)
        blocks.append((slice(b * rows_per_seq, (b + 1) * rows_per_seq), k_ref[keys, :], v_ref[keys, :], valid))
    _attend(q_ref, o_ref, sink_ref, blocks, *scratch)


def _attn_sample(sinks, q, k_all, v_all, n_valid, n_keys, rows_per_seq, seqs_per_step):
    t = q.shape[0]
    tq = rows_per_seq * seqs_per_step
    tk = n_keys * seqs_per_step
    nbytes = 2 * (2 * tq * Q_W * 2 + 2 * tk * KV_W * 4) + _attn_scratch_bytes(seqs_per_step, rows_per_seq, n_keys)
    return pl.pallas_call(
        functools.partial(_attn_sample_body, n_valid=n_valid, n_keys=n_keys, rows_per_seq=rows_per_seq),
        grid=(t // tq,),
        in_specs=[
            pl.BlockSpec(memory_space=pltpu.SMEM),
            pl.BlockSpec((tq, Q_W), lambda i: (i, 0)),
            pl.BlockSpec((tk, KV_W), lambda i: (i, 0)),
            pl.BlockSpec((tk, KV_W), lambda i: (i, 0)),
        ],
        out_specs=pl.BlockSpec((tq, Q_W), lambda i: (i, 0)),
        out_shape=jax.ShapeDtypeStruct((t, Q_W), BF16),
        scratch_shapes=_attn_scratch(seqs_per_step, rows_per_seq, n_keys),
        compiler_params=_params(1, nbytes),
        name="attn_sample",
    )(sinks, q, k_all, v_all)


def _mix_body(*refs, tiles_per_seq, hist_period):
    if hist_period:
        (x_ref, oa_ref, u_ref, b_ref, ga_ref, gc_ref, p1_ref, p2_ref,
         cw_ref, wao_ref, wco_ref, wmo_ref, g_ref, h_ref) = refs
    else:
        (x_ref, oa_ref, u_ref, b_ref, ga_ref, gc_ref,
         cw_ref, wao_ref, wco_ref, wmo_ref, g_ref, h_ref, carry_sc) = refs
    u = u_ref[...]
    tm = u.shape[0]
    if hist_period:
        p1, p2 = _shifted_rows(u, p1_ref[...], p2_ref[...], p2_ref[...], hist_period)
    else:
        @pl.when(pl.program_id(0) % tiles_per_seq == 0)
        def _():
            carry_sc[...] = jnp.zeros_like(carry_sc)

        last1 = carry_sc[HALO - 1:HALO, :]
        last2 = carry_sc[HALO - 2:HALO - 1, :]
        p1, p2 = _shifted_rows(u, last1, last2, last1, tm)
        carry_sc[...] = u[tm - HALO:tm, :]
    cy = cw_ref[0:1, :] * p2 + cw_ref[1:2, :] * p1 + cw_ref[2:3, :] * u
    y_conv = (b_ref[...].astype(F32) * cy).astype(BF16)
    attn_o = jnp.dot(oa_ref[...], wao_ref[...], preferred_element_type=F32)
    conv_o = jnp.dot(y_conv, wco_ref[...], preferred_element_type=F32)
    mixed = (jax.nn.sigmoid(ga_ref[...].astype(F32)) * attn_o
             + jax.nn.sigmoid(gc_ref[...].astype(F32)) * conv_o)
    mo = jnp.dot(mixed.astype(BF16), wmo_ref[...], preferred_element_type=F32)
    h_ref[...] = x_ref[...] + _rms_norm(mo, g_ref[...])


def _mix(x, oa, u, b, ga, gc, hist, cw, wao, wco, wmo, g, seq, tm):
    t = x.shape[0]
    row = lambda i: (i, 0)
    fixed = lambda i: (0, 0)
    resident = pl.Buffered(1)
    act_specs = [
        pl.BlockSpec((tm, D_MODEL), row),
        pl.BlockSpec((tm, Q_W), row),
        pl.BlockSpec((tm, D_CONV), row),
        pl.BlockSpec((tm, D_CONV), row),
        pl.BlockSpec((tm, D_MODEL), row),
        pl.BlockSpec((tm, D_MODEL), row),
    ]
    acts = [x, oa, u, b, ga, gc]
    scratch = []
    if hist is None:
        hist_period = 0
        scratch.append(pltpu.VMEM((HALO, D_CONV), F32))
    else:
        hist_period = seq
        act_specs += [pl.BlockSpec((tm, D_CONV), row)] * 2
        acts += list(hist)
    w_specs = [
        pl.BlockSpec((CONV_W, D_CONV), fixed),
        pl.BlockSpec((Q_W, D_MODEL), fixed, pipeline_mode=resident),
        pl.BlockSpec((D_CONV, D_MODEL), fixed, pipeline_mode=resident),
        pl.BlockSpec((D_MODEL, D_MODEL), fixed, pipeline_mode=resident),
        pl.BlockSpec((1, D_MODEL), fixed),
    ]
    row_bytes = D_MODEL * 4 * 2 + Q_W * 2 + D_CONV * (4 + 2) + D_MODEL * 2 * 2
    if hist is not None:
        row_bytes += 2 * D_CONV * 4
    nbytes = (2 * tm * row_bytes + (Q_W + D_CONV + D_MODEL) * D_MODEL * 2 + 6 * tm * D_MODEL * 4)
    return pl.pallas_call(
        functools.partial(_mix_body, tiles_per_seq=max(seq // tm, 1), hist_period=hist_period),
        grid=(t // tm,),
        in_specs=act_specs + w_specs,
        out_specs=pl.BlockSpec((tm, D_MODEL), row),
        out_shape=jax.ShapeDtypeStruct((t, D_MODEL), F32),
        scratch_shapes=scratch,
        compiler_params=_params(1, nbytes),
        name="mix",
    )(*acts, cw, wao, wco, wmo, g)


def _ffn_up_body(*refs, tiles_per_seq, hist_period):
    if hist_period:
        h_ref, g_ref, wa_ref, wg_ref, cw_ref, p1_ref, p2_ref, f_ref, a_ref, xn_sc, carry_sc = refs
    else:
        h_ref, g_ref, wa_ref, wg_ref, cw_ref, f_ref, tail_ref, xn_sc, carry_sc = refs
        @pl.when(pl.program_id(1) % tiles_per_seq == 0)
        def _():
            carry_sc[...] = jnp.zeros_like(carry_sc)

    tm = h_ref.shape[0]
    rb = min(tm, FF_ROW_BLK)
    n_cols = wa_ref.shape[1]
    for r in range(tm // rb):
        rows = slice(r * rb, (r + 1) * rb)
        xn_sc[rows, :] = _rms_norm(h_ref[rows, :], g_ref[...]).astype(BF16)
        for start in range(0, n_cols, TN_FF):
            cols = slice(start, min(start + TN_FF, n_cols))
            a = jnp.dot(xn_sc[rows, :], wa_ref[:, cols], preferred_element_type=F32)
            gate = jnp.dot(xn_sc[rows, :], wg_ref[:, cols], preferred_element_type=F32)
            if hist_period:
                p1, p2 = _shifted_rows(a, p1_ref[rows, cols], p2_ref[rows, cols], p2_ref[rows, cols], hist_period)
                a_ref[rows, cols] = a
            else:
                last1 = carry_sc[HALO - 1:HALO, cols]
                p1, p2 = _shifted_rows(a, last1, carry_sc[HALO - 2:HALO - 1, cols], last1, rb)
                carry_sc[:, cols] = a[rb - HALO:rb, :]
            ua = cw_ref[0:1, cols] * p2 + cw_ref[1:2, cols] * p1 + cw_ref[2:3, cols] * a
            f_ref[rows, cols] = (jax.nn.gelu(ua, approximate=True) * gate).astype(BF16)
    if not hist_period:
        tail_ref[...] = carry_sc[...]


def _ffn_up(h, g, w_up, cw, hist, seq, tm):
    t = h.shape[0]
    part = D_FF // FF_PARTS
    grid = (FF_PARTS, t // tm)
    resident = pl.Buffered(1)
    in_specs = [
        pl.BlockSpec((tm, D_MODEL), lambda c, i: (i, 0)),
        pl.BlockSpec((1, D_MODEL), lambda c, i: (0, 0)),
        pl.BlockSpec((D_MODEL, part), lambda c, i: (0, c), pipeline_mode=resident),
        pl.BlockSpec((D_MODEL, part), lambda c, i: (0, c + FF_PARTS), pipeline_mode=resident),
        pl.BlockSpec((CONV_W, part), lambda c, i: (0, c)),
    ]
    args = [h, g, w_up, w_up, cw]
    scratch = [pltpu.VMEM((tm, D_MODEL), BF16), pltpu.VMEM((HALO, part), F32)]
    f_spec = pl.BlockSpec((tm, part), lambda c, i: (i, c))
    f_shape = jax.ShapeDtypeStruct((t, D_FF), BF16)
    if hist is None:
        hist_period = 0
        out_specs = (f_spec, pl.BlockSpec((None, HALO, part), lambda c, i: (i, 0, c)))
        out_shape = (f_shape, jax.ShapeDtypeStruct((t // tm, HALO, D_FF), F32))
    else:
        hist_period = seq
        in_specs += [pl.BlockSpec((tm, part), lambda c, i: (i, c))] * 2
        args += list(hist)
        out_specs = (f_spec, pl.BlockSpec((tm, part), lambda c, i: (i, c)))
        out_shape = (f_shape, jax.ShapeDtypeStruct((t, D_FF), F32))
    nbytes = (2 * D_MODEL * part * 2 + 2 * tm * (D_MODEL * 4 + part * (2 + 4 * 3))
              + tm * D_MODEL * 2 + min(tm, FF_ROW_BLK) * (D_MODEL * 4 + 8 * TN_FF * 4))
    return pl.pallas_call(
        functools.partial(_ffn_up_body, tiles_per_seq=max(seq // tm, 1), hist_period=hist_period),
        grid=grid,
        in_specs=in_specs,
        out_specs=out_specs,
        out_shape=out_shape,
        scratch_shapes=scratch,
        compiler_params=_params(2, nbytes),
        name="ffn_up",
    )(*args)


def _ffn_down_body(f_ref, w_ref, h_ref, g_ref, o_ref):
    tm = f_ref.shape[0]
    rb = min(tm, ROW_BLK)
    for r in range(tm // rb):
        rows = slice(r * rb, (r + 1) * rb)
        acc = jnp.dot(f_ref[rows, :], w_ref[...], preferred_element_type=F32)
        o_ref[rows, :] = h_ref[rows, :] + _rms_norm(acc, g_ref[...])


def _ffn_down(f, w, h, g, tm):
    t = f.shape[0]
    nbytes = 2 * tm * (D_FF * 2 + D_MODEL * 4 * 2) + D_FF * D_MODEL * 2 + 2 * tm * D_MODEL * 4
    return pl.pallas_call(
        _ffn_down_body,
        grid=(t // tm,),
        in_specs=[
            pl.BlockSpec((tm, D_FF), lambda i: (i, 0)),
            pl.BlockSpec((D_FF, D_MODEL), lambda i: (0, 0), pipeline_mode=pl.Buffered(1)),
            pl.BlockSpec((tm, D_MODEL), lambda i: (i, 0)),
            pl.BlockSpec((1, D_MODEL), lambda i: (0, 0)),
        ],
        out_specs=pl.BlockSpec((tm, D_MODEL), lambda i: (i, 0)),
        out_shape=jax.ShapeDtypeStruct((t, D_MODEL), F32),
        compiler_params=_params(1, nbytes),
        name="ffn_down",
    )(f, w, h, g)


def _rope_tables(pos):
    half = ROT_DIM // 2
    inv = jnp.power(jnp.float32(ROPE_THETA), -jnp.arange(half, dtype=F32) * (2.0 / ROT_DIM))
    ang = pos.astype(F32)[:, None] * inv[None, :]
    cos = jnp.cos(ang)
    sin = jnp.sin(ang)
    rest = HEAD_DIM - ROT_DIM
    cos_h = jnp.concatenate([cos, cos, jnp.ones((pos.shape[0], rest), F32)], axis=1)
    sin_h = jnp.concatenate([-sin, sin, jnp.zeros((pos.shape[0], rest), F32)], axis=1)
    reps = LANES // HEAD_DIM
    return jnp.tile(cos_h, (1, reps)), jnp.tile(sin_h, (1, reps))


def _row_history(state, seq):
    b, _, c = state.shape
    p1 = jnp.concatenate([state[:, 1:2], jnp.zeros((b, seq - 1, c), F32)], axis=1)
    p2 = jnp.concatenate([state[:, 0:1], state[:, 1:2], jnp.zeros((b, seq - 2, c), F32)], axis=1)
    return p1.reshape(b * seq, c), p2.reshape(b * seq, c)


def kernel(x_prompt, x_sample, cache_k, cache_v, state_conv_mix, state_conv_ffn, norm_mix_pre, w_in,
           attn_sinks, conv_mix_w, w_attn_o, w_conv_o, w_mix_out, norm_mix_post, norm_ffn_pre,
           w_ffn_up, conv_ffn_w, w_ffn_down, norm_ffn_post):
    assert w_in.shape == (1, D_MODEL, D_IN), "one layer only"
    batch, seq, _ = x_prompt.shape
    dec_batch, dec_seq, _ = x_sample.shape
    assert seq % 1024 == 0 and (dec_batch * dec_seq) % 256 == 0 and dec_seq % 16 == 0
    assert cache_k.shape[2] == WINDOW

    g_pre = norm_mix_pre[0][None]
    g_post = norm_mix_post[0][None]
    g_ffn_pre = norm_ffn_pre[0][None]
    g_ffn_post = norm_ffn_post[0][None]
    w_in_b = w_in[0].astype(BF16)
    wao = w_attn_o[0].astype(BF16)
    wco = w_conv_o[0].astype(BF16)
    wmo = w_mix_out[0].astype(BF16)
    w_up = w_ffn_up[0].astype(BF16)
    w_down = w_ffn_down[0].astype(BF16)
    sinks = attn_sinks[0]
    cw_mix = conv_mix_w[0]
    cw_ffn = conv_ffn_w[0]

    tp = batch * seq
    xp = x_prompt.reshape(tp, D_MODEL)
    cos_p, sin_p = _rope_tables(jnp.arange(seq, dtype=F32))
    q, k, v, u, b, ga, gc = _inproj(xp, g_pre, cos_p, sin_p, w_in_b, tm=256)
    oa = _attn_prompt(sinks, q, k, v, seq, tq=256)
    hp = _mix(xp, oa, u, b, ga, gc, None, cw_mix, wao, wco, wmo, g_post, seq, tm=256)
    f, a_tail = _ffn_up(hp, g_ffn_pre, w_up, cw_ffn, None, seq, tm=FF_UP_TM)
    yp = _ffn_down(f, w_down, hp, g_ffn_post, tm=512)

    keep = min(WINDOW, seq)
    new_k_prompt = k.reshape(batch, seq, KV_W)[:, seq - keep:].reshape(1, batch, keep, N_KV_HEADS, HEAD_DIM)
    new_v_prompt = v.reshape(batch, seq, KV_W)[:, seq - keep:].reshape(1, batch, keep, N_KV_HEADS, HEAD_DIM)
    conv_mix_prompt = u.reshape(batch, seq, D_CONV)[:, seq - (CONV_W - 1):][None]
    a_tail = a_tail.reshape(batch, seq // FF_UP_TM, HALO, D_FF)
    conv_ffn_prompt = a_tail[:, -1, HALO - (CONV_W - 1):][None]

    ts = dec_batch * dec_seq
    xs = x_sample.reshape(ts, D_MODEL)
    pos_s = PAST_LEN + jnp.arange(dec_seq, dtype=F32)
    cos_s, sin_s = _rope_tables(jnp.tile(pos_s, dec_batch))
    qs, ks, vs, us, bs, gas, gcs = _inproj(xs, g_pre, cos_s, sin_s, w_in_b, tm=ts)
    n_valid = WINDOW + dec_seq
    n_keys = 2 * WINDOW
    pad = jnp.zeros((dec_batch, n_keys - n_valid, KV_W), F32)
    k_all = jnp.concatenate([cache_k[0].reshape(dec_batch, WINDOW, KV_W),
                             ks.reshape(dec_batch, dec_seq, KV_W), pad], axis=1)
    v_all = jnp.concatenate([cache_v[0].reshape(dec_batch, WINDOW, KV_W),
                             vs.reshape(dec_batch, dec_seq, KV_W), pad], axis=1)
    oas = _attn_sample(sinks, qs, k_all.reshape(dec_batch * n_keys, KV_W),
                       v_all.reshape(dec_batch * n_keys, KV_W), n_valid, n_keys, dec_seq, seqs_per_step=4)
    hs = _mix(xs, oas, us, bs, gas, gcs, _row_history(state_conv_mix[0], dec_seq),
              cw_mix, wao, wco, wmo, g_post, dec_seq, tm=ts)
    fs, a_s = _ffn_up(hs, g_ffn_pre, w_up, cw_ffn, _row_history(state_conv_ffn[0], dec_seq), dec_seq, tm=ts)
    ys = _ffn_down(fs, w_down, hs, g_ffn_post, tm=ts)

    new_k_sample = ks.reshape(1, dec_batch, dec_seq, N_KV_HEADS, HEAD_DIM)
    new_v_sample = vs.reshape(1, dec_batch, dec_seq, N_KV_HEADS, HEAD_DIM)
    conv_mix_sample = us.reshape(dec_batch, dec_seq, D_CONV)[:, dec_seq - (CONV_W - 1):][None]
    conv_ffn_sample = a_s.reshape(dec_batch, dec_seq, D_FF)[:, dec_seq - (CONV_W - 1):][None]

    return (yp.reshape(batch, seq, D_MODEL), ys.reshape(dec_batch, dec_seq, D_MODEL),
            new_k_prompt, new_v_prompt, conv_mix_prompt, conv_ffn_prompt,
            new_k_sample, new_v_sample, conv_mix_sample, conv_ffn_sample)
```

```python
import functools

import jax
import jax.numpy as jnp
from jax import lax
from jax.experimental import pallas as pl
from jax.experimental.pallas import tpu as pltpu

F32 = jnp.float32
BF16 = jnp.bfloat16

D_MODEL = 2048
CHUNK = 64
N_HEADS = 16
N_KV_HEADS = 4
GROUP = N_HEADS // N_KV_HEADS
HEAD_DIM = 64
ROT_DIM = HEAD_DIM // 4
ROPE_THETA = 500000.0
WINDOW = 128
D_CONV = 1024
CONV_W = 3
D_FF = 5632
EPS = 1e-6
NEG = -1e30
PAST_LEN = 4096
Q_W = N_HEADS * HEAD_DIM
KV_W = N_KV_HEADS * HEAD_DIM
D_IN = Q_W + 2 * KV_W + 3 * D_CONV + 2 * D_MODEL

LANES = 128
SUBLANES = 8
VMEM_LIMIT_CAP = 60 * 1024 * 1024
VMEM_COMPILER_SLACK = 6 * 1024 * 1024

TN_IN = 512
TN_FF = 512
FF_UP_TM = 1024
ATT_BLK = 2 * CHUNK
HALO = SUBLANES
ROW_BLK = 256

_T_Q = (0, 2)
_T_KV = (2, 3)
_T_XIN = (3, 5)
_T_B = (5, 7)
_T_C = (7, 9)
_T_GA = (9, 13)
_T_GC = (13, 17)


def _vmem_limit(nbytes):
    return int(min(nbytes + VMEM_COMPILER_SLACK, VMEM_LIMIT_CAP))


def _params(n_axes, nbytes):
    return pltpu.CompilerParams(
        dimension_semantics=("arbitrary",) * n_axes,
        vmem_limit_bytes=_vmem_limit(nbytes),
    )


def _rms_norm(x, g):
    return x * lax.rsqrt(jnp.mean(x * x, axis=-1, keepdims=True) + EPS) * g


def _rope128(x, cos, sin):
    dh = lax.broadcasted_iota(jnp.int32, x.shape, 1) & (HEAD_DIM - 1)
    half = ROT_DIM // 2
    partner = jnp.where(dh < half, pltpu.roll(x, LANES - half, axis=1), pltpu.roll(x, half, axis=1))
    return x * cos + partner * sin


def _shifted_rows(a, prev1_row0, prev2_row0, prev2_row1, period):
    assert period & (period - 1) == 0
    rows = lax.broadcasted_iota(jnp.int32, (a.shape[0], 1), 0) & (period - 1)
    p1 = jnp.where(rows == 0, prev1_row0, pltpu.roll(a, 1, axis=0))
    p2 = jnp.where(rows == 0, prev2_row0, jnp.where(rows == 1, prev2_row1, pltpu.roll(a, 2, axis=0)))
    return p1, p2


def _inproj_body(x_ref, g_ref, cos_ref, sin_ref, w_ref,
                 q_ref, k_ref, v_ref, u_ref, b_ref, ga_ref, gc_ref, xn_sc):
    xn_sc[...] = _rms_norm(x_ref[...], g_ref[...]).astype(BF16)
    cos = cos_ref[...]
    sin = sin_ref[...]

    def col(t):
        return jnp.dot(xn_sc[...], w_ref[:, t * TN_IN:(t + 1) * TN_IN], preferred_element_type=F32)

    def blocks(n):
        return [slice(blk * LANES, (blk + 1) * LANES) for blk in range(n)]

    for t in range(*_T_Q):
        acc = col(t)
        base = (t - _T_Q[0]) * TN_IN
        for sl in blocks(TN_IN // LANES):
            q_ref[:, base + sl.start:base + sl.stop] = (
                _rope128(acc[:, sl], cos, sin) * (HEAD_DIM ** -0.5)).astype(BF16)
    acc = col(_T_KV[0])
    for sl in blocks(KV_W // LANES):
        k_ref[:, sl] = _rope128(acc[:, sl], cos, sin)
    v_ref[...] = acc[:, KV_W:2 * KV_W]
    for c in range(_T_C[1] - _T_C[0]):
        cols = slice(c * TN_IN, (c + 1) * TN_IN)
        u_ref[:, cols] = col(_T_C[0] + c) * col(_T_XIN[0] + c)
        b_ref[:, cols] = col(_T_B[0] + c).astype(BF16)
    for c in range(_T_GA[1] - _T_GA[0]):
        cols = slice(c * TN_IN, (c + 1) * TN_IN)
        ga_ref[:, cols] = col(_T_GA[0] + c).astype(BF16)
        gc_ref[:, cols] = col(_T_GC[0] + c).astype(BF16)


def _inproj(x, g, cos, sin, w, tm):
    t = x.shape[0]
    n_pos = cos.shape[0] // tm
    row = lambda i: (i, 0)
    fixed = lambda i: (0, 0)
    pos = lambda i: (i % n_pos, 0)
    widths = (Q_W, KV_W, KV_W, D_CONV, D_CONV, D_MODEL, D_MODEL)
    dtypes = (BF16, F32, F32, F32, BF16, BF16, BF16)
    out_row_bytes = sum(wd * jnp.dtype(dt).itemsize for wd, dt in zip(widths, dtypes))
    nbytes = (D_MODEL * D_IN * 2 + 2 * tm * (D_MODEL * 4 + 2 * LANES * 4 + out_row_bytes)
              + tm * D_MODEL * 2 + 6 * tm * TN_IN * 4)
    return pl.pallas_call(
        _inproj_body,
        grid=(t // tm,),
        in_specs=[
            pl.BlockSpec((tm, D_MODEL), row),
            pl.BlockSpec((1, D_MODEL), fixed),
            pl.BlockSpec((tm, LANES), pos),
            pl.BlockSpec((tm, LANES), pos),
            pl.BlockSpec((D_MODEL, D_IN), fixed, pipeline_mode=pl.Buffered(1)),
        ],
        out_specs=tuple(pl.BlockSpec((tm, wd), row) for wd in widths),
        out_shape=tuple(jax.ShapeDtypeStruct((t, wd), dt) for wd, dt in zip(widths, dtypes)),
        scratch_shapes=[pltpu.VMEM((tm, D_MODEL), BF16)],
        compiler_params=_params(1, nbytes),
        name="inproj",
    )(x, g, cos, sin, w)


def _head_pair_operands(x256, h):
    blk = x256[:, (h // 2) * LANES:(h // 2 + 1) * LANES]
    lo_lanes = lax.broadcasted_iota(jnp.int32, blk.shape, 1) < HEAD_DIM
    if h % 2 == 0:
        lo = jnp.where(lo_lanes, blk, 0.0)
        hi = pltpu.roll(lo, HEAD_DIM, axis=1)
    else:
        hi = jnp.where(lo_lanes, 0.0, blk)
        lo = pltpu.roll(hi, HEAD_DIM, axis=1)
    return jnp.concatenate([lo, hi], axis=0).astype(BF16)


def _attend(q_ref, o_ref, sink_ref, blocks, s_sc, m_sc, p_sc, v_sc):
    n_keys = blocks[0][1].shape[0]
    n_rows = s_sc.shape[1]
    key_row = lax.broadcasted_iota(jnp.int32, (2 * n_keys, LANES), 0)
    key_lane = lax.broadcasted_iota(jnp.int32, (2 * n_keys, LANES), 1)
    ones_cat = jnp.where((key_row >= n_keys) == (key_lane >= HEAD_DIM), 1.0, 0.0).astype(BF16)
    lo_lanes = lax.broadcasted_iota(jnp.int32, (n_rows, LANES), 1) < HEAD_DIM
    halves = [slice(hh * n_keys, (hh + 1) * n_keys) for hh in range(2)]

    tiles = []
    n_v = 0
    for q_rows, k_all, v_all, valid in blocks:
        for h in range(N_KV_HEADS):
            kcat = _head_pair_operands(k_all, h)
            v_sc[n_v] = jnp.concatenate([_head_pair_operands(v_all, h), ones_cat], axis=1)
            for p in range(2 * h, 2 * h + 2):
                lanes = slice(p * LANES, (p + 1) * LANES)
                sc = lax.dot_general(q_ref[q_rows, lanes], kcat, (((1,), (1,)), ((), ())),
                                     preferred_element_type=F32)
                s_sc[len(tiles)] = jnp.where(valid, sc, NEG)
                tiles.append((q_rows, lanes, p, n_v))
            n_v += 1
    for t, (_, _, p, _) in enumerate(tiles):
        for hh in range(2):
            m = jnp.maximum(jnp.max(s_sc[t, :, halves[hh]], axis=-1, keepdims=True), sink_ref[2 * p + hh])
            m_sc[2 * t + hh] = jnp.broadcast_to(m, (n_rows, LANES))
    for t in range(len(tiles)):
        p_sc[t] = jnp.concatenate(
            [jnp.exp(s_sc[t, :, halves[hh]] - jnp.tile(m_sc[2 * t + hh], (1, n_keys // LANES))).astype(BF16)
             for hh in range(2)], axis=1)
    for t, (q_rows, lanes, p, vi) in enumerate(tiles):
        e = p_sc[t]
        num_den = jnp.dot(e, v_sc[vi], preferred_element_type=F32)
        num = num_den[:, :LANES]
        den = num_den[:, LANES:]
        sink_term = jnp.where(lo_lanes, jnp.exp(sink_ref[2 * p] - m_sc[2 * t]),
                              jnp.exp(sink_ref[2 * p + 1] - m_sc[2 * t + 1]))
        o_ref[q_rows, lanes] = (num / (den + sink_term)).astype(BF16)


def _attn_scratch(n_blocks, n_rows, n_keys):
    n_tiles = n_blocks * N_HEADS // 2
    return [
        pltpu.VMEM((n_tiles, n_rows, 2 * n_keys), F32),
        pltpu.VMEM((2 * n_tiles, n_rows, LANES), F32),
        pltpu.VMEM((n_tiles, n_rows, 2 * n_keys), BF16),
        pltpu.VMEM((n_blocks * N_KV_HEADS, 2 * n_keys, 2 * LANES), BF16),
    ]


def _attn_scratch_bytes(n_blocks, n_rows, n_keys):
    n_tiles = n_blocks * N_HEADS // 2
    return (n_tiles * n_rows * 2 * n_keys * 6 + 2 * n_tiles * n_rows * LANES * 4
            + n_blocks * N_KV_HEADS * 2 * n_keys * 2 * LANES * 2)


def _attn_prompt_body(sink_ref, q_ref, kc_ref, kp_ref, vc_ref, vp_ref, o_ref, *scratch, tiles_per_seq):
    first = (pl.program_id(0) % tiles_per_seq) == 0
    n_sub = q_ref.shape[0] // ATT_BLK
    n_keys = 2 * ATT_BLK
    shape = (ATT_BLK, 2 * n_keys)
    q_chunk = lax.broadcasted_iota(jnp.int32, shape, 0) // CHUNK
    k_chunk = (lax.broadcasted_iota(jnp.int32, shape, 1) & (n_keys - 1)) // CHUNK
    band = (k_chunk >= q_chunk) & (k_chunk <= q_chunk + WINDOW // CHUNK)
    first_valid = band & (k_chunk >= jnp.where(first, ATT_BLK // CHUNK, 0))
    blocks = []
    for s in range(n_sub):
        rows = slice(s * ATT_BLK, (s + 1) * ATT_BLK)
        if s == 0:
            k_prev, v_prev, valid = kp_ref[...], vp_ref[...], first_valid
        else:
            prev = slice((s - 1) * ATT_BLK, s * ATT_BLK)
            k_prev, v_prev, valid = kc_ref[prev, :], vc_ref[prev, :], band
        k_all = jnp.concatenate([k_prev, kc_ref[rows, :]], axis=0)
        v_all = jnp.concatenate([v_prev, vc_ref[rows, :]], axis=0)
        blocks.append((rows, k_all, v_all, valid))
    _attend(q_ref, o_ref, sink_ref, blocks, *scratch)


def _attn_prompt(sinks, q, k, v, seq, tq):
    t = q.shape[0]
    ratio = tq // ATT_BLK
    cur = lambda i: (i, 0)
    prev = lambda i: (jnp.maximum(i * ratio - 1, 0), 0)
    nbytes = (2 * (2 * tq * Q_W * 2 + 2 * (tq + ATT_BLK) * KV_W * 4)
              + _attn_scratch_bytes(ratio, ATT_BLK, 2 * ATT_BLK))
    return pl.pallas_call(
        functools.partial(_attn_prompt_body, tiles_per_seq=seq // tq),
        grid=(t // tq,),
        in_specs=[
            pl.BlockSpec(memory_space=pltpu.SMEM),
            pl.BlockSpec((tq, Q_W), cur),
            pl.BlockSpec((tq, KV_W), cur),
            pl.BlockSpec((ATT_BLK, KV_W), prev),
            pl.BlockSpec((tq, KV_W), cur),
            pl.BlockSpec((ATT_BLK, KV_W), prev),
        ],
        out_specs=pl.BlockSpec((tq, Q_W), cur),
        out_shape=jax.ShapeDtypeStruct((t, Q_W), BF16),
        scratch_shapes=_attn_scratch(ratio, ATT_BLK, 2 * ATT_BLK),
        compiler_params=_params(1, nbytes),
        name="attn_prompt",
    )(sinks, q, k, k, v, v)


def _attn_sample_body(sink_ref, q_ref, k_ref, v_ref, o_ref, *scratch, n_valid, n_keys, rows_per_seq):
    shape = (rows_per_seq, 2 * n_keys)
    valid = (lax.broadcasted_iota(jnp.int32, shape, 1) & (n_keys - 1)) < n_valid
    blocks = []
    for b in range(q_ref.shape[0] // rows_per_seq):
        keys = slice(b * n_keys, (b + 1) * n_keys)
        blocks.append((slice(b * rows_per_seq, (b + 1) * rows_per_seq), k_ref[keys, :], v_ref[keys, :], valid))
    _attend(q_ref, o_ref, sink_ref, blocks, *scratch)


def _attn_sample(sinks, q, k_all, v_all, n_valid, n_keys, rows_per_seq, seqs_per_step):
    t = q.shape[0]
    tq = rows_per_seq * seqs_per_step
    tk = n_keys * seqs_per_step
    nbytes = 2 * (2 * tq * Q_W * 2 + 2 * tk * KV_W * 4) + _attn_scratch_bytes(seqs_per_step, rows_per_seq, n_keys)
    return pl.pallas_call(
        functools.partial(_attn_sample_body, n_valid=n_valid, n_keys=n_keys, rows_per_seq=rows_per_seq),
        grid=(t // tq,),
        in_specs=[
            pl.BlockSpec(memory_space=pltpu.SMEM),
            pl.BlockSpec((tq, Q_W), lambda i: (i, 0)),
            pl.BlockSpec((tk, KV_W), lambda i: (i, 0)),
            pl.BlockSpec((tk, KV_W), lambda i: (i, 0)),
        ],
        out_specs=pl.BlockSpec((tq, Q_W), lambda i: (i, 0)),
        out_shape=jax.ShapeDtypeStruct((t, Q_W), BF16),
        scratch_shapes=_attn_scratch(seqs_per_step, rows_per_seq, n_keys),
        compiler_params=_params(1, nbytes),
        name="attn_sample",
    )(sinks, q, k_all, v_all)


def _mix_body(*refs, tiles_per_seq, hist_period):
    if hist_period:
        (x_ref, oa_ref, u_ref, b_ref, ga_ref, gc_ref, p1_ref, p2_ref,
         cw_ref, wao_ref, wco_ref, wmo_ref, g_ref, h_ref) = refs
    else:
        (x_ref, oa_ref, u_ref, b_ref, ga_ref, gc_ref,
         cw_ref, wao_ref, wco_ref, wmo_ref, g_ref, h_ref, carry_sc) = refs
    u = u_ref[...]
    tm = u.shape[0]
    if hist_period:
        p1, p2 = _shifted_rows(u, p1_ref[...], p2_ref[...], p2_ref[...], hist_period)
    else:
        @pl.when(pl.program_id(0) % tiles_per_seq == 0)
        def _():
            carry_sc[...] = jnp.zeros_like(carry_sc)

        last1 = carry_sc[HALO - 1:HALO, :]
        last2 = carry_sc[HALO - 2:HALO - 1, :]
        p1, p2 = _shifted_rows(u, last1, last2, last1, tm)
        carry_sc[...] = u[tm - HALO:tm, :]
    cy = cw_ref[0:1, :] * p2 + cw_ref[1:2, :] * p1 + cw_ref[2:3, :] * u
    y_conv = (b_ref[...].astype(F32) * cy).astype(BF16)
    attn_o = jnp.dot(oa_ref[...], wao_ref[...], preferred_element_type=F32)
    conv_o = jnp.dot(y_conv, wco_ref[...], preferred_element_type=F32)
    mixed = (jax.nn.sigmoid(ga_ref[...].astype(F32)) * attn_o
             + jax.nn.sigmoid(gc_ref[...].astype(F32)) * conv_o)
    mo = jnp.dot(mixed.astype(BF16), wmo_ref[...], preferred_element_type=F32)
    h_ref[...] = x_ref[...] + _rms_norm(mo, g_ref[...])


def _mix(x, oa, u, b, ga, gc, hist, cw, wao, wco, wmo, g, seq, tm):
    t = x.shape[0]
    row = lambda i: (i, 0)
    fixed = lambda i: (0, 0)
    resident = pl.Buffered(1)
    act_specs = [
        pl.BlockSpec((tm, D_MODEL), row),
        pl.BlockSpec((tm, Q_W), row),
        pl.BlockSpec((tm, D_CONV), row),
        pl.BlockSpec((tm, D_CONV), row),
        pl.BlockSpec((tm, D_MODEL), row),
        pl.BlockSpec((tm, D_MODEL), row),
    ]
    acts = [x, oa, u, b, ga, gc]
    scratch = []
    if hist is None:
        hist_period = 0
        scratch.append(pltpu.VMEM((HALO, D_CONV), F32))
    else:
        hist_period = seq
        act_specs += [pl.BlockSpec((tm, D_CONV), row)] * 2
        acts += list(hist)
    w_specs = [
        pl.BlockSpec((CONV_W, D_CONV), fixed),
        pl.BlockSpec((Q_W, D_MODEL), fixed, pipeline_mode=resident),
        pl.BlockSpec((D_CONV, D_MODEL), fixed, pipeline_mode=resident),
        pl.BlockSpec((D_MODEL, D_MODEL), fixed, pipeline_mode=resident),
        pl.BlockSpec((1, D_MODEL), fixed),
    ]
    row_bytes = D_MODEL * 4 * 2 + Q_W * 2 + D_CONV * (4 + 2) + D_MODEL * 2 * 2
    if hist is not None:
        row_bytes += 2 * D_CONV * 4
    nbytes = (2 * tm * row_bytes + (Q_W + D_CONV + D_MODEL) * D_MODEL * 2 + 6 * tm * D_MODEL * 4)
    return pl.pallas_call(
        functools.partial(_mix_body, tiles_per_seq=max(seq // tm, 1), hist_period=hist_period),
        grid=(t // tm,),
        in_specs=act_specs + w_specs,
        out_specs=pl.BlockSpec((tm, D_MODEL), row),
        out_shape=jax.ShapeDtypeStruct((t, D_MODEL), F32),
        scratch_shapes=scratch,
        compiler_params=_params(1, nbytes),
        name="mix",
    )(*acts, cw, wao, wco, wmo, g)


def _ffn_up_body(*refs, tiles_per_seq, hist_period):
    if hist_period:
        h_ref, g_ref, wa_ref, wg_ref, cw_ref, p1_ref, p2_ref, f_ref, a_ref, xn_sc = refs
    else:
        h_ref, g_ref, wa_ref, wg_ref, cw_ref, f_ref, tail_ref, xn_sc, carry_sc = refs
    i = pl.program_id(0)
    j = pl.program_id(1)

    @pl.when(j == 0)
    def _():
        xn_sc[...] = _rms_norm(h_ref[...], g_ref[...]).astype(BF16)

    if not hist_period:
        @pl.when(i % tiles_per_seq == 0)
        def _():
            carry_sc[j] = jnp.zeros(carry_sc.shape[1:], F32)

        tail = carry_sc[j]
    tm = xn_sc.shape[0]
    rb = min(tm, ROW_BLK)
    for r in range(tm // rb):
        rows = slice(r * rb, (r + 1) * rb)
        xn = xn_sc[rows, :]
        a = jnp.dot(xn, wa_ref[...], preferred_element_type=F32)
        gate = jnp.dot(xn, wg_ref[...], preferred_element_type=F32)
        if hist_period:
            p1, p2 = _shifted_rows(a, p1_ref[rows, :], p2_ref[rows, :], p2_ref[rows, :], hist_period)
            a_ref[rows, :] = a
        else:
            last1 = tail[HALO - 1:HALO, :]
            p1, p2 = _shifted_rows(a, last1, tail[HALO - 2:HALO - 1, :], last1, rb)
            tail = a[rb - HALO:rb, :]
        ua = cw_ref[0:1, :] * p2 + cw_ref[1:2, :] * p1 + cw_ref[2:3, :] * a
        f_ref[rows, :] = (jax.nn.gelu(ua, approximate=True) * gate).astype(BF16)
    if not hist_period:
        carry_sc[j] = tail
        tail_ref[...] = tail


def _ffn_up(h, g, w_up, cw, hist, seq, tm):
    t = h.shape[0]
    n_j = D_FF // TN_FF
    grid = (t // tm, n_j)
    in_specs = [
        pl.BlockSpec((tm, D_MODEL), lambda i, j: (i, 0)),
        pl.BlockSpec((1, D_MODEL), lambda i, j: (0, 0)),
        pl.BlockSpec((D_MODEL, TN_FF), lambda i, j: (0, j)),
        pl.BlockSpec((D_MODEL, TN_FF), lambda i, j: (0, j + n_j)),
        pl.BlockSpec((CONV_W, TN_FF), lambda i, j: (0, j)),
    ]
    args = [h, g, w_up, w_up, cw]
    scratch = [pltpu.VMEM((tm, D_MODEL), BF16)]
    f_spec = pl.BlockSpec((tm, TN_FF), lambda i, j: (i, j))
    f_shape = jax.ShapeDtypeStruct((t, D_FF), BF16)
    if hist is None:
        hist_period = 0
        scratch.append(pltpu.VMEM((n_j, HALO, TN_FF), F32))
        out_specs = (f_spec, pl.BlockSpec((None, HALO, TN_FF), lambda i, j: (i, 0, j)))
        out_shape = (f_shape, jax.ShapeDtypeStruct((t // tm, HALO, D_FF), F32))
    else:
        hist_period = seq
        in_specs += [pl.BlockSpec((tm, TN_FF), lambda i, j: (i, j))] * 2
        args += list(hist)
        out_specs = (f_spec, pl.BlockSpec((tm, TN_FF), lambda i, j: (i, j)))
        out_shape = (f_shape, jax.ShapeDtypeStruct((t, D_FF), F32))
    nbytes = (2 * tm * D_MODEL * 4 + tm * D_MODEL * 2 + 4 * D_MODEL * TN_FF * 2
              + 2 * tm * TN_FF * (2 + 4 * 3) + 8 * tm * TN_FF * 4)
    return pl.pallas_call(
        functools.partial(_ffn_up_body, tiles_per_seq=max(seq // tm, 1), hist_period=hist_period),
        grid=grid,
        in_specs=in_specs,
        out_specs=out_specs,
        out_shape=out_shape,
        scratch_shapes=scratch,
        compiler_params=_params(2, nbytes),
        name="ffn_up",
    )(*args)


def _ffn_down_body(f_ref, w_ref, h_ref, g_ref, o_ref):
    tm = f_ref.shape[0]
    rb = min(tm, ROW_BLK)
    for r in range(tm // rb):
        rows = slice(r * rb, (r + 1) * rb)
        acc = jnp.dot(f_ref[rows, :], w_ref[...], preferred_element_type=F32)
        o_ref[rows, :] = h_ref[rows, :] + _rms_norm(acc, g_ref[...])


def _ffn_down(f, w, h, g, tm):
    t = f.shape[0]
    nbytes = 2 * tm * (D_FF * 2 + D_MODEL * 4 * 2) + D_FF * D_MODEL * 2 + 2 * tm * D_MODEL * 4
    return pl.pallas_call(
        _ffn_down_body,
        grid=(t // tm,),
        in_specs=[
            pl.BlockSpec((tm, D_FF), lambda i: (i, 0)),
            pl.BlockSpec((D_FF, D_MODEL), lambda i: (0, 0), pipeline_mode=pl.Buffered(1)),
            pl.BlockSpec((tm, D_MODEL), lambda i: (i, 0)),
            pl.BlockSpec((1, D_MODEL), lambda i: (0, 0)),
        ],
        out_specs=pl.BlockSpec((tm, D_MODEL), lambda i: (i, 0)),
        out_shape=jax.ShapeDtypeStruct((t, D_MODEL), F32),
        compiler_params=_params(1, nbytes),
        name="ffn_down",
    )(f, w, h, g)


def _rope_tables(pos):
    half = ROT_DIM // 2
    inv = jnp.power(jnp.float32(ROPE_THETA), -jnp.arange(half, dtype=F32) * (2.0 / ROT_DIM))
    ang = pos.astype(F32)[:, None] * inv[None, :]
    cos = jnp.cos(ang)
    sin = jnp.sin(ang)
    rest = HEAD_DIM - ROT_DIM
    cos_h = jnp.concatenate([cos, cos, jnp.ones((pos.shape[0], rest), F32)], axis=1)
    sin_h = jnp.concatenate([-sin, sin, jnp.zeros((pos.shape[0], rest), F32)], axis=1)
    reps = LANES // HEAD_DIM
    return jnp.tile(cos_h, (1, reps)), jnp.tile(sin_h, (1, reps))


def _row_history(state, seq):
    b, _, c = state.shape
    p1 = jnp.concatenate([state[:, 1:2], jnp.zeros((b, seq - 1, c), F32)], axis=1)
    p2 = jnp.concatenate([state[:, 0:1], state[:, 1:2], jnp.zeros((b, seq - 2, c), F32)], axis=1)
    return p1.reshape(b * seq, c), p2.reshape(b * seq, c)


def kernel(x_prompt, x_sample, cache_k, cache_v, state_conv_mix, state_conv_ffn, norm_mix_pre, w_in,
           attn_sinks, conv_mix_w, w_attn_o, w_conv_o, w_mix_out, norm_mix_post, norm_ffn_pre,
           w_ffn_up, conv_ffn_w, w_ffn_down, norm_ffn_post):
    assert w_in.shape == (1, D_MODEL, D_IN), "one layer only"
    batch, seq, _ = x_prompt.shape
    dec_batch, dec_seq, _ = x_sample.shape
    assert seq % 1024 == 0 and (dec_batch * dec_seq) % 256 == 0 and dec_seq % 16 == 0
    assert cache_k.shape[2] == WINDOW

    g_pre = norm_mix_pre[0][None]
    g_post = norm_mix_post[0][None]
    g_ffn_pre = norm_ffn_pre[0][None]
    g_ffn_post = norm_ffn_post[0][None]
    w_in_b = w_in[0].astype(BF16)
    wao = w_attn_o[0].astype(BF16)
    wco = w_conv_o[0].astype(BF16)
    wmo = w_mix_out[0].astype(BF16)
    w_up = w_ffn_up[0].astype(BF16)
    w_down = w_ffn_down[0].astype(BF16)
    sinks = attn_sinks[0]
    cw_mix = conv_mix_w[0]
    cw_ffn = conv_ffn_w[0]

    tp = batch * seq
    xp = x_prompt.reshape(tp, D_MODEL)
    cos_p, sin_p = _rope_tables(jnp.arange(seq, dtype=F32))
    q, k, v, u, b, ga, gc = _inproj(xp, g_pre, cos_p, sin_p, w_in_b, tm=256)
    oa = _attn_prompt(sinks, q, k, v, seq, tq=256)
    hp = _mix(xp, oa, u, b, ga, gc, None, cw_mix, wao, wco, wmo, g_post, seq, tm=256)
    f, a_tail = _ffn_up(hp, g_ffn_pre, w_up, cw_ffn, None, seq, tm=FF_UP_TM)
    yp = _ffn_down(f, w_down, hp, g_ffn_post, tm=512)

    keep = min(WINDOW, seq)
    new_k_prompt = k.reshape(batch, seq, KV_W)[:, seq - keep:].reshape(1, batch, keep, N_KV_HEADS, HEAD_DIM)
    new_v_prompt = v.reshape(batch, seq, KV_W)[:, seq - keep:].reshape(1, batch, keep, N_KV_HEADS, HEAD_DIM)
    conv_mix_prompt = u.reshape(batch, seq, D_CONV)[:, seq - (CONV_W - 1):][None]
    a_tail = a_tail.reshape(batch, seq // FF_UP_TM, HALO, D_FF)
    conv_ffn_prompt = a_tail[:, -1, HALO - (CONV_W - 1):][None]

    ts = dec_batch * dec_seq
    xs = x_sample.reshape(ts, D_MODEL)
    pos_s = PAST_LEN + jnp.arange(dec_seq, dtype=F32)
    cos_s, sin_s = _rope_tables(jnp.tile(pos_s, dec_batch))
    qs, ks, vs, us, bs, gas, gcs = _inproj(xs, g_pre, cos_s, sin_s, w_in_b, tm=ts)
    n_valid = WINDOW + dec_seq
    n_keys = 2 * WINDOW
    pad = jnp.zeros((dec_batch, n_keys - n_valid, KV_W), F32)
    k_all = jnp.concatenate([cache_k[0].reshape(dec_batch, WINDOW, KV_W),
                             ks.reshape(dec_batch, dec_seq, KV_W), pad], axis=1)
    v_all = jnp.concatenate([cache_v[0].reshape(dec_batch, WINDOW, KV_W),
                             vs.reshape(dec_batch, dec_seq, KV_W), pad], axis=1)
    oas = _attn_sample(sinks, qs, k_all.reshape(dec_batch * n_keys, KV_W),
                       v_all.reshape(dec_batch * n_keys, KV_W), n_valid, n_keys, dec_seq, seqs_per_step=4)
    hs = _mix(xs, oas, us, bs, gas, gcs, _row_history(state_conv_mix[0], dec_seq),
              cw_mix, wao, wco, wmo, g_post, dec_seq, tm=ts)
    fs, a_s = _ffn_up(hs, g_ffn_pre, w_up, cw_ffn, _row_history(state_conv_ffn[0], dec_seq), dec_seq, tm=ts)
    ys = _ffn_down(fs, w_down, hs, g_ffn_post, tm=ts)

    new_k_sample = ks.reshape(1, dec_batch, dec_seq, N_KV_HEADS, HEAD_DIM)
    new_v_sample = vs.reshape(1, dec_batch, dec_seq, N_KV_HEADS, HEAD_DIM)
    conv_mix_sample = us.reshape(dec_batch, dec_seq, D_CONV)[:, dec_seq - (CONV_W - 1):][None]
    conv_ffn_sample = a_s.reshape(dec_batch, dec_seq, D_FF)[:, dec_seq - (CONV_W - 1):][None]

    return (yp.reshape(batch, seq, D_MODEL), ys.reshape(dec_batch, dec_seq, D_MODEL),
            new_k_prompt, new_v_prompt, conv_mix_prompt, conv_ffn_prompt,
            new_k_sample, new_v_sample, conv_mix_sample, conv_ffn_sample)
```

```python
import functools

import jax
import jax.numpy as jnp
from jax import lax
from jax.experimental import pallas as pl
from jax.experimental.pallas import tpu as pltpu

F32 = jnp.float32
BF16 = jnp.bfloat16

D_MODEL = 2048
CHUNK = 64
N_HEADS = 16
N_KV_HEADS = 4
GROUP = N_HEADS // N_KV_HEADS
HEAD_DIM = 64
ROT_DIM = HEAD_DIM // 4
ROPE_THETA = 500000.0
WINDOW = 128
D_CONV = 1024
CONV_W = 3
D_FF = 5632
EPS = 1e-6
NEG = -1e30
PAST_LEN = 4096
Q_W = N_HEADS * HEAD_DIM
KV_W = N_KV_HEADS * HEAD_DIM
D_IN = Q_W + 2 * KV_W + 3 * D_CONV + 2 * D_MODEL

LANES = 128
SUBLANES = 8
VMEM_LIMIT_CAP = 60 * 1024 * 1024
VMEM_COMPILER_SLACK = 6 * 1024 * 1024

TN_IN = 512
TN_FF = 512
FF_UP_TM = 1024
IN_STAGE_ROWS = 64
W2K_STAGE_ROWS = 256
ATT_BLK = 2 * CHUNK
HALO = SUBLANES
ROW_BLK = 256

_T_Q = (0, 2)
_T_KV = (2, 3)
_T_XIN = (3, 5)
_T_B = (5, 7)
_T_C = (7, 9)
_T_GA = (9, 13)
_T_GC = (13, 17)


def _vmem_limit(nbytes):
    return int(min(nbytes + VMEM_COMPILER_SLACK, VMEM_LIMIT_CAP))


def _params(n_axes, nbytes):
    return pltpu.CompilerParams(
        dimension_semantics=("arbitrary",) * n_axes,
        vmem_limit_bytes=_vmem_limit(nbytes),
    )


def _rms_norm(x, g):
    return x * lax.rsqrt(jnp.mean(x * x, axis=-1, keepdims=True) + EPS) * g


def _rope128(x, cos, sin):
    dh = lax.broadcasted_iota(jnp.int32, x.shape, 1) & (HEAD_DIM - 1)
    half = ROT_DIM // 2
    partner = jnp.where(dh < half, pltpu.roll(x, LANES - half, axis=1), pltpu.roll(x, half, axis=1))
    return x * cos + partner * sin


def _shifted_rows(a, prev1_row0, prev2_row0, prev2_row1, period):
    assert period & (period - 1) == 0
    rows = lax.broadcasted_iota(jnp.int32, (a.shape[0], 1), 0) & (period - 1)
    p1 = jnp.where(rows == 0, prev1_row0, pltpu.roll(a, 1, axis=0))
    p2 = jnp.where(rows == 0, prev2_row0, jnp.where(rows == 1, prev2_row1, pltpu.roll(a, 2, axis=0)))
    return p1, p2


def _stage_weight(w_hbm, w_sc, w_out, stage, in_sems, out_sem, step, n_steps):
    chunk = stage.shape[1]
    n_chunks = w_hbm.shape[0] // chunk

    def fetch(c, slot):
        return pltpu.make_async_copy(w_hbm.at[pl.ds(c * chunk, chunk), :], stage.at[slot], in_sems.at[slot])

    export = pltpu.make_async_copy(w_sc, w_out, out_sem)

    @pl.when(step == 0)
    def _():
        fetch(0, 0).start()

        def body(c, carry):
            slot = lax.rem(c, 2)

            @pl.when(c + 1 < n_chunks)
            def _():
                fetch(c + 1, 1 - slot).start()

            fetch(c, slot).wait()
            w_sc[pl.ds(pl.multiple_of(c * chunk, chunk), chunk), :] = stage[slot].astype(BF16)
            return carry

        lax.fori_loop(0, n_chunks, body, 0)
        export.start()

    @pl.when(step == n_steps - 1)
    def _():
        export.wait()


def _staging_scratch(w_shape, chunk):
    assert w_shape[0] % chunk == 0
    return [
        pltpu.VMEM(w_shape, BF16),
        pltpu.VMEM((2, chunk, w_shape[1]), F32),
        pltpu.SemaphoreType.DMA((2,)),
        pltpu.SemaphoreType.DMA(()),
    ]


def _staging_bytes(w_shape, chunk):
    return w_shape[0] * w_shape[1] * 2 + 2 * chunk * w_shape[1] * 4


def _inproj_body(*refs, staged):
    if staged:
        (x_ref, g_ref, cos_ref, sin_ref, w_hbm, q_ref, k_ref, v_ref, u_ref, b_ref, ga_ref, gc_ref,
         w_out, xn_sc, w_ref, stage, in_sems, out_sem) = refs
        _stage_weight(w_hbm, w_ref, w_out, stage, in_sems, out_sem, pl.program_id(0), pl.num_programs(0))
    else:
        x_ref, g_ref, cos_ref, sin_ref, w_ref, q_ref, k_ref, v_ref, u_ref, b_ref, ga_ref, gc_ref, xn_sc = refs
    xn_sc[...] = _rms_norm(x_ref[...], g_ref[...]).astype(BF16)
    cos = cos_ref[...]
    sin = sin_ref[...]

    def col(t):
        return jnp.dot(xn_sc[...], w_ref[:, t * TN_IN:(t + 1) * TN_IN], preferred_element_type=F32)

    def blocks(n):
        return [slice(blk * LANES, (blk + 1) * LANES) for blk in range(n)]

    for t in range(*_T_Q):
        acc = col(t)
        base = (t - _T_Q[0]) * TN_IN
        for sl in blocks(TN_IN // LANES):
            q_ref[:, base + sl.start:base + sl.stop] = (
                _rope128(acc[:, sl], cos, sin) * (HEAD_DIM ** -0.5)).astype(BF16)
    acc = col(_T_KV[0])
    for sl in blocks(KV_W // LANES):
        k_ref[:, sl] = _rope128(acc[:, sl], cos, sin)
    v_ref[...] = acc[:, KV_W:2 * KV_W]
    for c in range(_T_C[1] - _T_C[0]):
        cols = slice(c * TN_IN, (c + 1) * TN_IN)
        u_ref[:, cols] = col(_T_C[0] + c) * col(_T_XIN[0] + c)
        b_ref[:, cols] = col(_T_B[0] + c).astype(BF16)
    for c in range(_T_GA[1] - _T_GA[0]):
        cols = slice(c * TN_IN, (c + 1) * TN_IN)
        ga_ref[:, cols] = col(_T_GA[0] + c).astype(BF16)
        gc_ref[:, cols] = col(_T_GC[0] + c).astype(BF16)


def _inproj(x, g, cos, sin, w, tm):
    t = x.shape[0]
    n_pos = cos.shape[0] // tm
    row = lambda i: (i, 0)
    fixed = lambda i: (0, 0)
    pos = lambda i: (i % n_pos, 0)
    widths = (Q_W, KV_W, KV_W, D_CONV, D_CONV, D_MODEL, D_MODEL)
    dtypes = (BF16, F32, F32, F32, BF16, BF16, BF16)
    out_row_bytes = sum(wd * jnp.dtype(dt).itemsize for wd, dt in zip(widths, dtypes))
    nbytes = (2 * tm * (D_MODEL * 4 + 2 * LANES * 4 + out_row_bytes)
              + tm * D_MODEL * 2 + 6 * tm * TN_IN * 4)
    out_specs = [pl.BlockSpec((tm, wd), row) for wd in widths]
    out_shape = [jax.ShapeDtypeStruct((t, wd), dt) for wd, dt in zip(widths, dtypes)]
    scratch = [pltpu.VMEM((tm, D_MODEL), BF16)]
    staged = w.dtype == F32
    if staged:
        w_spec = pl.BlockSpec(memory_space=pl.ANY)
        out_specs.append(pl.BlockSpec(memory_space=pl.ANY))
        out_shape.append(jax.ShapeDtypeStruct(w.shape, BF16))
        scratch += _staging_scratch(w.shape, IN_STAGE_ROWS)
        nbytes += _staging_bytes(w.shape, IN_STAGE_ROWS)
    else:
        w_spec = pl.BlockSpec((D_MODEL, D_IN), fixed, pipeline_mode=pl.Buffered(1))
        nbytes += D_MODEL * D_IN * 2
    return pl.pallas_call(
        functools.partial(_inproj_body, staged=staged),
        grid=(t // tm,),
        in_specs=[
            pl.BlockSpec((tm, D_MODEL), row),
            pl.BlockSpec((1, D_MODEL), fixed),
            pl.BlockSpec((tm, LANES), pos),
            pl.BlockSpec((tm, LANES), pos),
            w_spec,
        ],
        out_specs=tuple(out_specs),
        out_shape=tuple(out_shape),
        scratch_shapes=scratch,
        compiler_params=_params(1, nbytes),
        name="inproj",
    )(x, g, cos, sin, w)


def _head_pair_operands(x256, h):
    blk = x256[:, (h // 2) * LANES:(h // 2 + 1) * LANES]
    lo_lanes = lax.broadcasted_iota(jnp.int32, blk.shape, 1) < HEAD_DIM
    if h % 2 == 0:
        lo = jnp.where(lo_lanes, blk, 0.0)
        hi = pltpu.roll(lo, HEAD_DIM, axis=1)
    else:
        hi = jnp.where(lo_lanes, 0.0, blk)
        lo = pltpu.roll(hi, HEAD_DIM, axis=1)
    return jnp.concatenate([lo, hi], axis=0).astype(BF16)


def _attend(q_ref, o_ref, sink_ref, blocks, s_sc, m_sc, p_sc, v_sc):
    n_keys = blocks[0][1].shape[0]
    n_rows = s_sc.shape[1]
    key_row = lax.broadcasted_iota(jnp.int32, (2 * n_keys, LANES), 0)
    key_lane = lax.broadcasted_iota(jnp.int32, (2 * n_keys, LANES), 1)
    ones_cat = jnp.where((key_row >= n_keys) == (key_lane >= HEAD_DIM), 1.0, 0.0).astype(BF16)
    lo_lanes = lax.broadcasted_iota(jnp.int32, (n_rows, LANES), 1) < HEAD_DIM
    halves = [slice(hh * n_keys, (hh + 1) * n_keys) for hh in range(2)]

    tiles = []
    n_v = 0
    for q_rows, k_all, v_all, valid in blocks:
        for h in range(N_KV_HEADS):
            kcat = _head_pair_operands(k_all, h)
            v_sc[n_v] = jnp.concatenate([_head_pair_operands(v_all, h), ones_cat], axis=1)
            for p in range(2 * h, 2 * h + 2):
                lanes = slice(p * LANES, (p + 1) * LANES)
                sc = lax.dot_general(q_ref[q_rows, lanes], kcat, (((1,), (1,)), ((), ())),
                                     preferred_element_type=F32)
                s_sc[len(tiles)] = jnp.where(valid, sc, NEG)
                tiles.append((q_rows, lanes, p, n_v))
            n_v += 1
    for t, (_, _, p, _) in enumerate(tiles):
        for hh in range(2):
            m = jnp.maximum(jnp.max(s_sc[t, :, halves[hh]], axis=-1, keepdims=True), sink_ref[2 * p + hh])
            m_sc[2 * t + hh] = jnp.broadcast_to(m, (n_rows, LANES))
    for t in range(len(tiles)):
        p_sc[t] = jnp.concatenate(
            [jnp.exp(s_sc[t, :, halves[hh]] - jnp.tile(m_sc[2 * t + hh], (1, n_keys // LANES))).astype(BF16)
             for hh in range(2)], axis=1)
    for t, (q_rows, lanes, p, vi) in enumerate(tiles):
        e = p_sc[t]
        num_den = jnp.dot(e, v_sc[vi], preferred_element_type=F32)
        num = num_den[:, :LANES]
        den = num_den[:, LANES:]
        sink_term = jnp.where(lo_lanes, jnp.exp(sink_ref[2 * p] - m_sc[2 * t]),
                              jnp.exp(sink_ref[2 * p + 1] - m_sc[2 * t + 1]))
        o_ref[q_rows, lanes] = (num / (den + sink_term)).astype(BF16)


def _attn_scratch(n_blocks, n_rows, n_keys):
    n_tiles = n_blocks * N_HEADS // 2
    return [
        pltpu.VMEM((n_tiles, n_rows, 2 * n_keys), F32),
        pltpu.VMEM((2 * n_tiles, n_rows, LANES), F32),
        pltpu.VMEM((n_tiles, n_rows, 2 * n_keys), BF16),
        pltpu.VMEM((n_blocks * N_KV_HEADS, 2 * n_keys, 2 * LANES), BF16),
    ]


def _attn_scratch_bytes(n_blocks, n_rows, n_keys):
    n_tiles = n_blocks * N_HEADS // 2
    return (n_tiles * n_rows * 2 * n_keys * 6 + 2 * n_tiles * n_rows * LANES * 4
            + n_blocks * N_KV_HEADS * 2 * n_keys * 2 * LANES * 2)


def _attn_prompt_body(sink_ref, q_ref, kc_ref, kp_ref, vc_ref, vp_ref, o_ref, *scratch, tiles_per_seq):
    first = (pl.program_id(0) % tiles_per_seq) == 0
    n_sub = q_ref.shape[0] // ATT_BLK
    n_keys = 2 * ATT_BLK
    shape = (ATT_BLK, 2 * n_keys)
    q_chunk = lax.broadcasted_iota(jnp.int32, shape, 0) // CHUNK
    k_chunk = (lax.broadcasted_iota(jnp.int32, shape, 1) & (n_keys - 1)) // CHUNK
    band = (k_chunk >= q_chunk) & (k_chunk <= q_chunk + WINDOW // CHUNK)
    first_valid = band & (k_chunk >= jnp.where(first, ATT_BLK // CHUNK, 0))
    blocks = []
    for s in range(n_sub):
        rows = slice(s * ATT_BLK, (s + 1) * ATT_BLK)
        if s == 0:
            k_prev, v_prev, valid = kp_ref[...], vp_ref[...], first_valid
        else:
            prev = slice((s - 1) * ATT_BLK, s * ATT_BLK)
            k_prev, v_prev, valid = kc_ref[prev, :], vc_ref[prev, :], band
        k_all = jnp.concatenate([k_prev, kc_ref[rows, :]], axis=0)
        v_all = jnp.concatenate([v_prev, vc_ref[rows, :]], axis=0)
        blocks.append((rows, k_all, v_all, valid))
    _attend(q_ref, o_ref, sink_ref, blocks, *scratch)


def _attn_prompt(sinks, q, k, v, seq, tq):
    t = q.shape[0]
    ratio = tq // ATT_BLK
    cur = lambda i: (i, 0)
    prev = lambda i: (jnp.maximum(i * ratio - 1, 0), 0)
    nbytes = (2 * (2 * tq * Q_W * 2 + 2 * (tq + ATT_BLK) * KV_W * 4)
              + _attn_scratch_bytes(ratio, ATT_BLK, 2 * ATT_BLK))
    return pl.pallas_call(
        functools.partial(_attn_prompt_body, tiles_per_seq=seq // tq),
        grid=(t // tq,),
        in_specs=[
            pl.BlockSpec(memory_space=pltpu.SMEM),
            pl.BlockSpec((tq, Q_W), cur),
            pl.BlockSpec((tq, KV_W), cur),
            pl.BlockSpec((ATT_BLK, KV_W), prev),
            pl.BlockSpec((tq, KV_W), cur),
            pl.BlockSpec((ATT_BLK, KV_W), prev),
        ],
        out_specs=pl.BlockSpec((tq, Q_W), cur),
        out_shape=jax.ShapeDtypeStruct((t, Q_W), BF16),
        scratch_shapes=_attn_scratch(ratio, ATT_BLK, 2 * ATT_BLK),
        compiler_params=_params(1, nbytes),
        name="attn_prompt",
    )(sinks, q, k, k, v, v)


def _attn_sample_body(sink_ref, q_ref, k_ref, v_ref, o_ref, *scratch, n_valid, n_keys, rows_per_seq):
    shape = (rows_per_seq, 2 * n_keys)
    valid = (lax.broadcasted_iota(jnp.int32, shape, 1) & (n_keys - 1)) < n_valid
    blocks = []
    for b in range(q_ref.shape[0] // rows_per_seq):
        keys = slice(b * n_keys, (b + 1) * n_keys)
        blocks.append((slice(b * rows_per_seq, (b + 1) * rows_per_seq), k_ref[keys, :], v_ref[keys, :], valid))
    _attend(q_ref, o_ref, sink_ref, blocks, *scratch)


def _attn_sample(sinks, q, k_all, v_all, n_valid, n_keys, rows_per_seq, seqs_per_step):
    t = q.shape[0]
    tq = rows_per_seq * seqs_per_step
    tk = n_keys * seqs_per_step
    nbytes = 2 * (2 * tq * Q_W * 2 + 2 * tk * KV_W * 4) + _attn_scratch_bytes(seqs_per_step, rows_per_seq, n_keys)
    return pl.pallas_call(
        functools.partial(_attn_sample_body, n_valid=n_valid, n_keys=n_keys, rows_per_seq=rows_per_seq),
        grid=(t // tq,),
        in_specs=[
            pl.BlockSpec(memory_space=pltpu.SMEM),
            pl.BlockSpec((tq, Q_W), lambda i: (i, 0)),
            pl.BlockSpec((tk, KV_W), lambda i: (i, 0)),
            pl.BlockSpec((tk, KV_W), lambda i: (i, 0)),
        ],
        out_specs=pl.BlockSpec((tq, Q_W), lambda i: (i, 0)),
        out_shape=jax.ShapeDtypeStruct((t, Q_W), BF16),
        scratch_shapes=_attn_scratch(seqs_per_step, rows_per_seq, n_keys),
        compiler_params=_params(1, nbytes),
        name="attn_sample",
    )(sinks, q, k_all, v_all)


def _mix_body(*refs, tiles_per_seq, hist_period, staged):
    refs = list(refs)
    x_ref, oa_ref, u_ref, b_ref, ga_ref, gc_ref = refs[:6]
    del refs[:6]
    if hist_period:
        p1_ref, p2_ref = refs[:2]
        del refs[:2]
    cw_ref, wao_ref, wco_ref, wmo_ref, g_ref, h_ref = refs[:6]
    del refs[:6]
    if staged:
        w_outs = refs[:3]
        del refs[:3]
    if not hist_period:
        carry_sc = refs.pop(0)
    if staged:
        w_scs, (stage, in_sems), out_sems = refs[:3], refs[3:5], refs[5:8]
        step, n_steps = pl.program_id(0), pl.num_programs(0)
        for w_hbm, w_sc, w_out, out_sem in zip((wao_ref, wco_ref, wmo_ref), w_scs, w_outs, out_sems):
            _stage_weight(w_hbm, w_sc, w_out, stage, in_sems, out_sem, step, n_steps)
        wao_ref, wco_ref, wmo_ref = w_scs
    u = u_ref[...]
    tm = u.shape[0]
    if hist_period:
        p1, p2 = _shifted_rows(u, p1_ref[...], p2_ref[...], p2_ref[...], hist_period)
    else:
        @pl.when(pl.program_id(0) % tiles_per_seq == 0)
        def _():
            carry_sc[...] = jnp.zeros_like(carry_sc)

        last1 = carry_sc[HALO - 1:HALO, :]
        last2 = carry_sc[HALO - 2:HALO - 1, :]
        p1, p2 = _shifted_rows(u, last1, last2, last1, tm)
        carry_sc[...] = u[tm - HALO:tm, :]
    cy = cw_ref[0:1, :] * p2 + cw_ref[1:2, :] * p1 + cw_ref[2:3, :] * u
    y_conv = (b_ref[...].astype(F32) * cy).astype(BF16)
    attn_o = jnp.dot(oa_ref[...], wao_ref[...], preferred_element_type=F32)
    conv_o = jnp.dot(y_conv, wco_ref[...], preferred_element_type=F32)
    mixed = (jax.nn.sigmoid(ga_ref[...].astype(F32)) * attn_o
             + jax.nn.sigmoid(gc_ref[...].astype(F32)) * conv_o)
    mo = jnp.dot(mixed.astype(BF16), wmo_ref[...], preferred_element_type=F32)
    h_ref[...] = x_ref[...] + _rms_norm(mo, g_ref[...])


def _mix(x, oa, u, b, ga, gc, hist, cw, wao, wco, wmo, g, seq, tm):
    t = x.shape[0]
    row = lambda i: (i, 0)
    fixed = lambda i: (0, 0)
    act_specs = [
        pl.BlockSpec((tm, D_MODEL), row),
        pl.BlockSpec((tm, Q_W), row),
        pl.BlockSpec((tm, D_CONV), row),
        pl.BlockSpec((tm, D_CONV), row),
        pl.BlockSpec((tm, D_MODEL), row),
        pl.BlockSpec((tm, D_MODEL), row),
    ]
    acts = [x, oa, u, b, ga, gc]
    scratch = []
    if hist is None:
        hist_period = 0
        scratch.append(pltpu.VMEM((HALO, D_CONV), F32))
    else:
        hist_period = seq
        act_specs += [pl.BlockSpec((tm, D_CONV), row)] * 2
        acts += list(hist)
    weights = (wao, wco, wmo)
    staged = wao.dtype == F32
    out_specs = [pl.BlockSpec((tm, D_MODEL), row)]
    out_shape = [jax.ShapeDtypeStruct((t, D_MODEL), F32)]
    row_bytes = D_MODEL * 4 * 2 + Q_W * 2 + D_CONV * (4 + 2) + D_MODEL * 2 * 2
    if hist is not None:
        row_bytes += 2 * D_CONV * 4
    nbytes = 2 * tm * row_bytes + (Q_W + D_CONV + D_MODEL) * D_MODEL * 2 + 6 * tm * D_MODEL * 4
    if staged:
        w_specs = [pl.BlockSpec(memory_space=pl.ANY)] * 3
        out_specs += [pl.BlockSpec(memory_space=pl.ANY)] * 3
        out_shape += [jax.ShapeDtypeStruct(w.shape, BF16) for w in weights]
        scratch += [pltpu.VMEM(w.shape, BF16) for w in weights]
        scratch += [pltpu.VMEM((2, W2K_STAGE_ROWS, D_MODEL), F32), pltpu.SemaphoreType.DMA((2,))]
        scratch += [pltpu.SemaphoreType.DMA(())] * 3
        nbytes += 2 * W2K_STAGE_ROWS * D_MODEL * 4
    else:
        w_specs = [pl.BlockSpec(w.shape, fixed, pipeline_mode=pl.Buffered(1)) for w in weights]
    in_specs = (act_specs + [pl.BlockSpec((CONV_W, D_CONV), fixed)] + w_specs
                + [pl.BlockSpec((1, D_MODEL), fixed)])
    return pl.pallas_call(
        functools.partial(_mix_body, tiles_per_seq=max(seq // tm, 1), hist_period=hist_period, staged=staged),
        grid=(t // tm,),
        in_specs=in_specs,
        out_specs=tuple(out_specs),
        out_shape=tuple(out_shape),
        scratch_shapes=scratch,
        compiler_params=_params(1, nbytes),
        name="mix",
    )(*acts, cw, wao, wco, wmo, g)


def _ffn_up_body(*refs, tiles_per_seq, hist_period):
    if hist_period:
        h_ref, g_ref, wa_ref, wg_ref, cw_ref, p1_ref, p2_ref, f_ref, a_ref, xn_sc = refs
    else:
        h_ref, g_ref, wa_ref, wg_ref, cw_ref, f_ref, tail_ref, xn_sc, carry_sc = refs
    i = pl.program_id(0)
    j = pl.program_id(1)

    @pl.when(j == 0)
    def _():
        xn_sc[...] = _rms_norm(h_ref[...], g_ref[...]).astype(BF16)

    if not hist_period:
        @pl.when(i % tiles_per_seq == 0)
        def _():
            carry_sc[j] = jnp.zeros(carry_sc.shape[1:], F32)

        tail = carry_sc[j]
    tm = xn_sc.shape[0]
    rb = min(tm, ROW_BLK)
    for r in range(tm // rb):
        rows = slice(r * rb, (r + 1) * rb)
        xn = xn_sc[rows, :]
        a = jnp.dot(xn, wa_ref[...], preferred_element_type=F32)
        gate = jnp.dot(xn, wg_ref[...], preferred_element_type=F32)
        if hist_period:
            p1, p2 = _shifted_rows(a, p1_ref[rows, :], p2_ref[rows, :], p2_ref[rows, :], hist_period)
            a_ref[rows, :] = a
        else:
            last1 = tail[HALO - 1:HALO, :]
            p1, p2 = _shifted_rows(a, last1, tail[HALO - 2:HALO - 1, :], last1, rb)
            tail = a[rb - HALO:rb, :]
        ua = cw_ref[0:1, :] * p2 + cw_ref[1:2, :] * p1 + cw_ref[2:3, :] * a
        f_ref[rows, :] = (jax.nn.gelu(ua, approximate=True) * gate).astype(BF16)
    if not hist_period:
        carry_sc[j] = tail
        tail_ref[...] = tail


def _ffn_up(h, g, w_up, cw, hist, seq, tm):
    t = h.shape[0]
    n_j = D_FF // TN_FF
    grid = (t // tm, n_j)
    in_specs = [
        pl.BlockSpec((tm, D_MODEL), lambda i, j: (i, 0)),
        pl.BlockSpec((1, D_MODEL), lambda i, j: (0, 0)),
        pl.BlockSpec((D_MODEL, TN_FF), lambda i, j: (0, j)),
        pl.BlockSpec((D_MODEL, TN_FF), lambda i, j: (0, j + n_j)),
        pl.BlockSpec((CONV_W, TN_FF), lambda i, j: (0, j)),
    ]
    args = [h, g, w_up, w_up, cw]
    scratch = [pltpu.VMEM((tm, D_MODEL), BF16)]
    f_spec = pl.BlockSpec((tm, TN_FF), lambda i, j: (i, j))
    f_shape = jax.ShapeDtypeStruct((t, D_FF), BF16)
    if hist is None:
        hist_period = 0
        scratch.append(pltpu.VMEM((n_j, HALO, TN_FF), F32))
        out_specs = (f_spec, pl.BlockSpec((None, HALO, TN_FF), lambda i, j: (i, 0, j)))
        out_shape = (f_shape, jax.ShapeDtypeStruct((t // tm, HALO, D_FF), F32))
    else:
        hist_period = seq
        in_specs += [pl.BlockSpec((tm, TN_FF), lambda i, j: (i, j))] * 2
        args += list(hist)
        out_specs = (f_spec, pl.BlockSpec((tm, TN_FF), lambda i, j: (i, j)))
        out_shape = (f_shape, jax.ShapeDtypeStruct((t, D_FF), F32))
    nbytes = (2 * tm * D_MODEL * 4 + tm * D_MODEL * 2 + 4 * D_MODEL * TN_FF * 2
              + 2 * tm * TN_FF * (2 + 4 * 3) + 8 * tm * TN_FF * 4)
    return pl.pallas_call(
        functools.partial(_ffn_up_body, tiles_per_seq=max(seq // tm, 1), hist_period=hist_period),
        grid=grid,
        in_specs=in_specs,
        out_specs=out_specs,
        out_shape=out_shape,
        scratch_shapes=scratch,
        compiler_params=_params(2, nbytes),
        name="ffn_up",
    )(*args)


def _ffn_down_body(*refs, staged):
    if staged:
        f_ref, w_hbm, h_ref, g_ref, o_ref, w_out, w_ref, stage, in_sems, out_sem = refs
        _stage_weight(w_hbm, w_ref, w_out, stage, in_sems, out_sem, pl.program_id(0), pl.num_programs(0))
    else:
        f_ref, w_ref, h_ref, g_ref, o_ref = refs
    tm = f_ref.shape[0]
    rb = min(tm, ROW_BLK)
    for r in range(tm // rb):
        rows = slice(r * rb, (r + 1) * rb)
        acc = jnp.dot(f_ref[rows, :], w_ref[...], preferred_element_type=F32)
        o_ref[rows, :] = h_ref[rows, :] + _rms_norm(acc, g_ref[...])


def _ffn_down(f, w, h, g, tm):
    t = f.shape[0]
    nbytes = 2 * tm * (D_FF * 2 + D_MODEL * 4 * 2) + 2 * min(tm, ROW_BLK) * D_MODEL * 4
    out_specs = [pl.BlockSpec((tm, D_MODEL), lambda i: (i, 0))]
    out_shape = [jax.ShapeDtypeStruct((t, D_MODEL), F32)]
    scratch = []
    staged = w.dtype == F32
    if staged:
        w_spec = pl.BlockSpec(memory_space=pl.ANY)
        out_specs.append(pl.BlockSpec(memory_space=pl.ANY))
        out_shape.append(jax.ShapeDtypeStruct(w.shape, BF16))
        scratch += _staging_scratch(w.shape, W2K_STAGE_ROWS)
        nbytes += _staging_bytes(w.shape, W2K_STAGE_ROWS)
    else:
        w_spec = pl.BlockSpec((D_FF, D_MODEL), lambda i: (0, 0), pipeline_mode=pl.Buffered(1))
        nbytes += D_FF * D_MODEL * 2
    return pl.pallas_call(
        functools.partial(_ffn_down_body, staged=staged),
        grid=(t // tm,),
        in_specs=[
            pl.BlockSpec((tm, D_FF), lambda i: (i, 0)),
            w_spec,
            pl.BlockSpec((tm, D_MODEL), lambda i: (i, 0)),
            pl.BlockSpec((1, D_MODEL), lambda i: (0, 0)),
        ],
        out_specs=tuple(out_specs),
        out_shape=tuple(out_shape),
        scratch_shapes=scratch,
        compiler_params=_params(1, nbytes),
        name="ffn_down",
    )(f, w, h, g)


def _rope_tables(pos):
    half = ROT_DIM // 2
    inv = jnp.power(jnp.float32(ROPE_THETA), -jnp.arange(half, dtype=F32) * (2.0 / ROT_DIM))
    ang = pos.astype(F32)[:, None] * inv[None, :]
    cos = jnp.cos(ang)
    sin = jnp.sin(ang)
    rest = HEAD_DIM - ROT_DIM
    cos_h = jnp.concatenate([cos, cos, jnp.ones((pos.shape[0], rest), F32)], axis=1)
    sin_h = jnp.concatenate([-sin, sin, jnp.zeros((pos.shape[0], rest), F32)], axis=1)
    reps = LANES // HEAD_DIM
    return jnp.tile(cos_h, (1, reps)), jnp.tile(sin_h, (1, reps))


def _row_history(state, seq):
    b, _, c = state.shape
    p1 = jnp.concatenate([state[:, 1:2], jnp.zeros((b, seq - 1, c), F32)], axis=1)
    p2 = jnp.concatenate([state[:, 0:1], state[:, 1:2], jnp.zeros((b, seq - 2, c), F32)], axis=1)
    return p1.reshape(b * seq, c), p2.reshape(b * seq, c)


def kernel(x_prompt, x_sample, cache_k, cache_v, state_conv_mix, state_conv_ffn, norm_mix_pre, w_in,
           attn_sinks, conv_mix_w, w_attn_o, w_conv_o, w_mix_out, norm_mix_post, norm_ffn_pre,
           w_ffn_up, conv_ffn_w, w_ffn_down, norm_ffn_post):
    assert w_in.shape == (1, D_MODEL, D_IN), "one layer only"
    batch, seq, _ = x_prompt.shape
    dec_batch, dec_seq, _ = x_sample.shape
    assert seq % 1024 == 0 and (dec_batch * dec_seq) % 256 == 0 and dec_seq % 16 == 0
    assert cache_k.shape[2] == WINDOW

    g_pre = norm_mix_pre[0][None]
    g_post = norm_mix_post[0][None]
    g_ffn_pre = norm_ffn_pre[0][None]
    g_ffn_post = norm_ffn_post[0][None]
    w_up = w_ffn_up[0].astype(BF16)
    sinks = attn_sinks[0]
    cw_mix = conv_mix_w[0]
    cw_ffn = conv_ffn_w[0]

    tp = batch * seq
    xp = x_prompt.reshape(tp, D_MODEL)
    cos_p, sin_p = _rope_tables(jnp.arange(seq, dtype=F32))
    q, k, v, u, b, ga, gc, w_in_b = _inproj(xp, g_pre, cos_p, sin_p, w_in[0], tm=256)
    oa = _attn_prompt(sinks, q, k, v, seq, tq=256)
    hp, wao, wco, wmo = _mix(xp, oa, u, b, ga, gc, None, cw_mix, w_attn_o[0], w_conv_o[0], w_mix_out[0],
                             g_post, seq, tm=256)
    f, a_tail = _ffn_up(hp, g_ffn_pre, w_up, cw_ffn, None, seq, tm=FF_UP_TM)
    yp, w_down = _ffn_down(f, w_ffn_down[0], hp, g_ffn_post, tm=512)

    keep = min(WINDOW, seq)
    new_k_prompt = k.reshape(batch, seq, KV_W)[:, seq - keep:].reshape(1, batch, keep, N_KV_HEADS, HEAD_DIM)
    new_v_prompt = v.reshape(batch, seq, KV_W)[:, seq - keep:].reshape(1, batch, keep, N_KV_HEADS, HEAD_DIM)
    conv_mix_prompt = u.reshape(batch, seq, D_CONV)[:, seq - (CONV_W - 1):][None]
    a_tail = a_tail.reshape(batch, seq // FF_UP_TM, HALO, D_FF)
    conv_ffn_prompt = a_tail[:, -1, HALO - (CONV_W - 1):][None]

    ts = dec_batch * dec_seq
    xs = x_sample.reshape(ts, D_MODEL)
    pos_s = PAST_LEN + jnp.arange(dec_seq, dtype=F32)
    cos_s, sin_s = _rope_tables(jnp.tile(pos_s, dec_batch))
    qs, ks, vs, us, bs, gas, gcs = _inproj(xs, g_pre, cos_s, sin_s, w_in_b, tm=ts)
    n_valid = WINDOW + dec_seq
    n_keys = 2 * WINDOW
    pad = jnp.zeros((dec_batch, n_keys - n_valid, KV_W), F32)
    k_all = jnp.concatenate([cache_k[0].reshape(dec_batch, WINDOW, KV_W),
                             ks.reshape(dec_batch, dec_seq, KV_W), pad], axis=1)
    v_all = jnp.concatenate([cache_v[0].reshape(dec_batch, WINDOW, KV_W),
                             vs.reshape(dec_batch, dec_seq, KV_W), pad], axis=1)
    oas = _attn_sample(sinks, qs, k_all.reshape(dec_batch * n_keys, KV_W),
                       v_all.reshape(dec_batch * n_keys, KV_W), n_valid, n_keys, dec_seq, seqs_per_step=4)
    (hs,) = _mix(xs, oas, us, bs, gas, gcs, _row_history(state_conv_mix[0], dec_seq),
                 cw_mix, wao, wco, wmo, g_post, dec_seq, tm=ts)
    fs, a_s = _ffn_up(hs, g_ffn_pre, w_up, cw_ffn, _row_history(state_conv_ffn[0], dec_seq), dec_seq, tm=ts)
    (ys,) = _ffn_down(fs, w_down, hs, g_ffn_post, tm=ts)

    new_k_sample = ks.reshape(1, dec_batch, dec_seq, N_KV_HEADS, HEAD_DIM)
    new_v_sample = vs.reshape(1, dec_batch, dec_seq, N_KV_HEADS, HEAD_DIM)
    conv_mix_sample = us.reshape(dec_batch, dec_seq, D_CONV)[:, dec_seq - (CONV_W - 1):][None]
    conv_ffn_sample = a_s.reshape(dec_batch, dec_seq, D_FF)[:, dec_seq - (CONV_W - 1):][None]

    return (yp.reshape(batch, seq, D_MODEL), ys.reshape(dec_batch, dec_seq, D_MODEL),
            new_k_prompt, new_v_prompt, conv_mix_prompt, conv_ffn_prompt,
            new_k_sample, new_v_sample, conv_mix_sample, conv_ffn_sample)
```

```python
import functools

import jax
import jax.numpy as jnp
from jax import lax
from jax.experimental import pallas as pl
from jax.experimental.pallas import tpu as pltpu

F32 = jnp.float32
BF16 = jnp.bfloat16

D_MODEL = 2048
CHUNK = 64
N_HEADS = 16
N_KV_HEADS = 4
GROUP = N_HEADS // N_KV_HEADS
HEAD_DIM = 64
ROT_DIM = HEAD_DIM // 4
ROPE_THETA = 500000.0
WINDOW = 128
D_CONV = 1024
CONV_W = 3
D_FF = 5632
EPS = 1e-6
NEG = -1e30
PAST_LEN = 4096
Q_W = N_HEADS * HEAD_DIM
KV_W = N_KV_HEADS * HEAD_DIM
D_IN = Q_W + 2 * KV_W + 3 * D_CONV + 2 * D_MODEL

LANES = 128
SUBLANES = 8
VMEM_LIMIT_CAP = 60 * 1024 * 1024
VMEM_COMPILER_SLACK = 6 * 1024 * 1024

TN_IN = 512
TN_FF = 512
FF_UP_TM = 1024
IN_STAGE_ROWS = 64
BG_ROWS_2K = 128
BG_ROWS_UP = 64
ATT_BLK = 2 * CHUNK
HALO = SUBLANES
ROW_BLK = 256

_T_Q = (0, 2)
_T_KV = (2, 3)
_T_XIN = (3, 5)
_T_B = (5, 7)
_T_C = (7, 9)
_T_GA = (9, 13)
_T_GC = (13, 17)


def _vmem_limit(nbytes):
    return int(min(nbytes + VMEM_COMPILER_SLACK, VMEM_LIMIT_CAP))


def _params(n_axes, nbytes):
    return pltpu.CompilerParams(
        dimension_semantics=("arbitrary",) * n_axes,
        vmem_limit_bytes=_vmem_limit(nbytes),
    )


def _rms_norm(x, g):
    return x * lax.rsqrt(jnp.mean(x * x, axis=-1, keepdims=True) + EPS) * g


def _rope128(x, cos, sin):
    dh = lax.broadcasted_iota(jnp.int32, x.shape, 1) & (HEAD_DIM - 1)
    half = ROT_DIM // 2
    partner = jnp.where(dh < half, pltpu.roll(x, LANES - half, axis=1), pltpu.roll(x, half, axis=1))
    return x * cos + partner * sin


def _shifted_rows(a, prev1_row0, prev2_row0, prev2_row1, period):
    assert period & (period - 1) == 0
    rows = lax.broadcasted_iota(jnp.int32, (a.shape[0], 1), 0) & (period - 1)
    p1 = jnp.where(rows == 0, prev1_row0, pltpu.roll(a, 1, axis=0))
    p2 = jnp.where(rows == 0, prev2_row0, jnp.where(rows == 1, prev2_row1, pltpu.roll(a, 2, axis=0)))
    return p1, p2


def _stage_weight(w_hbm, w_sc, w_out, stage, in_sems, out_sem, step, n_steps):
    chunk = stage.shape[1]
    n_chunks = w_hbm.shape[0] // chunk

    def fetch(c, slot):
        return pltpu.make_async_copy(w_hbm.at[pl.ds(c * chunk, chunk), :], stage.at[slot], in_sems.at[slot])

    export = pltpu.make_async_copy(w_sc, w_out, out_sem)

    @pl.when(step == 0)
    def _():
        fetch(0, 0).start()

        def body(c, carry):
            slot = lax.rem(c, 2)

            @pl.when(c + 1 < n_chunks)
            def _():
                fetch(c + 1, 1 - slot).start()

            fetch(c, slot).wait()
            w_sc[pl.ds(pl.multiple_of(c * chunk, chunk), chunk), :] = stage[slot].astype(BF16)
            return carry

        lax.fori_loop(0, n_chunks, body, 0)
        export.start()

    @pl.when(step == n_steps - 1)
    def _():
        export.wait()


def _background_cast(jobs, buf_in, buf_out, sems, step, n_steps):
    chunk = buf_in.shape[1]
    slot_now = lax.rem(step, 2)
    slot_prev = 1 - slot_now
    first = 0
    for src, dst in jobs:
        n = src.shape[0] // chunk
        assert src.shape[0] % chunk == 0 and first + n + 2 <= n_steps

        def fetch(c, slot, src=src):
            return pltpu.make_async_copy(src.at[pl.ds(c * chunk, chunk), :], buf_in.at[slot], sems.at[0, slot])

        def send(c, slot, dst=dst):
            return pltpu.make_async_copy(buf_out.at[slot], dst.at[pl.ds(c * chunk, chunk), :], sems.at[1, slot])

        c_fetch = step - first
        c_cast = c_fetch - 1
        c_done = c_fetch - 2

        @pl.when((c_done >= 0) & (c_done < n))
        def _():
            send(c_done, slot_now).wait()

        @pl.when((c_cast >= 0) & (c_cast < n))
        def _():
            fetch(c_cast, slot_prev).wait()
            buf_out[slot_prev] = buf_in[slot_prev].astype(BF16)
            send(c_cast, slot_prev).start()

        @pl.when((c_fetch >= 0) & (c_fetch < n))
        def _():
            fetch(c_fetch, slot_now).start()

        first += n


def _background_scratch(chunk, cols):
    return [
        pltpu.VMEM((2, chunk, cols), F32),
        pltpu.VMEM((2, chunk, cols), BF16),
        pltpu.SemaphoreType.DMA((2, 2)),
    ]


def _background_bytes(chunk, cols):
    return 2 * chunk * cols * (4 + 2)


def _staging_scratch(w_shape, chunk):
    assert w_shape[0] % chunk == 0
    return [
        pltpu.VMEM(w_shape, BF16),
        pltpu.VMEM((2, chunk, w_shape[1]), F32),
        pltpu.SemaphoreType.DMA((2,)),
        pltpu.SemaphoreType.DMA(()),
    ]


def _staging_bytes(w_shape, chunk):
    return w_shape[0] * w_shape[1] * 2 + 2 * chunk * w_shape[1] * 4


def _inproj_body(*refs, staged, n_bg, n_steps):
    refs = list(refs)
    x_ref, g_ref, cos_ref, sin_ref, w_ref = refs[:5]
    bg_src = refs[5:5 + n_bg]
    del refs[:5 + n_bg]
    q_ref, k_ref, v_ref, u_ref, b_ref, ga_ref, gc_ref = refs[:7]
    del refs[:7]
    if staged:
        w_hbm, w_out = w_ref, refs.pop(0)
    bg_dst = refs[:n_bg]
    del refs[:n_bg]
    xn_sc = refs.pop(0)
    step = pl.program_id(0)
    if staged:
        w_ref, stage, in_sems, out_sem = refs[:4]
        del refs[:4]
        _stage_weight(w_hbm, w_ref, w_out, stage, in_sems, out_sem, step, n_steps)
    if n_bg:
        _background_cast(list(zip(bg_src, bg_dst)), *refs, step, n_steps)
    xn_sc[...] = _rms_norm(x_ref[...], g_ref[...]).astype(BF16)
    cos = cos_ref[...]
    sin = sin_ref[...]

    def col(t):
        return jnp.dot(xn_sc[...], w_ref[:, t * TN_IN:(t + 1) * TN_IN], preferred_element_type=F32)

    def blocks(n):
        return [slice(blk * LANES, (blk + 1) * LANES) for blk in range(n)]

    for t in range(*_T_Q):
        acc = col(t)
        base = (t - _T_Q[0]) * TN_IN
        for sl in blocks(TN_IN // LANES):
            q_ref[:, base + sl.start:base + sl.stop] = (
                _rope128(acc[:, sl], cos, sin) * (HEAD_DIM ** -0.5)).astype(BF16)
    acc = col(_T_KV[0])
    for sl in blocks(KV_W // LANES):
        k_ref[:, sl] = _rope128(acc[:, sl], cos, sin)
    v_ref[...] = acc[:, KV_W:2 * KV_W]
    for c in range(_T_C[1] - _T_C[0]):
        cols = slice(c * TN_IN, (c + 1) * TN_IN)
        u_ref[:, cols] = col(_T_C[0] + c) * col(_T_XIN[0] + c)
        b_ref[:, cols] = col(_T_B[0] + c).astype(BF16)
    for c in range(_T_GA[1] - _T_GA[0]):
        cols = slice(c * TN_IN, (c + 1) * TN_IN)
        ga_ref[:, cols] = col(_T_GA[0] + c).astype(BF16)
        gc_ref[:, cols] = col(_T_GC[0] + c).astype(BF16)


def _inproj(x, g, cos, sin, w, tm, background=()):
    t = x.shape[0]
    n_pos = cos.shape[0] // tm
    row = lambda i: (i, 0)
    fixed = lambda i: (0, 0)
    pos = lambda i: (i % n_pos, 0)
    widths = (Q_W, KV_W, KV_W, D_CONV, D_CONV, D_MODEL, D_MODEL)
    dtypes = (BF16, F32, F32, F32, BF16, BF16, BF16)
    out_row_bytes = sum(wd * jnp.dtype(dt).itemsize for wd, dt in zip(widths, dtypes))
    nbytes = (2 * tm * (D_MODEL * 4 + 2 * LANES * 4 + out_row_bytes)
              + tm * D_MODEL * 2 + 6 * tm * TN_IN * 4)
    out_specs = [pl.BlockSpec((tm, wd), row) for wd in widths]
    out_shape = [jax.ShapeDtypeStruct((t, wd), dt) for wd, dt in zip(widths, dtypes)]
    scratch = [pltpu.VMEM((tm, D_MODEL), BF16)]
    staged = w.dtype == F32
    if staged:
        w_spec = pl.BlockSpec(memory_space=pl.ANY)
        out_specs.append(pl.BlockSpec(memory_space=pl.ANY))
        out_shape.append(jax.ShapeDtypeStruct(w.shape, BF16))
        scratch += _staging_scratch(w.shape, IN_STAGE_ROWS)
        nbytes += _staging_bytes(w.shape, IN_STAGE_ROWS)
    else:
        w_spec = pl.BlockSpec((D_MODEL, D_IN), fixed, pipeline_mode=pl.Buffered(1))
        nbytes += D_MODEL * D_IN * 2
    if background:
        out_specs += [pl.BlockSpec(memory_space=pl.ANY)] * len(background)
        out_shape += [jax.ShapeDtypeStruct(bw.shape, BF16) for bw in background]
        scratch += _background_scratch(BG_ROWS_2K, D_MODEL)
        nbytes += _background_bytes(BG_ROWS_2K, D_MODEL)
    return pl.pallas_call(
        functools.partial(_inproj_body, staged=staged, n_bg=len(background), n_steps=t // tm),
        grid=(t // tm,),
        in_specs=[
            pl.BlockSpec((tm, D_MODEL), row),
            pl.BlockSpec((1, D_MODEL), fixed),
            pl.BlockSpec((tm, LANES), pos),
            pl.BlockSpec((tm, LANES), pos),
            w_spec,
        ] + [pl.BlockSpec(memory_space=pl.ANY)] * len(background),
        out_specs=tuple(out_specs),
        out_shape=tuple(out_shape),
        scratch_shapes=scratch,
        compiler_params=_params(1, nbytes),
        name="inproj",
    )(x, g, cos, sin, w, *background)


def _head_pair_operands(x256, h):
    blk = x256[:, (h // 2) * LANES:(h // 2 + 1) * LANES]
    lo_lanes = lax.broadcasted_iota(jnp.int32, blk.shape, 1) < HEAD_DIM
    if h % 2 == 0:
        lo = jnp.where(lo_lanes, blk, 0.0)
        hi = pltpu.roll(lo, HEAD_DIM, axis=1)
    else:
        hi = jnp.where(lo_lanes, 0.0, blk)
        lo = pltpu.roll(hi, HEAD_DIM, axis=1)
    return jnp.concatenate([lo, hi], axis=0).astype(BF16)


def _attend(q_ref, o_ref, sink_ref, blocks, s_sc, m_sc, p_sc, v_sc):
    n_keys = blocks[0][1].shape[0]
    n_rows = s_sc.shape[1]
    key_row = lax.broadcasted_iota(jnp.int32, (2 * n_keys, LANES), 0)
    key_lane = lax.broadcasted_iota(jnp.int32, (2 * n_keys, LANES), 1)
    ones_cat = jnp.where((key_row >= n_keys) == (key_lane >= HEAD_DIM), 1.0, 0.0).astype(BF16)
    lo_lanes = lax.broadcasted_iota(jnp.int32, (n_rows, LANES), 1) < HEAD_DIM
    halves = [slice(hh * n_keys, (hh + 1) * n_keys) for hh in range(2)]

    tiles = []
    n_v = 0
    for q_rows, k_all, v_all, valid in blocks:
        for h in range(N_KV_HEADS):
            kcat = _head_pair_operands(k_all, h)
            v_sc[n_v] = jnp.concatenate([_head_pair_operands(v_all, h), ones_cat], axis=1)
            for p in range(2 * h, 2 * h + 2):
                lanes = slice(p * LANES, (p + 1) * LANES)
                sc = lax.dot_general(q_ref[q_rows, lanes], kcat, (((1,), (1,)), ((), ())),
                                     preferred_element_type=F32)
                s_sc[len(tiles)] = jnp.where(valid, sc, NEG)
                tiles.append((q_rows, lanes, p, n_v))
            n_v += 1
    for t, (_, _, p, _) in enumerate(tiles):
        for hh in range(2):
            m = jnp.maximum(jnp.max(s_sc[t, :, halves[hh]], axis=-1, keepdims=True), sink_ref[2 * p + hh])
            m_sc[2 * t + hh] = jnp.broadcast_to(m, (n_rows, LANES))
    for t in range(len(tiles)):
        p_sc[t] = jnp.concatenate(
            [jnp.exp(s_sc[t, :, halves[hh]] - jnp.tile(m_sc[2 * t + hh], (1, n_keys // LANES))).astype(BF16)
             for hh in range(2)], axis=1)
    for t, (q_rows, lanes, p, vi) in enumerate(tiles):
        e = p_sc[t]
        num_den = jnp.dot(e, v_sc[vi], preferred_element_type=F32)
        num = num_den[:, :LANES]
        den = num_den[:, LANES:]
        sink_term = jnp.where(lo_lanes, jnp.exp(sink_ref[2 * p] - m_sc[2 * t]),
                              jnp.exp(sink_ref[2 * p + 1] - m_sc[2 * t + 1]))
        o_ref[q_rows, lanes] = (num / (den + sink_term)).astype(BF16)


def _attn_scratch(n_blocks, n_rows, n_keys):
    n_tiles = n_blocks * N_HEADS // 2
    return [
        pltpu.VMEM((n_tiles, n_rows, 2 * n_keys), F32),
        pltpu.VMEM((2 * n_tiles, n_rows, LANES), F32),
        pltpu.VMEM((n_tiles, n_rows, 2 * n_keys), BF16),
        pltpu.VMEM((n_blocks * N_KV_HEADS, 2 * n_keys, 2 * LANES), BF16),
    ]


def _attn_scratch_bytes(n_blocks, n_rows, n_keys):
    n_tiles = n_blocks * N_HEADS // 2
    return (n_tiles * n_rows * 2 * n_keys * 6 + 2 * n_tiles * n_rows * LANES * 4
            + n_blocks * N_KV_HEADS * 2 * n_keys * 2 * LANES * 2)


def _attn_prompt_body(sink_ref, q_ref, kc_ref, kp_ref, vc_ref, vp_ref, o_ref, *scratch, tiles_per_seq):
    first = (pl.program_id(0) % tiles_per_seq) == 0
    n_sub = q_ref.shape[0] // ATT_BLK
    n_keys = 2 * ATT_BLK
    shape = (ATT_BLK, 2 * n_keys)
    q_chunk = lax.broadcasted_iota(jnp.int32, shape, 0) // CHUNK
    k_chunk = (lax.broadcasted_iota(jnp.int32, shape, 1) & (n_keys - 1)) // CHUNK
    band = (k_chunk >= q_chunk) & (k_chunk <= q_chunk + WINDOW // CHUNK)
    first_valid = band & (k_chunk >= jnp.where(first, ATT_BLK // CHUNK, 0))
    blocks = []
    for s in range(n_sub):
        rows = slice(s * ATT_BLK, (s + 1) * ATT_BLK)
        if s == 0:
            k_prev, v_prev, valid = kp_ref[...], vp_ref[...], first_valid
        else:
            prev = slice((s - 1) * ATT_BLK, s * ATT_BLK)
            k_prev, v_prev, valid = kc_ref[prev, :], vc_ref[prev, :], band
        k_all = jnp.concatenate([k_prev, kc_ref[rows, :]], axis=0)
        v_all = jnp.concatenate([v_prev, vc_ref[rows, :]], axis=0)
        blocks.append((rows, k_all, v_all, valid))
    _attend(q_ref, o_ref, sink_ref, blocks, *scratch)


def _attn_prompt(sinks, q, k, v, seq, tq):
    t = q.shape[0]
    ratio = tq // ATT_BLK
    cur = lambda i: (i, 0)
    prev = lambda i: (jnp.maximum(i * ratio - 1, 0), 0)
    nbytes = (2 * (2 * tq * Q_W * 2 + 2 * (tq + ATT_BLK) * KV_W * 4)
              + _attn_scratch_bytes(ratio, ATT_BLK, 2 * ATT_BLK))
    return pl.pallas_call(
        functools.partial(_attn_prompt_body, tiles_per_seq=seq // tq),
        grid=(t // tq,),
        in_specs=[
            pl.BlockSpec(memory_space=pltpu.SMEM),
            pl.BlockSpec((tq, Q_W), cur),
            pl.BlockSpec((tq, KV_W), cur),
            pl.BlockSpec((ATT_BLK, KV_W), prev),
            pl.BlockSpec((tq, KV_W), cur),
            pl.BlockSpec((ATT_BLK, KV_W), prev),
        ],
        out_specs=pl.BlockSpec((tq, Q_W), cur),
        out_shape=jax.ShapeDtypeStruct((t, Q_W), BF16),
        scratch_shapes=_attn_scratch(ratio, ATT_BLK, 2 * ATT_BLK),
        compiler_params=_params(1, nbytes),
        name="attn_prompt",
    )(sinks, q, k, k, v, v)


def _attn_sample_body(sink_ref, q_ref, k_ref, v_ref, o_ref, *scratch, n_valid, n_keys, rows_per_seq):
    shape = (rows_per_seq, 2 * n_keys)
    valid = (lax.broadcasted_iota(jnp.int32, shape, 1) & (n_keys - 1)) < n_valid
    blocks = []
    for b in range(q_ref.shape[0] // rows_per_seq):
        keys = slice(b * n_keys, (b + 1) * n_keys)
        blocks.append((slice(b * rows_per_seq, (b + 1) * rows_per_seq), k_ref[keys, :], v_ref[keys, :], valid))
    _attend(q_ref, o_ref, sink_ref, blocks, *scratch)


def _attn_sample(sinks, q, k_all, v_all, n_valid, n_keys, rows_per_seq, seqs_per_step):
    t = q.shape[0]
    tq = rows_per_seq * seqs_per_step
    tk = n_keys * seqs_per_step
    nbytes = 2 * (2 * tq * Q_W * 2 + 2 * tk * KV_W * 4) + _attn_scratch_bytes(seqs_per_step, rows_per_seq, n_keys)
    return pl.pallas_call(
        functools.partial(_attn_sample_body, n_valid=n_valid, n_keys=n_keys, rows_per_seq=rows_per_seq),
        grid=(t // tq,),
        in_specs=[
            pl.BlockSpec(memory_space=pltpu.SMEM),
            pl.BlockSpec((tq, Q_W), lambda i: (i, 0)),
            pl.BlockSpec((tk, KV_W), lambda i: (i, 0)),
            pl.BlockSpec((tk, KV_W), lambda i: (i, 0)),
        ],
        out_specs=pl.BlockSpec((tq, Q_W), lambda i: (i, 0)),
        out_shape=jax.ShapeDtypeStruct((t, Q_W), BF16),
        scratch_shapes=_attn_scratch(seqs_per_step, rows_per_seq, n_keys),
        compiler_params=_params(1, nbytes),
        name="attn_sample",
    )(sinks, q, k_all, v_all)


def _mix_body(*refs, tiles_per_seq, hist_period, n_bg, n_steps):
    refs = list(refs)
    x_ref, oa_ref, u_ref, b_ref, ga_ref, gc_ref = refs[:6]
    del refs[:6]
    if hist_period:
        p1_ref, p2_ref = refs[:2]
        del refs[:2]
    cw_ref, wao_ref, wco_ref, wmo_ref, g_ref = refs[:5]
    bg_src = refs[5:5 + n_bg]
    del refs[:5 + n_bg]
    h_ref = refs.pop(0)
    bg_dst = refs[:n_bg]
    del refs[:n_bg]
    if not hist_period:
        carry_sc = refs.pop(0)
    if n_bg:
        _background_cast(list(zip(bg_src, bg_dst)), *refs, pl.program_id(0), n_steps)
    u = u_ref[...]
    tm = u.shape[0]
    if hist_period:
        p1, p2 = _shifted_rows(u, p1_ref[...], p2_ref[...], p2_ref[...], hist_period)
    else:
        @pl.when(pl.program_id(0) % tiles_per_seq == 0)
        def _():
            carry_sc[...] = jnp.zeros_like(carry_sc)

        last1 = carry_sc[HALO - 1:HALO, :]
        last2 = carry_sc[HALO - 2:HALO - 1, :]
        p1, p2 = _shifted_rows(u, last1, last2, last1, tm)
        carry_sc[...] = u[tm - HALO:tm, :]
    cy = cw_ref[0:1, :] * p2 + cw_ref[1:2, :] * p1 + cw_ref[2:3, :] * u
    y_conv = (b_ref[...].astype(F32) * cy).astype(BF16)
    attn_o = jnp.dot(oa_ref[...], wao_ref[...], preferred_element_type=F32)
    conv_o = jnp.dot(y_conv, wco_ref[...], preferred_element_type=F32)
    mixed = (jax.nn.sigmoid(ga_ref[...].astype(F32)) * attn_o
             + jax.nn.sigmoid(gc_ref[...].astype(F32)) * conv_o)
    mo = jnp.dot(mixed.astype(BF16), wmo_ref[...], preferred_element_type=F32)
    h_ref[...] = x_ref[...] + _rms_norm(mo, g_ref[...])


def _mix(x, oa, u, b, ga, gc, hist, cw, wao, wco, wmo, g, seq, tm, background=()):
    t = x.shape[0]
    row = lambda i: (i, 0)
    fixed = lambda i: (0, 0)
    act_specs = [
        pl.BlockSpec((tm, D_MODEL), row),
        pl.BlockSpec((tm, Q_W), row),
        pl.BlockSpec((tm, D_CONV), row),
        pl.BlockSpec((tm, D_CONV), row),
        pl.BlockSpec((tm, D_MODEL), row),
        pl.BlockSpec((tm, D_MODEL), row),
    ]
    acts = [x, oa, u, b, ga, gc]
    scratch = []
    if hist is None:
        hist_period = 0
        scratch.append(pltpu.VMEM((HALO, D_CONV), F32))
    else:
        hist_period = seq
        act_specs += [pl.BlockSpec((tm, D_CONV), row)] * 2
        acts += list(hist)
    weights = (wao, wco, wmo)
    out_specs = [pl.BlockSpec((tm, D_MODEL), row)]
    out_shape = [jax.ShapeDtypeStruct((t, D_MODEL), F32)]
    row_bytes = D_MODEL * 4 * 2 + Q_W * 2 + D_CONV * (4 + 2) + D_MODEL * 2 * 2
    if hist is not None:
        row_bytes += 2 * D_CONV * 4
    nbytes = 2 * tm * row_bytes + (Q_W + D_CONV + D_MODEL) * D_MODEL * 2 + 6 * tm * D_MODEL * 4
    if background:
        cols = background[0].shape[1]
        out_specs += [pl.BlockSpec(memory_space=pl.ANY)] * len(background)
        out_shape += [jax.ShapeDtypeStruct(bw.shape, BF16) for bw in background]
        scratch += _background_scratch(BG_ROWS_UP, cols)
        nbytes += _background_bytes(BG_ROWS_UP, cols)
    in_specs = (act_specs + [pl.BlockSpec((CONV_W, D_CONV), fixed)]
                + [pl.BlockSpec(w.shape, fixed, pipeline_mode=pl.Buffered(1)) for w in weights]
                + [pl.BlockSpec((1, D_MODEL), fixed)] + [pl.BlockSpec(memory_space=pl.ANY)] * len(background))
    return pl.pallas_call(
        functools.partial(_mix_body, tiles_per_seq=max(seq // tm, 1), hist_period=hist_period,
                          n_bg=len(background), n_steps=t // tm),
        grid=(t // tm,),
        in_specs=in_specs,
        out_specs=tuple(out_specs),
        out_shape=tuple(out_shape),
        scratch_shapes=scratch,
        compiler_params=_params(1, nbytes),
        name="mix",
    )(*acts, cw, wao, wco, wmo, g, *background)


def _ffn_up_body(*refs, tiles_per_seq, hist_period, n_bg, n_steps):
    refs = list(refs)
    h_ref, g_ref, wa_ref, wg_ref, cw_ref = refs[:5]
    del refs[:5]
    if hist_period:
        p1_ref, p2_ref = refs[:2]
        del refs[:2]
    bg_src = refs[:n_bg]
    del refs[:n_bg]
    f_ref, aux_ref = refs[:2]
    bg_dst = refs[2:2 + n_bg]
    del refs[:2 + n_bg]
    xn_sc = refs.pop(0)
    if hist_period:
        a_ref = aux_ref
    else:
        tail_ref, carry_sc = aux_ref, refs.pop(0)
    i = pl.program_id(0)
    j = pl.program_id(1)
    if n_bg:
        _background_cast(list(zip(bg_src, bg_dst)), *refs, i * pl.num_programs(1) + j, n_steps)

    @pl.when(j == 0)
    def _():
        xn_sc[...] = _rms_norm(h_ref[...], g_ref[...]).astype(BF16)

    if not hist_period:
        @pl.when(i % tiles_per_seq == 0)
        def _():
            carry_sc[j] = jnp.zeros(carry_sc.shape[1:], F32)

        tail = carry_sc[j]
    tm = xn_sc.shape[0]
    rb = min(tm, ROW_BLK)
    for r in range(tm // rb):
        rows = slice(r * rb, (r + 1) * rb)
        xn = xn_sc[rows, :]
        a = jnp.dot(xn, wa_ref[...], preferred_element_type=F32)
        gate = jnp.dot(xn, wg_ref[...], preferred_element_type=F32)
        if hist_period:
            p1, p2 = _shifted_rows(a, p1_ref[rows, :], p2_ref[rows, :], p2_ref[rows, :], hist_period)
            a_ref[rows, :] = a
        else:
            last1 = tail[HALO - 1:HALO, :]
            p1, p2 = _shifted_rows(a, last1, tail[HALO - 2:HALO - 1, :], last1, rb)
            tail = a[rb - HALO:rb, :]
        ua = cw_ref[0:1, :] * p2 + cw_ref[1:2, :] * p1 + cw_ref[2:3, :] * a
        f_ref[rows, :] = (jax.nn.gelu(ua, approximate=True) * gate).astype(BF16)
    if not hist_period:
        carry_sc[j] = tail
        tail_ref[...] = tail


def _ffn_up(h, g, w_up, cw, hist, seq, tm, background=()):
    t = h.shape[0]
    n_j = D_FF // TN_FF
    grid = (t // tm, n_j)
    in_specs = [
        pl.BlockSpec((tm, D_MODEL), lambda i, j: (i, 0)),
        pl.BlockSpec((1, D_MODEL), lambda i, j: (0, 0)),
        pl.BlockSpec((D_MODEL, TN_FF), lambda i, j: (0, j)),
        pl.BlockSpec((D_MODEL, TN_FF), lambda i, j: (0, j + n_j)),
        pl.BlockSpec((CONV_W, TN_FF), lambda i, j: (0, j)),
    ]
    args = [h, g, w_up, w_up, cw]
    scratch = [pltpu.VMEM((tm, D_MODEL), BF16)]
    f_spec = pl.BlockSpec((tm, TN_FF), lambda i, j: (i, j))
    f_shape = jax.ShapeDtypeStruct((t, D_FF), BF16)
    if hist is None:
        hist_period = 0
        scratch.append(pltpu.VMEM((n_j, HALO, TN_FF), F32))
        out_specs = (f_spec, pl.BlockSpec((None, HALO, TN_FF), lambda i, j: (i, 0, j)))
        out_shape = (f_shape, jax.ShapeDtypeStruct((t // tm, HALO, D_FF), F32))
    else:
        hist_period = seq
        in_specs += [pl.BlockSpec((tm, TN_FF), lambda i, j: (i, j))] * 2
        args += list(hist)
        out_specs = (f_spec, pl.BlockSpec((tm, TN_FF), lambda i, j: (i, j)))
        out_shape = (f_shape, jax.ShapeDtypeStruct((t, D_FF), F32))
    nbytes = (2 * tm * D_MODEL * 4 + tm * D_MODEL * 2 + 4 * D_MODEL * TN_FF * 2
              + 2 * tm * TN_FF * (2 + 4 * 3) + 8 * tm * TN_FF * 4)
    if background:
        in_specs += [pl.BlockSpec(memory_space=pl.ANY)] * len(background)
        args += list(background)
        out_specs += (pl.BlockSpec(memory_space=pl.ANY),) * len(background)
        out_shape += tuple(jax.ShapeDtypeStruct(bw.shape, BF16) for bw in background)
        scratch += _background_scratch(BG_ROWS_2K, D_MODEL)
        nbytes += _background_bytes(BG_ROWS_2K, D_MODEL)
    return pl.pallas_call(
        functools.partial(_ffn_up_body, tiles_per_seq=max(seq // tm, 1), hist_period=hist_period,
                          n_bg=len(background), n_steps=grid[0] * grid[1]),
        grid=grid,
        in_specs=in_specs,
        out_specs=out_specs,
        out_shape=out_shape,
        scratch_shapes=scratch,
        compiler_params=_params(2, nbytes),
        name="ffn_up",
    )(*args)


def _ffn_down_body(f_ref, w_ref, h_ref, g_ref, o_ref):
    tm = f_ref.shape[0]
    rb = min(tm, ROW_BLK)
    for r in range(tm // rb):
        rows = slice(r * rb, (r + 1) * rb)
        acc = jnp.dot(f_ref[rows, :], w_ref[...], preferred_element_type=F32)
        o_ref[rows, :] = h_ref[rows, :] + _rms_norm(acc, g_ref[...])


def _ffn_down(f, w, h, g, tm):
    t = f.shape[0]
    nbytes = (2 * tm * (D_FF * 2 + D_MODEL * 4 * 2) + D_FF * D_MODEL * 2
              + 2 * min(tm, ROW_BLK) * D_MODEL * 4)
    return pl.pallas_call(
        _ffn_down_body,
        grid=(t // tm,),
        in_specs=[
            pl.BlockSpec((tm, D_FF), lambda i: (i, 0)),
            pl.BlockSpec((D_FF, D_MODEL), lambda i: (0, 0), pipeline_mode=pl.Buffered(1)),
            pl.BlockSpec((tm, D_MODEL), lambda i: (i, 0)),
            pl.BlockSpec((1, D_MODEL), lambda i: (0, 0)),
        ],
        out_specs=pl.BlockSpec((tm, D_MODEL), lambda i: (i, 0)),
        out_shape=jax.ShapeDtypeStruct((t, D_MODEL), F32),
        compiler_params=_params(1, nbytes),
        name="ffn_down",
    )(f, w, h, g)


def _rope_tables(pos):
    half = ROT_DIM // 2
    inv = jnp.power(jnp.float32(ROPE_THETA), -jnp.arange(half, dtype=F32) * (2.0 / ROT_DIM))
    ang = pos.astype(F32)[:, None] * inv[None, :]
    cos = jnp.cos(ang)
    sin = jnp.sin(ang)
    rest = HEAD_DIM - ROT_DIM
    cos_h = jnp.concatenate([cos, cos, jnp.ones((pos.shape[0], rest), F32)], axis=1)
    sin_h = jnp.concatenate([-sin, sin, jnp.zeros((pos.shape[0], rest), F32)], axis=1)
    reps = LANES // HEAD_DIM
    return jnp.tile(cos_h, (1, reps)), jnp.tile(sin_h, (1, reps))


def _row_history(state, seq):
    b, _, c = state.shape
    p1 = jnp.concatenate([state[:, 1:2], jnp.zeros((b, seq - 1, c), F32)], axis=1)
    p2 = jnp.concatenate([state[:, 0:1], state[:, 1:2], jnp.zeros((b, seq - 2, c), F32)], axis=1)
    return p1.reshape(b * seq, c), p2.reshape(b * seq, c)


def kernel(x_prompt, x_sample, cache_k, cache_v, state_conv_mix, state_conv_ffn, norm_mix_pre, w_in,
           attn_sinks, conv_mix_w, w_attn_o, w_conv_o, w_mix_out, norm_mix_post, norm_ffn_pre,
           w_ffn_up, conv_ffn_w, w_ffn_down, norm_ffn_post):
    assert w_in.shape == (1, D_MODEL, D_IN), "one layer only"
    batch, seq, _ = x_prompt.shape
    dec_batch, dec_seq, _ = x_sample.shape
    assert seq % 1024 == 0 and (dec_batch * dec_seq) % 256 == 0 and dec_seq % 16 == 0
    assert cache_k.shape[2] == WINDOW

    g_pre = norm_mix_pre[0][None]
    g_post = norm_mix_post[0][None]
    g_ffn_pre = norm_ffn_pre[0][None]
    g_ffn_post = norm_ffn_post[0][None]
    sinks = attn_sinks[0]
    cw_mix = conv_mix_w[0]
    cw_ffn = conv_ffn_w[0]

    tp = batch * seq
    xp = x_prompt.reshape(tp, D_MODEL)
    cos_p, sin_p = _rope_tables(jnp.arange(seq, dtype=F32))
    q, k, v, u, b, ga, gc, w_in_b, wao, wco, wmo = _inproj(
        xp, g_pre, cos_p, sin_p, w_in[0], tm=256, background=(w_attn_o[0], w_conv_o[0], w_mix_out[0]))
    oa = _attn_prompt(sinks, q, k, v, seq, tq=256)
    hp, w_up = _mix(xp, oa, u, b, ga, gc, None, cw_mix, wao, wco, wmo, g_post, seq, tm=256,
                    background=(w_ffn_up[0],))
    f, a_tail, w_down = _ffn_up(hp, g_ffn_pre, w_up, cw_ffn, None, seq, tm=FF_UP_TM,
                                background=(w_ffn_down[0],))
    yp = _ffn_down(f, w_down, hp, g_ffn_post, tm=512)

    keep = min(WINDOW, seq)
    new_k_prompt = k.reshape(batch, seq, KV_W)[:, seq - keep:].reshape(1, batch, keep, N_KV_HEADS, HEAD_DIM)
    new_v_prompt = v.reshape(batch, seq, KV_W)[:, seq - keep:].reshape(1, batch, keep, N_KV_HEADS, HEAD_DIM)
    conv_mix_prompt = u.reshape(batch, seq, D_CONV)[:, seq - (CONV_W - 1):][None]
    a_tail = a_tail.reshape(batch, seq // FF_UP_TM, HALO, D_FF)
    conv_ffn_prompt = a_tail[:, -1, HALO - (CONV_W - 1):][None]

    ts = dec_batch * dec_seq
    xs = x_sample.reshape(ts, D_MODEL)
    pos_s = PAST_LEN + jnp.arange(dec_seq, dtype=F32)
    cos_s, sin_s = _rope_tables(jnp.tile(pos_s, dec_batch))
    qs, ks, vs, us, bs, gas, gcs = _inproj(xs, g_pre, cos_s, sin_s, w_in_b, tm=ts)
    n_valid = WINDOW + dec_seq
    n_keys = 2 * WINDOW
    pad = jnp.zeros((dec_batch, n_keys - n_valid, KV_W), F32)
    k_all = jnp.concatenate([cache_k[0].reshape(dec_batch, WINDOW, KV_W),
                             ks.reshape(dec_batch, dec_seq, KV_W), pad], axis=1)
    v_all = jnp.concatenate([cache_v[0].reshape(dec_batch, WINDOW, KV_W),
                             vs.reshape(dec_batch, dec_seq, KV_W), pad], axis=1)
    oas = _attn_sample(sinks, qs, k_all.reshape(dec_batch * n_keys, KV_W),
                       v_all.reshape(dec_batch * n_keys, KV_W), n_valid, n_keys, dec_seq, seqs_per_step=4)
    (hs,) = _mix(xs, oas, us, bs, gas, gcs, _row_history(state_conv_mix[0], dec_seq),
                 cw_mix, wao, wco, wmo, g_post, dec_seq, tm=ts)
    fs, a_s = _ffn_up(hs, g_ffn_pre, w_up, cw_ffn, _row_history(state_conv_ffn[0], dec_seq), dec_seq, tm=ts)
    ys = _ffn_down(fs, w_down, hs, g_ffn_post, tm=ts)

    new_k_sample = ks.reshape(1, dec_batch, dec_seq, N_KV_HEADS, HEAD_DIM)
    new_v_sample = vs.reshape(1, dec_batch, dec_seq, N_KV_HEADS, HEAD_DIM)
    conv_mix_sample = us.reshape(dec_batch, dec_seq, D_CONV)[:, dec_seq - (CONV_W - 1):][None]
    conv_ffn_sample = a_s.reshape(dec_batch, dec_seq, D_FF)[:, dec_seq - (CONV_W - 1):][None]

    return (yp.reshape(batch, seq, D_MODEL), ys.reshape(dec_batch, dec_seq, D_MODEL),
            new_k_prompt, new_v_prompt, conv_mix_prompt, conv_ffn_prompt,
            new_k_sample, new_v_sample, conv_mix_sample, conv_ffn_sample)
```

```python
import functools

import jax
import jax.numpy as jnp
from jax import lax
from jax.experimental import pallas as pl
from jax.experimental.pallas import tpu as pltpu

F32 = jnp.float32
BF16 = jnp.bfloat16

D_MODEL = 2048
CHUNK = 64
N_HEADS = 16
N_KV_HEADS = 4
GROUP = N_HEADS // N_KV_HEADS
HEAD_DIM = 64
ROT_DIM = HEAD_DIM // 4
ROPE_THETA = 500000.0
WINDOW = 128
D_CONV = 1024
CONV_W = 3
D_FF = 5632
EPS = 1e-6
NEG = -1e30
PAST_LEN = 4096
Q_W = N_HEADS * HEAD_DIM
KV_W = N_KV_HEADS * HEAD_DIM
D_IN = Q_W + 2 * KV_W + 3 * D_CONV + 2 * D_MODEL

LANES = 128
SUBLANES = 8
VMEM_LIMIT_CAP = 60 * 1024 * 1024
VMEM_COMPILER_SLACK = 6 * 1024 * 1024

TN_IN = 512
TN_FF = 512
FF_UP_TM = 1024
IN_STAGE_ROWS = 64
BG_ROWS_2K = 128
BG_ROWS_UP = 64
ATT_BLK = 2 * CHUNK
HALO = SUBLANES
ROW_BLK = 256

_T_Q = (0, 2)
_T_KV = (2, 3)
_T_XIN = (3, 5)
_T_B = (5, 7)
_T_C = (7, 9)
_T_GA = (9, 13)
_T_GC = (13, 17)


def _vmem_limit(nbytes):
    return int(min(nbytes + VMEM_COMPILER_SLACK, VMEM_LIMIT_CAP))


def _params(n_axes, nbytes):
    return pltpu.CompilerParams(
        dimension_semantics=("arbitrary",) * n_axes,
        vmem_limit_bytes=_vmem_limit(nbytes),
    )


def _rms_norm(x, g):
    return x * lax.rsqrt(jnp.mean(x * x, axis=-1, keepdims=True) + EPS) * g


def _rope128(x, cos, sin):
    dh = lax.broadcasted_iota(jnp.int32, x.shape, 1) & (HEAD_DIM - 1)
    half = ROT_DIM // 2
    partner = jnp.where(dh < half, pltpu.roll(x, LANES - half, axis=1), pltpu.roll(x, half, axis=1))
    return x * cos + partner * sin


def _shifted_rows(a, prev1_row0, prev2_row0, prev2_row1, period):
    assert period & (period - 1) == 0
    rows = lax.broadcasted_iota(jnp.int32, (a.shape[0], 1), 0) & (period - 1)
    p1 = jnp.where(rows == 0, prev1_row0, pltpu.roll(a, 1, axis=0))
    p2 = jnp.where(rows == 0, prev2_row0, jnp.where(rows == 1, prev2_row1, pltpu.roll(a, 2, axis=0)))
    return p1, p2


def _stage_weight(w_hbm, w_sc, w_out, stage, in_sems, out_sem, step, n_steps):
    chunk = stage.shape[1]
    n_chunks = w_hbm.shape[0] // chunk

    def fetch(c, slot):
        return pltpu.make_async_copy(w_hbm.at[pl.ds(c * chunk, chunk), :], stage.at[slot], in_sems.at[slot])

    export = pltpu.make_async_copy(w_sc, w_out, out_sem)

    @pl.when(step == 0)
    def _():
        fetch(0, 0).start()

        def body(c, carry):
            slot = lax.rem(c, 2)

            @pl.when(c + 1 < n_chunks)
            def _():
                fetch(c + 1, 1 - slot).start()

            fetch(c, slot).wait()
            w_sc[pl.ds(pl.multiple_of(c * chunk, chunk), chunk), :] = stage[slot].astype(BF16)
            return carry

        lax.fori_loop(0, n_chunks, body, 0)
        export.start()

    @pl.when(step == n_steps - 1)
    def _():
        export.wait()


def _background_cast(jobs, buf_in, buf_out, sems, step, n_steps):
    chunk = buf_in.shape[1]
    slot_now = lax.rem(step, 2)
    slot_prev = 1 - slot_now
    first = 0
    for src, dst in jobs:
        n = src.shape[0] // chunk
        assert src.shape[0] % chunk == 0 and first + n + 2 <= n_steps

        def fetch(c, slot, src=src):
            return pltpu.make_async_copy(src.at[pl.ds(c * chunk, chunk), :], buf_in.at[slot], sems.at[0, slot])

        def send(c, slot, dst=dst):
            return pltpu.make_async_copy(buf_out.at[slot], dst.at[pl.ds(c * chunk, chunk), :], sems.at[1, slot])

        c_fetch = step - first
        c_cast = c_fetch - 1
        c_done = c_fetch - 2

        @pl.when((c_done >= 0) & (c_done < n))
        def _():
            send(c_done, slot_now).wait()

        @pl.when((c_cast >= 0) & (c_cast < n))
        def _():
            fetch(c_cast, slot_prev).wait()
            buf_out[slot_prev] = buf_in[slot_prev].astype(BF16)
            send(c_cast, slot_prev).start()

        @pl.when((c_fetch >= 0) & (c_fetch < n))
        def _():
            fetch(c_fetch, slot_now).start()

        first += n


def _background_scratch(chunk, cols):
    return [
        pltpu.VMEM((2, chunk, cols), F32),
        pltpu.VMEM((2, chunk, cols), BF16),
        pltpu.SemaphoreType.DMA((2, 2)),
    ]


def _background_bytes(chunk, cols):
    return 2 * chunk * cols * (4 + 2)


def _staging_scratch(w_shape, chunk):
    assert w_shape[0] % chunk == 0
    return [
        pltpu.VMEM(w_shape, BF16),
        pltpu.VMEM((2, chunk, w_shape[1]), F32),
        pltpu.SemaphoreType.DMA((2,)),
        pltpu.SemaphoreType.DMA(()),
    ]


def _staging_bytes(w_shape, chunk):
    return w_shape[0] * w_shape[1] * 2 + 2 * chunk * w_shape[1] * 4


def _inproj_body(*refs, staged, n_bg, n_steps):
    refs = list(refs)
    x_ref, g_ref, cos_ref, sin_ref, w_ref = refs[:5]
    bg_src = refs[5:5 + n_bg]
    del refs[:5 + n_bg]
    q_ref, k_ref, v_ref, u_ref, b_ref, ga_ref, gc_ref = refs[:7]
    del refs[:7]
    if staged:
        w_hbm, w_out = w_ref, refs.pop(0)
    bg_dst = refs[:n_bg]
    del refs[:n_bg]
    xn_sc = refs.pop(0)
    step = pl.program_id(0)
    if staged:
        w_ref, stage, in_sems, out_sem = refs[:4]
        del refs[:4]
        _stage_weight(w_hbm, w_ref, w_out, stage, in_sems, out_sem, step, n_steps)
    if n_bg:
        _background_cast(list(zip(bg_src, bg_dst)), *refs, step, n_steps)
    xn_sc[...] = _rms_norm(x_ref[...], g_ref[...]).astype(BF16)
    cos = cos_ref[...]
    sin = sin_ref[...]

    def col(t):
        return jnp.dot(xn_sc[...], w_ref[:, t * TN_IN:(t + 1) * TN_IN], preferred_element_type=F32)

    def blocks(n):
        return [slice(blk * LANES, (blk + 1) * LANES) for blk in range(n)]

    for t in range(*_T_Q):
        acc = col(t)
        base = (t - _T_Q[0]) * TN_IN
        for sl in blocks(TN_IN // LANES):
            q_ref[:, base + sl.start:base + sl.stop] = (
                _rope128(acc[:, sl], cos, sin) * (HEAD_DIM ** -0.5)).astype(BF16)
    acc = col(_T_KV[0])
    for sl in blocks(KV_W // LANES):
        k_ref[:, sl] = _rope128(acc[:, sl], cos, sin)
    v_ref[...] = acc[:, KV_W:2 * KV_W]
    for c in range(_T_C[1] - _T_C[0]):
        cols = slice(c * TN_IN, (c + 1) * TN_IN)
        u_ref[:, cols] = col(_T_C[0] + c) * col(_T_XIN[0] + c)
        b_ref[:, cols] = col(_T_B[0] + c).astype(BF16)
    for c in range(_T_GA[1] - _T_GA[0]):
        cols = slice(c * TN_IN, (c + 1) * TN_IN)
        ga_ref[:, cols] = col(_T_GA[0] + c).astype(BF16)
        gc_ref[:, cols] = col(_T_GC[0] + c).astype(BF16)


def _inproj(x, g, cos, sin, w, tm, background=()):
    t = x.shape[0]
    n_pos = cos.shape[0] // tm
    row = lambda i: (i, 0)
    fixed = lambda i: (0, 0)
    pos = lambda i: (i % n_pos, 0)
    widths = (Q_W, KV_W, KV_W, D_CONV, D_CONV, D_MODEL, D_MODEL)
    dtypes = (BF16, F32, F32, F32, BF16, BF16, BF16)
    out_row_bytes = sum(wd * jnp.dtype(dt).itemsize for wd, dt in zip(widths, dtypes))
    nbytes = (2 * tm * (D_MODEL * 4 + 2 * LANES * 4 + out_row_bytes)
              + tm * D_MODEL * 2 + 6 * tm * TN_IN * 4)
    out_specs = [pl.BlockSpec((tm, wd), row) for wd in widths]
    out_shape = [jax.ShapeDtypeStruct((t, wd), dt) for wd, dt in zip(widths, dtypes)]
    scratch = [pltpu.VMEM((tm, D_MODEL), BF16)]
    staged = w.dtype == F32
    if staged:
        w_spec = pl.BlockSpec(memory_space=pl.ANY)
        out_specs.append(pl.BlockSpec(memory_space=pl.ANY))
        out_shape.append(jax.ShapeDtypeStruct(w.shape, BF16))
        scratch += _staging_scratch(w.shape, IN_STAGE_ROWS)
        nbytes += _staging_bytes(w.shape, IN_STAGE_ROWS)
    else:
        w_spec = pl.BlockSpec((D_MODEL, D_IN), fixed, pipeline_mode=pl.Buffered(1))
        nbytes += D_MODEL * D_IN * 2
    if background:
        out_specs += [pl.BlockSpec(memory_space=pl.ANY)] * len(background)
        out_shape += [jax.ShapeDtypeStruct(bw.shape, BF16) for bw in background]
        scratch += _background_scratch(BG_ROWS_2K, D_MODEL)
        nbytes += _background_bytes(BG_ROWS_2K, D_MODEL)
    return pl.pallas_call(
        functools.partial(_inproj_body, staged=staged, n_bg=len(background), n_steps=t // tm),
        grid=(t // tm,),
        in_specs=[
            pl.BlockSpec((tm, D_MODEL), row),
            pl.BlockSpec((1, D_MODEL), fixed),
            pl.BlockSpec((tm, LANES), pos),
            pl.BlockSpec((tm, LANES), pos),
            w_spec,
        ] + [pl.BlockSpec(memory_space=pl.ANY)] * len(background),
        out_specs=tuple(out_specs),
        out_shape=tuple(out_shape),
        scratch_shapes=scratch,
        compiler_params=_params(1, nbytes),
        name="inproj",
    )(x, g, cos, sin, w, *background)


def _head_pair_operands(x256, h):
    blk = x256[:, (h // 2) * LANES:(h // 2 + 1) * LANES]
    lo_lanes = lax.broadcasted_iota(jnp.int32, blk.shape, 1) < HEAD_DIM
    if h % 2 == 0:
        lo = jnp.where(lo_lanes, blk, 0.0)
        hi = pltpu.roll(lo, HEAD_DIM, axis=1)
    else:
        hi = jnp.where(lo_lanes, 0.0, blk)
        lo = pltpu.roll(hi, HEAD_DIM, axis=1)
    return jnp.concatenate([lo, hi], axis=0).astype(BF16)


def _attend(q_ref, o_ref, sink_ref, blocks, s_sc, m_sc, p_sc, v_sc, after_scores=None):
    n_keys = blocks[0][1].shape[0]
    n_rows = s_sc.shape[1]
    key_row = lax.broadcasted_iota(jnp.int32, (2 * n_keys, LANES), 0)
    key_lane = lax.broadcasted_iota(jnp.int32, (2 * n_keys, LANES), 1)
    ones_cat = jnp.where((key_row >= n_keys) == (key_lane >= HEAD_DIM), 1.0, 0.0).astype(BF16)
    lo_lanes = lax.broadcasted_iota(jnp.int32, (n_rows, LANES), 1) < HEAD_DIM
    halves = [slice(hh * n_keys, (hh + 1) * n_keys) for hh in range(2)]

    tiles = []
    n_v = 0
    for q_rows, k_all, v_all, valid in blocks:
        for h in range(N_KV_HEADS):
            kcat = _head_pair_operands(k_all, h)
            v_sc[n_v] = jnp.concatenate([_head_pair_operands(v_all, h), ones_cat], axis=1)
            for p in range(2 * h, 2 * h + 2):
                lanes = slice(p * LANES, (p + 1) * LANES)
                sc = lax.dot_general(q_ref[q_rows, lanes], kcat, (((1,), (1,)), ((), ())),
                                     preferred_element_type=F32)
                s_sc[len(tiles)] = jnp.where(valid, sc, NEG)
                tiles.append((q_rows, lanes, p, n_v))
            n_v += 1
    if after_scores is not None:
        after_scores()
    for t, (_, _, p, _) in enumerate(tiles):
        for hh in range(2):
            m = jnp.maximum(jnp.max(s_sc[t, :, halves[hh]], axis=-1, keepdims=True), sink_ref[2 * p + hh])
            m_sc[2 * t + hh] = jnp.broadcast_to(m, (n_rows, LANES))
    for t in range(len(tiles)):
        p_sc[t] = jnp.concatenate(
            [jnp.exp(s_sc[t, :, halves[hh]] - jnp.tile(m_sc[2 * t + hh], (1, n_keys // LANES))).astype(BF16)
             for hh in range(2)], axis=1)
    for t, (q_rows, lanes, p, vi) in enumerate(tiles):
        e = p_sc[t]
        num_den = jnp.dot(e, v_sc[vi], preferred_element_type=F32)
        num = num_den[:, :LANES]
        den = num_den[:, LANES:]
        sink_term = jnp.where(lo_lanes, jnp.exp(sink_ref[2 * p] - m_sc[2 * t]),
                              jnp.exp(sink_ref[2 * p + 1] - m_sc[2 * t + 1]))
        o_ref[q_rows, lanes] = (num / (den + sink_term)).astype(BF16)


def _attn_scratch(n_blocks, n_rows, n_keys):
    n_tiles = n_blocks * N_HEADS // 2
    return [
        pltpu.VMEM((n_tiles, n_rows, 2 * n_keys), F32),
        pltpu.VMEM((2 * n_tiles, n_rows, LANES), F32),
        pltpu.VMEM((n_tiles, n_rows, 2 * n_keys), BF16),
        pltpu.VMEM((n_blocks * N_KV_HEADS, 2 * n_keys, 2 * LANES), BF16),
    ]


def _attn_scratch_bytes(n_blocks, n_rows, n_keys):
    n_tiles = n_blocks * N_HEADS // 2
    return (n_tiles * n_rows * 2 * n_keys * 6 + 2 * n_tiles * n_rows * LANES * 4
            + n_blocks * N_KV_HEADS * 2 * n_keys * 2 * LANES * 2)


def _prompt_attention(first, sink_ref, q_ref, kc_ref, kp_ref, vc_ref, vp_ref, o_ref, scratch, after_scores=None):
    n_sub = q_ref.shape[0] // ATT_BLK
    n_keys = 2 * ATT_BLK
    shape = (ATT_BLK, 2 * n_keys)
    q_chunk = lax.broadcasted_iota(jnp.int32, shape, 0) // CHUNK
    k_chunk = (lax.broadcasted_iota(jnp.int32, shape, 1) & (n_keys - 1)) // CHUNK
    band = (k_chunk >= q_chunk) & (k_chunk <= q_chunk + WINDOW // CHUNK)
    first_valid = band & (k_chunk >= jnp.where(first, ATT_BLK // CHUNK, 0))
    blocks = []
    for s in range(n_sub):
        rows = slice(s * ATT_BLK, (s + 1) * ATT_BLK)
        if s == 0:
            k_prev, v_prev, valid = kp_ref[...], vp_ref[...], first_valid
        else:
            prev = slice((s - 1) * ATT_BLK, s * ATT_BLK)
            k_prev, v_prev, valid = kc_ref[prev, :], vc_ref[prev, :], band
        k_all = jnp.concatenate([k_prev, kc_ref[rows, :]], axis=0)
        v_all = jnp.concatenate([v_prev, vc_ref[rows, :]], axis=0)
        blocks.append((rows, k_all, v_all, valid))
    _attend(q_ref, o_ref, sink_ref, blocks, *scratch, after_scores=after_scores)


def _attn_sample_body(sink_ref, q_ref, k_ref, v_ref, o_ref, *scratch, n_valid, n_keys, rows_per_seq):
    shape = (rows_per_seq, 2 * n_keys)
    valid = (lax.broadcasted_iota(jnp.int32, shape, 1) & (n_keys - 1)) < n_valid
    blocks = []
    for b in range(q_ref.shape[0] // rows_per_seq):
        keys = slice(b * n_keys, (b + 1) * n_keys)
        blocks.append((slice(b * rows_per_seq, (b + 1) * rows_per_seq), k_ref[keys, :], v_ref[keys, :], valid))
    _attend(q_ref, o_ref, sink_ref, blocks, *scratch)


def _attn_sample(sinks, q, k_all, v_all, n_valid, n_keys, rows_per_seq, seqs_per_step):
    t = q.shape[0]
    tq = rows_per_seq * seqs_per_step
    tk = n_keys * seqs_per_step
    nbytes = 2 * (2 * tq * Q_W * 2 + 2 * tk * KV_W * 4) + _attn_scratch_bytes(seqs_per_step, rows_per_seq, n_keys)
    return pl.pallas_call(
        functools.partial(_attn_sample_body, n_valid=n_valid, n_keys=n_keys, rows_per_seq=rows_per_seq),
        grid=(t // tq,),
        in_specs=[
            pl.BlockSpec(memory_space=pltpu.SMEM),
            pl.BlockSpec((tq, Q_W), lambda i: (i, 0)),
            pl.BlockSpec((tk, KV_W), lambda i: (i, 0)),
            pl.BlockSpec((tk, KV_W), lambda i: (i, 0)),
        ],
        out_specs=pl.BlockSpec((tq, Q_W), lambda i: (i, 0)),
        out_shape=jax.ShapeDtypeStruct((t, Q_W), BF16),
        scratch_shapes=_attn_scratch(seqs_per_step, rows_per_seq, n_keys),
        compiler_params=_params(1, nbytes),
        name="attn_sample",
    )(sinks, q, k_all, v_all)


def _mix_body(*refs, tiles_per_seq, hist_period, fused_attn, n_bg, n_steps):
    refs = list(refs)
    if fused_attn:
        sink_ref, x_ref, q_ref, kc_ref, kp_ref, vc_ref, vp_ref = refs[:7]
        del refs[:7]
    else:
        x_ref, oa_ref = refs[:2]
        del refs[:2]
    u_ref, b_ref, ga_ref, gc_ref = refs[:4]
    del refs[:4]
    if hist_period:
        p1_ref, p2_ref = refs[:2]
        del refs[:2]
    cw_ref, wao_ref, wco_ref, wmo_ref, g_ref = refs[:5]
    bg_src = refs[5:5 + n_bg]
    del refs[:5 + n_bg]
    h_ref = refs.pop(0)
    bg_dst = refs[:n_bg]
    del refs[:n_bg]
    if not hist_period:
        carry_sc = refs.pop(0)
    if fused_attn:
        oa_ref = refs.pop(0)
        attn_scratch = refs[:4]
        del refs[:4]
    if n_bg:
        _background_cast(list(zip(bg_src, bg_dst)), *refs, pl.program_id(0), n_steps)
    u = u_ref[...]
    tm = u.shape[0]
    if hist_period:
        p1, p2 = _shifted_rows(u, p1_ref[...], p2_ref[...], p2_ref[...], hist_period)
    else:
        @pl.when(pl.program_id(0) % tiles_per_seq == 0)
        def _():
            carry_sc[...] = jnp.zeros_like(carry_sc)

        last1 = carry_sc[HALO - 1:HALO, :]
        last2 = carry_sc[HALO - 2:HALO - 1, :]
        p1, p2 = _shifted_rows(u, last1, last2, last1, tm)
        carry_sc[...] = u[tm - HALO:tm, :]
    cy = cw_ref[0:1, :] * p2 + cw_ref[1:2, :] * p1 + cw_ref[2:3, :] * u
    y_conv = (b_ref[...].astype(F32) * cy).astype(BF16)
    if fused_attn:
        conv_branch = []
        first = (pl.program_id(0) % tiles_per_seq) == 0
        _prompt_attention(
            first, sink_ref, q_ref, kc_ref, kp_ref, vc_ref, vp_ref, oa_ref, attn_scratch,
            after_scores=lambda: conv_branch.append(jnp.dot(y_conv, wco_ref[...], preferred_element_type=F32)))
        conv_o = conv_branch[0]
    else:
        conv_o = jnp.dot(y_conv, wco_ref[...], preferred_element_type=F32)
    attn_o = jnp.dot(oa_ref[...], wao_ref[...], preferred_element_type=F32)
    mixed = (jax.nn.sigmoid(ga_ref[...].astype(F32)) * attn_o
             + jax.nn.sigmoid(gc_ref[...].astype(F32)) * conv_o)
    mo = jnp.dot(mixed.astype(BF16), wmo_ref[...], preferred_element_type=F32)
    h_ref[...] = x_ref[...] + _rms_norm(mo, g_ref[...])


def _mix(x, oa, u, b, ga, gc, hist, cw, wao, wco, wmo, g, seq, tm, background=(), attn=None):
    t = x.shape[0]
    row = lambda i: (i, 0)
    fixed = lambda i: (0, 0)
    scratch = []
    if attn is None:
        act_specs = [pl.BlockSpec((tm, D_MODEL), row), pl.BlockSpec((tm, Q_W), row)]
        acts = [x, oa]
    else:
        sinks, q, k, v = attn
        ratio = tm // ATT_BLK
        prev = lambda i: (jnp.maximum(i * ratio - 1, 0), 0)
        act_specs = [
            pl.BlockSpec(memory_space=pltpu.SMEM),
            pl.BlockSpec((tm, D_MODEL), row),
            pl.BlockSpec((tm, Q_W), row),
            pl.BlockSpec((tm, KV_W), row),
            pl.BlockSpec((ATT_BLK, KV_W), prev),
            pl.BlockSpec((tm, KV_W), row),
            pl.BlockSpec((ATT_BLK, KV_W), prev),
        ]
        acts = [sinks, x, q, k, k, v, v]
    act_specs += [
        pl.BlockSpec((tm, D_CONV), row),
        pl.BlockSpec((tm, D_CONV), row),
        pl.BlockSpec((tm, D_MODEL), row),
        pl.BlockSpec((tm, D_MODEL), row),
    ]
    acts += [u, b, ga, gc]
    if hist is None:
        hist_period = 0
        scratch.append(pltpu.VMEM((HALO, D_CONV), F32))
    else:
        hist_period = seq
        act_specs += [pl.BlockSpec((tm, D_CONV), row)] * 2
        acts += list(hist)
    attn_bytes = 0
    if attn is not None:
        scratch.append(pltpu.VMEM((tm, Q_W), BF16))
        scratch += _attn_scratch(tm // ATT_BLK, ATT_BLK, 2 * ATT_BLK)
        attn_bytes = (tm * Q_W * 2 + 4 * (tm + ATT_BLK) * KV_W * 4
                      + _attn_scratch_bytes(tm // ATT_BLK, ATT_BLK, 2 * ATT_BLK))
    weights = (wao, wco, wmo)
    out_specs = [pl.BlockSpec((tm, D_MODEL), row)]
    out_shape = [jax.ShapeDtypeStruct((t, D_MODEL), F32)]
    row_bytes = D_MODEL * 4 * 2 + Q_W * 2 + D_CONV * (4 + 2) + D_MODEL * 2 * 2
    if hist is not None:
        row_bytes += 2 * D_CONV * 4
    nbytes = (2 * tm * row_bytes + (Q_W + D_CONV + D_MODEL) * D_MODEL * 2 + 6 * tm * D_MODEL * 4
              + attn_bytes)
    if background:
        cols = background[0].shape[1]
        out_specs += [pl.BlockSpec(memory_space=pl.ANY)] * len(background)
        out_shape += [jax.ShapeDtypeStruct(bw.shape, BF16) for bw in background]
        scratch += _background_scratch(BG_ROWS_UP, cols)
        nbytes += _background_bytes(BG_ROWS_UP, cols)
    in_specs = (act_specs + [pl.BlockSpec((CONV_W, D_CONV), fixed)]
                + [pl.BlockSpec(w.shape, fixed, pipeline_mode=pl.Buffered(1)) for w in weights]
                + [pl.BlockSpec((1, D_MODEL), fixed)] + [pl.BlockSpec(memory_space=pl.ANY)] * len(background))
    return pl.pallas_call(
        functools.partial(_mix_body, tiles_per_seq=max(seq // tm, 1), hist_period=hist_period,
                          fused_attn=attn is not None, n_bg=len(background), n_steps=t // tm),
        grid=(t // tm,),
        in_specs=in_specs,
        out_specs=tuple(out_specs),
        out_shape=tuple(out_shape),
        scratch_shapes=scratch,
        compiler_params=_params(1, nbytes),
        name="mix",
    )(*acts, cw, wao, wco, wmo, g, *background)


def _ffn_up_body(*refs, tiles_per_seq, hist_period, n_bg, n_steps):
    refs = list(refs)
    h_ref, g_ref, wa_ref, wg_ref, cw_ref = refs[:5]
    del refs[:5]
    if hist_period:
        p1_ref, p2_ref = refs[:2]
        del refs[:2]
    bg_src = refs[:n_bg]
    del refs[:n_bg]
    f_ref, aux_ref = refs[:2]
    bg_dst = refs[2:2 + n_bg]
    del refs[:2 + n_bg]
    xn_sc = refs.pop(0)
    if hist_period:
        a_ref = aux_ref
    else:
        tail_ref, carry_sc = aux_ref, refs.pop(0)
    i = pl.program_id(0)
    j = pl.program_id(1)
    if n_bg:
        _background_cast(list(zip(bg_src, bg_dst)), *refs, i * pl.num_programs(1) + j, n_steps)

    @pl.when(j == 0)
    def _():
        xn_sc[...] = _rms_norm(h_ref[...], g_ref[...]).astype(BF16)

    if not hist_period:
        @pl.when(i % tiles_per_seq == 0)
        def _():
            carry_sc[j] = jnp.zeros(carry_sc.shape[1:], F32)

        tail = carry_sc[j]
    tm = xn_sc.shape[0]
    rb = min(tm, ROW_BLK)
    for r in range(tm // rb):
        rows = slice(r * rb, (r + 1) * rb)
        xn = xn_sc[rows, :]
        a = jnp.dot(xn, wa_ref[...], preferred_element_type=F32)
        gate = jnp.dot(xn, wg_ref[...], preferred_element_type=F32)
        if hist_period:
            p1, p2 = _shifted_rows(a, p1_ref[rows, :], p2_ref[rows, :], p2_ref[rows, :], hist_period)
            a_ref[rows, :] = a
        else:
            last1 = tail[HALO - 1:HALO, :]
            p1, p2 = _shifted_rows(a, last1, tail[HALO - 2:HALO - 1, :], last1, rb)
            tail = a[rb - HALO:rb, :]
        ua = cw_ref[0:1, :] * p2 + cw_ref[1:2, :] * p1 + cw_ref[2:3, :] * a
        f_ref[rows, :] = (jax.nn.gelu(ua, approximate=True) * gate).astype(BF16)
    if not hist_period:
        carry_sc[j] = tail
        tail_ref[...] = tail


def _ffn_up(h, g, w_up, cw, hist, seq, tm, background=()):
    t = h.shape[0]
    n_j = D_FF // TN_FF
    grid = (t // tm, n_j)
    in_specs = [
        pl.BlockSpec((tm, D_MODEL), lambda i, j: (i, 0)),
        pl.BlockSpec((1, D_MODEL), lambda i, j: (0, 0)),
        pl.BlockSpec((D_MODEL, TN_FF), lambda i, j: (0, j)),
        pl.BlockSpec((D_MODEL, TN_FF), lambda i, j: (0, j + n_j)),
        pl.BlockSpec((CONV_W, TN_FF), lambda i, j: (0, j)),
    ]
    args = [h, g, w_up, w_up, cw]
    scratch = [pltpu.VMEM((tm, D_MODEL), BF16)]
    f_spec = pl.BlockSpec((tm, TN_FF), lambda i, j: (i, j))
    f_shape = jax.ShapeDtypeStruct((t, D_FF), BF16)
    if hist is None:
        hist_period = 0
        scratch.append(pltpu.VMEM((n_j, HALO, TN_FF), F32))
        out_specs = (f_spec, pl.BlockSpec((None, HALO, TN_FF), lambda i, j: (i, 0, j)))
        out_shape = (f_shape, jax.ShapeDtypeStruct((t // tm, HALO, D_FF), F32))
    else:
        hist_period = seq
        in_specs += [pl.BlockSpec((tm, TN_FF), lambda i, j: (i, j))] * 2
        args += list(hist)
        out_specs = (f_spec, pl.BlockSpec((tm, TN_FF), lambda i, j: (i, j)))
        out_shape = (f_shape, jax.ShapeDtypeStruct((t, D_FF), F32))
    nbytes = (2 * tm * D_MODEL * 4 + tm * D_MODEL * 2 + 4 * D_MODEL * TN_FF * 2
              + 2 * tm * TN_FF * (2 + 4 * 3) + 8 * tm * TN_FF * 4)
    if background:
        in_specs += [pl.BlockSpec(memory_space=pl.ANY)] * len(background)
        args += list(background)
        out_specs += (pl.BlockSpec(memory_space=pl.ANY),) * len(background)
        out_shape += tuple(jax.ShapeDtypeStruct(bw.shape, BF16) for bw in background)
        scratch += _background_scratch(BG_ROWS_2K, D_MODEL)
        nbytes += _background_bytes(BG_ROWS_2K, D_MODEL)
    return pl.pallas_call(
        functools.partial(_ffn_up_body, tiles_per_seq=max(seq // tm, 1), hist_period=hist_period,
                          n_bg=len(background), n_steps=grid[0] * grid[1]),
        grid=grid,
        in_specs=in_specs,
        out_specs=out_specs,
        out_shape=out_shape,
        scratch_shapes=scratch,
        compiler_params=_params(2, nbytes),
        name="ffn_up",
    )(*args)


def _ffn_down_body(f_ref, w_ref, h_ref, g_ref, o_ref):
    tm = f_ref.shape[0]
    rb = min(tm, ROW_BLK)
    for r in range(tm // rb):
        rows = slice(r * rb, (r + 1) * rb)
        acc = jnp.dot(f_ref[rows, :], w_ref[...], preferred_element_type=F32)
        o_ref[rows, :] = h_ref[rows, :] + _rms_norm(acc, g_ref[...])


def _ffn_down(f, w, h, g, tm):
    t = f.shape[0]
    nbytes = (2 * tm * (D_FF * 2 + D_MODEL * 4 * 2) + D_FF * D_MODEL * 2
              + 2 * min(tm, ROW_BLK) * D_MODEL * 4)
    return pl.pallas_call(
        _ffn_down_body,
        grid=(t // tm,),
        in_specs=[
            pl.BlockSpec((tm, D_FF), lambda i: (i, 0)),
            pl.BlockSpec((D_FF, D_MODEL), lambda i: (0, 0), pipeline_mode=pl.Buffered(1)),
            pl.BlockSpec((tm, D_MODEL), lambda i: (i, 0)),
            pl.BlockSpec((1, D_MODEL), lambda i: (0, 0)),
        ],
        out_specs=pl.BlockSpec((tm, D_MODEL), lambda i: (i, 0)),
        out_shape=jax.ShapeDtypeStruct((t, D_MODEL), F32),
        compiler_params=_params(1, nbytes),
        name="ffn_down",
    )(f, w, h, g)


def _rope_tables(pos):
    half = ROT_DIM // 2
    inv = jnp.power(jnp.float32(ROPE_THETA), -jnp.arange(half, dtype=F32) * (2.0 / ROT_DIM))
    ang = pos.astype(F32)[:, None] * inv[None, :]
    cos = jnp.cos(ang)
    sin = jnp.sin(ang)
    rest = HEAD_DIM - ROT_DIM
    cos_h = jnp.concatenate([cos, cos, jnp.ones((pos.shape[0], rest), F32)], axis=1)
    sin_h = jnp.concatenate([-sin, sin, jnp.zeros((pos.shape[0], rest), F32)], axis=1)
    reps = LANES // HEAD_DIM
    return jnp.tile(cos_h, (1, reps)), jnp.tile(sin_h, (1, reps))


def _row_history(state, seq):
    b, _, c = state.shape
    p1 = jnp.concatenate([state[:, 1:2], jnp.zeros((b, seq - 1, c), F32)], axis=1)
    p2 = jnp.concatenate([state[:, 0:1], state[:, 1:2], jnp.zeros((b, seq - 2, c), F32)], axis=1)
    return p1.reshape(b * seq, c), p2.reshape(b * seq, c)


def kernel(x_prompt, x_sample, cache_k, cache_v, state_conv_mix, state_conv_ffn, norm_mix_pre, w_in,
           attn_sinks, conv_mix_w, w_attn_o, w_conv_o, w_mix_out, norm_mix_post, norm_ffn_pre,
           w_ffn_up, conv_ffn_w, w_ffn_down, norm_ffn_post):
    assert w_in.shape == (1, D_MODEL, D_IN), "one layer only"
    batch, seq, _ = x_prompt.shape
    dec_batch, dec_seq, _ = x_sample.shape
    assert seq % 1024 == 0 and (dec_batch * dec_seq) % 256 == 0 and dec_seq % 16 == 0
    assert cache_k.shape[2] == WINDOW

    g_pre = norm_mix_pre[0][None]
    g_post = norm_mix_post[0][None]
    g_ffn_pre = norm_ffn_pre[0][None]
    g_ffn_post = norm_ffn_post[0][None]
    sinks = attn_sinks[0]
    cw_mix = conv_mix_w[0]
    cw_ffn = conv_ffn_w[0]

    tp = batch * seq
    xp = x_prompt.reshape(tp, D_MODEL)
    cos_p, sin_p = _rope_tables(jnp.arange(seq, dtype=F32))
    q, k, v, u, b, ga, gc, w_in_b, wao, wco, wmo = _inproj(
        xp, g_pre, cos_p, sin_p, w_in[0], tm=256, background=(w_attn_o[0], w_conv_o[0], w_mix_out[0]))
    hp, w_up = _mix(xp, None, u, b, ga, gc, None, cw_mix, wao, wco, wmo, g_post, seq, tm=256,
                    background=(w_ffn_up[0],), attn=(sinks, q, k, v))
    f, a_tail, w_down = _ffn_up(hp, g_ffn_pre, w_up, cw_ffn, None, seq, tm=FF_UP_TM,
                                background=(w_ffn_down[0],))
    yp = _ffn_down(f, w_down, hp, g_ffn_post, tm=512)

    keep = min(WINDOW, seq)
    new_k_prompt = k.reshape(batch, seq, KV_W)[:, seq - keep:].reshape(1, batch, keep, N_KV_HEADS, HEAD_DIM)
    new_v_prompt = v.reshape(batch, seq, KV_W)[:, seq - keep:].reshape(1, batch, keep, N_KV_HEADS, HEAD_DIM)
    conv_mix_prompt = u.reshape(batch, seq, D_CONV)[:, seq - (CONV_W - 1):][None]
    a_tail = a_tail.reshape(batch, seq // FF_UP_TM, HALO, D_FF)
    conv_ffn_prompt = a_tail[:, -1, HALO - (CONV_W - 1):][None]

    ts = dec_batch * dec_seq
    xs = x_sample.reshape(ts, D_MODEL)
    pos_s = PAST_LEN + jnp.arange(dec_seq, dtype=F32)
    cos_s, sin_s = _rope_tables(jnp.tile(pos_s, dec_batch))
    qs, ks, vs, us, bs, gas, gcs = _inproj(xs, g_pre, cos_s, sin_s, w_in_b, tm=ts)
    n_valid = WINDOW + dec_seq
    n_keys = 2 * WINDOW
    pad = jnp.zeros((dec_batch, n_keys - n_valid, KV_W), F32)
    k_all = jnp.concatenate([cache_k[0].reshape(dec_batch, WINDOW, KV_W),
                             ks.reshape(dec_batch, dec_seq, KV_W), pad], axis=1)
    v_all = jnp.concatenate([cache_v[0].reshape(dec_batch, WINDOW, KV_W),
                             vs.reshape(dec_batch, dec_seq, KV_W), pad], axis=1)
    oas = _attn_sample(sinks, qs, k_all.reshape(dec_batch * n_keys, KV_W),
                       v_all.reshape(dec_batch * n_keys, KV_W), n_valid, n_keys, dec_seq, seqs_per_step=4)
    (hs,) = _mix(xs, oas, us, bs, gas, gcs, _row_history(state_conv_mix[0], dec_seq),
                 cw_mix, wao, wco, wmo, g_post, dec_seq, tm=ts)
    fs, a_s = _ffn_up(hs, g_ffn_pre, w_up, cw_ffn, _row_history(state_conv_ffn[0], dec_seq), dec_seq, tm=ts)
    ys = _ffn_down(fs, w_down, hs, g_ffn_post, tm=ts)

    new_k_sample = ks.reshape(1, dec_batch, dec_seq, N_KV_HEADS, HEAD_DIM)
    new_v_sample = vs.reshape(1, dec_batch, dec_seq, N_KV_HEADS, HEAD_DIM)
    conv_mix_sample = us.reshape(dec_batch, dec_seq, D_CONV)[:, dec_seq - (CONV_W - 1):][None]
    conv_ffn_sample = a_s.reshape(dec_batch, dec_seq, D_FF)[:, dec_seq - (CONV_W - 1):][None]

    return (yp.reshape(batch, seq, D_MODEL), ys.reshape(dec_batch, dec_seq, D_MODEL),
            new_k_prompt, new_v_prompt, conv_mix_prompt, conv_ffn_prompt,
            new_k_sample, new_v_sample, conv_mix_sample, conv_ffn_sample)
```

```python
import functools

import jax
import jax.numpy as jnp
from jax import lax
from jax.experimental import pallas as pl
from jax.experimental.pallas import tpu as pltpu

F32 = jnp.float32
BF16 = jnp.bfloat16

D_MODEL = 2048
CHUNK = 64
N_HEADS = 16
N_KV_HEADS = 4
GROUP = N_HEADS // N_KV_HEADS
HEAD_DIM = 64
ROT_DIM = HEAD_DIM // 4
ROPE_THETA = 500000.0
WINDOW = 128
D_CONV = 1024
CONV_W = 3
D_FF = 5632
EPS = 1e-6
NEG = -1e30
PAST_LEN = 4096
Q_W = N_HEADS * HEAD_DIM
KV_W = N_KV_HEADS * HEAD_DIM
D_IN = Q_W + 2 * KV_W + 3 * D_CONV + 2 * D_MODEL

LANES = 128
SUBLANES = 8
VMEM_LIMIT_CAP = 60 * 1024 * 1024
VMEM_COMPILER_SLACK = 6 * 1024 * 1024

TN_IN = 512
TN_FF = 512
FF_UP_TM = 1024
IN_STAGE_ROWS = 64
BG_ROWS_2K = 256
BG_ROWS_UP = 128
ATT_BLK = 2 * CHUNK
HALO = SUBLANES
ROW_BLK = 256

_T_Q = (0, 2)
_T_KV = (2, 3)
_T_XIN = (3, 5)
_T_B = (5, 7)
_T_C = (7, 9)
_T_GA = (9, 13)
_T_GC = (13, 17)


def _vmem_limit(nbytes):
    return int(min(nbytes + VMEM_COMPILER_SLACK, VMEM_LIMIT_CAP))


def _params(n_axes, nbytes):
    return pltpu.CompilerParams(
        dimension_semantics=("arbitrary",) * n_axes,
        vmem_limit_bytes=_vmem_limit(nbytes),
    )


def _rms_norm(x, g):
    return x * lax.rsqrt(jnp.mean(x * x, axis=-1, keepdims=True) + EPS) * g


def _rope128(x, cos, sin):
    dh = lax.broadcasted_iota(jnp.int32, x.shape, 1) & (HEAD_DIM - 1)
    half = ROT_DIM // 2
    partner = jnp.where(dh < half, pltpu.roll(x, LANES - half, axis=1), pltpu.roll(x, half, axis=1))
    return x * cos + partner * sin


def _shifted_rows(a, prev1_row0, prev2_row0, prev2_row1, period):
    assert period & (period - 1) == 0
    rows = lax.broadcasted_iota(jnp.int32, (a.shape[0], 1), 0) & (period - 1)
    p1 = jnp.where(rows == 0, prev1_row0, pltpu.roll(a, 1, axis=0))
    p2 = jnp.where(rows == 0, prev2_row0, jnp.where(rows == 1, prev2_row1, pltpu.roll(a, 2, axis=0)))
    return p1, p2


def _stage_weight(w_hbm, w_sc, w_out, stage, in_sems, out_sem, step, n_steps):
    chunk = stage.shape[1]
    n_chunks = w_hbm.shape[0] // chunk

    def fetch(c, slot):
        return pltpu.make_async_copy(w_hbm.at[pl.ds(c * chunk, chunk), :], stage.at[slot], in_sems.at[slot])

    export = pltpu.make_async_copy(w_sc, w_out, out_sem)

    @pl.when(step == 0)
    def _():
        fetch(0, 0).start()

        def body(c, carry):
            slot = lax.rem(c, 2)

            @pl.when(c + 1 < n_chunks)
            def _():
                fetch(c + 1, 1 - slot).start()

            fetch(c, slot).wait()
            w_sc[pl.ds(pl.multiple_of(c * chunk, chunk), chunk), :] = stage[slot].astype(BF16)
            return carry

        lax.fori_loop(0, n_chunks, body, 0)
        export.start()

    @pl.when(step == n_steps - 1)
    def _():
        export.wait()


def _background_cast(jobs, buf_in, buf_out, sems, step, n_steps):
    chunk = buf_in.shape[0]
    first = 0
    for src, dst in jobs:
        n = src.shape[0] // chunk
        assert src.shape[0] % chunk == 0 and first + n + 2 <= n_steps

        def fetch(c, src=src):
            return pltpu.make_async_copy(src.at[pl.ds(c * chunk, chunk), :], buf_in, sems.at[0])

        def send(c, dst=dst):
            return pltpu.make_async_copy(buf_out, dst.at[pl.ds(c * chunk, chunk), :], sems.at[1])

        c_fetch = step - first
        c_cast = c_fetch - 1
        c_done = c_fetch - 2

        @pl.when((c_done >= 0) & (c_done < n))
        def _():
            send(c_done).wait()

        @pl.when((c_cast >= 0) & (c_cast < n))
        def _():
            fetch(c_cast).wait()
            buf_out[...] = buf_in[...].astype(BF16)
            send(c_cast).start()

        @pl.when((c_fetch >= 0) & (c_fetch < n))
        def _():
            fetch(c_fetch).start()

        first += n


def _background_scratch(chunk, cols):
    return [
        pltpu.VMEM((chunk, cols), F32),
        pltpu.VMEM((chunk, cols), BF16),
        pltpu.SemaphoreType.DMA((2,)),
    ]


def _background_bytes(chunk, cols):
    return chunk * cols * (4 + 2)


def _staging_scratch(w_shape, chunk):
    assert w_shape[0] % chunk == 0
    return [
        pltpu.VMEM(w_shape, BF16),
        pltpu.VMEM((2, chunk, w_shape[1]), F32),
        pltpu.SemaphoreType.DMA((2,)),
        pltpu.SemaphoreType.DMA(()),
    ]


def _staging_bytes(w_shape, chunk):
    return w_shape[0] * w_shape[1] * 2 + 2 * chunk * w_shape[1] * 4


def _inproj_body(*refs, staged, n_bg, n_steps):
    refs = list(refs)
    x_ref, g_ref, cos_ref, sin_ref, w_ref = refs[:5]
    bg_src = refs[5:5 + n_bg]
    del refs[:5 + n_bg]
    q_ref, k_ref, v_ref, u_ref, b_ref, ga_ref, gc_ref = refs[:7]
    del refs[:7]
    if staged:
        w_hbm, w_out = w_ref, refs.pop(0)
    bg_dst = refs[:n_bg]
    del refs[:n_bg]
    xn_sc = refs.pop(0)
    step = pl.program_id(0)
    if staged:
        w_ref, stage, in_sems, out_sem = refs[:4]
        del refs[:4]
        _stage_weight(w_hbm, w_ref, w_out, stage, in_sems, out_sem, step, n_steps)
    if n_bg:
        _background_cast(list(zip(bg_src, bg_dst)), *refs, step, n_steps)
    xn_sc[...] = _rms_norm(x_ref[...], g_ref[...]).astype(BF16)
    cos = cos_ref[...]
    sin = sin_ref[...]

    def col(t):
        return jnp.dot(xn_sc[...], w_ref[:, t * TN_IN:(t + 1) * TN_IN], preferred_element_type=F32)

    def blocks(n):
        return [slice(blk * LANES, (blk + 1) * LANES) for blk in range(n)]

    for t in range(*_T_Q):
        acc = col(t)
        base = (t - _T_Q[0]) * TN_IN
        for sl in blocks(TN_IN // LANES):
            q_ref[:, base + sl.start:base + sl.stop] = (
                _rope128(acc[:, sl], cos, sin) * (HEAD_DIM ** -0.5)).astype(BF16)
    acc = col(_T_KV[0])
    for sl in blocks(KV_W // LANES):
        k_ref[:, sl] = _rope128(acc[:, sl], cos, sin)
    v_ref[...] = acc[:, KV_W:2 * KV_W]
    for c in range(_T_C[1] - _T_C[0]):
        cols = slice(c * TN_IN, (c + 1) * TN_IN)
        u_ref[:, cols] = col(_T_C[0] + c) * col(_T_XIN[0] + c)
        b_ref[:, cols] = col(_T_B[0] + c).astype(BF16)
    for c in range(_T_GA[1] - _T_GA[0]):
        cols = slice(c * TN_IN, (c + 1) * TN_IN)
        ga_ref[:, cols] = col(_T_GA[0] + c).astype(BF16)
        gc_ref[:, cols] = col(_T_GC[0] + c).astype(BF16)


def _inproj(x, g, cos, sin, w, tm, background=()):
    t = x.shape[0]
    n_pos = cos.shape[0] // tm
    row = lambda i: (i, 0)
    fixed = lambda i: (0, 0)
    pos = lambda i: (i % n_pos, 0)
    widths = (Q_W, KV_W, KV_W, D_CONV, D_CONV, D_MODEL, D_MODEL)
    dtypes = (BF16, F32, F32, F32, BF16, BF16, BF16)
    out_row_bytes = sum(wd * jnp.dtype(dt).itemsize for wd, dt in zip(widths, dtypes))
    nbytes = (2 * tm * (D_MODEL * 4 + 2 * LANES * 4 + out_row_bytes)
              + tm * D_MODEL * 2 + 6 * tm * TN_IN * 4)
    out_specs = [pl.BlockSpec((tm, wd), row) for wd in widths]
    out_shape = [jax.ShapeDtypeStruct((t, wd), dt) for wd, dt in zip(widths, dtypes)]
    scratch = [pltpu.VMEM((tm, D_MODEL), BF16)]
    staged = w.dtype == F32
    if staged:
        w_spec = pl.BlockSpec(memory_space=pl.ANY)
        out_specs.append(pl.BlockSpec(memory_space=pl.ANY))
        out_shape.append(jax.ShapeDtypeStruct(w.shape, BF16))
        scratch += _staging_scratch(w.shape, IN_STAGE_ROWS)
        nbytes += _staging_bytes(w.shape, IN_STAGE_ROWS)
    else:
        w_spec = pl.BlockSpec((D_MODEL, D_IN), fixed, pipeline_mode=pl.Buffered(1))
        nbytes += D_MODEL * D_IN * 2
    if background:
        out_specs += [pl.BlockSpec(memory_space=pl.ANY)] * len(background)
        out_shape += [jax.ShapeDtypeStruct(bw.shape, BF16) for bw in background]
        scratch += _background_scratch(BG_ROWS_2K, D_MODEL)
        nbytes += _background_bytes(BG_ROWS_2K, D_MODEL)
    return pl.pallas_call(
        functools.partial(_inproj_body, staged=staged, n_bg=len(background), n_steps=t // tm),
        grid=(t // tm,),
        in_specs=[
            pl.BlockSpec((tm, D_MODEL), row),
            pl.BlockSpec((1, D_MODEL), fixed),
            pl.BlockSpec((tm, LANES), pos),
            pl.BlockSpec((tm, LANES), pos),
            w_spec,
        ] + [pl.BlockSpec(memory_space=pl.ANY)] * len(background),
        out_specs=tuple(out_specs),
        out_shape=tuple(out_shape),
        scratch_shapes=scratch,
        compiler_params=_params(1, nbytes),
        name="inproj",
    )(x, g, cos, sin, w, *background)


def _head_pair_operands(x256, h):
    blk = x256[:, (h // 2) * LANES:(h // 2 + 1) * LANES]
    lo_lanes = lax.broadcasted_iota(jnp.int32, blk.shape, 1) < HEAD_DIM
    if h % 2 == 0:
        lo = jnp.where(lo_lanes, blk, 0.0)
        hi = pltpu.roll(lo, HEAD_DIM, axis=1)
    else:
        hi = jnp.where(lo_lanes, 0.0, blk)
        lo = pltpu.roll(hi, HEAD_DIM, axis=1)
    return jnp.concatenate([lo, hi], axis=0).astype(BF16)


def _attend(q_ref, o_ref, sink_ref, blocks, s_sc, m_sc, p_sc, v_sc, after_scores=None):
    n_keys = blocks[0][1].shape[0]
    n_rows = s_sc.shape[1]
    key_row = lax.broadcasted_iota(jnp.int32, (2 * n_keys, LANES), 0)
    key_lane = lax.broadcasted_iota(jnp.int32, (2 * n_keys, LANES), 1)
    ones_cat = jnp.where((key_row >= n_keys) == (key_lane >= HEAD_DIM), 1.0, 0.0).astype(BF16)
    lo_lanes = lax.broadcasted_iota(jnp.int32, (n_rows, LANES), 1) < HEAD_DIM
    halves = [slice(hh * n_keys, (hh + 1) * n_keys) for hh in range(2)]

    tiles = []
    n_v = 0
    for q_rows, k_all, v_all, valid in blocks:
        for h in range(N_KV_HEADS):
            kcat = _head_pair_operands(k_all, h)
            v_sc[n_v] = jnp.concatenate([_head_pair_operands(v_all, h), ones_cat], axis=1)
            for p in range(2 * h, 2 * h + 2):
                lanes = slice(p * LANES, (p + 1) * LANES)
                sc = lax.dot_general(q_ref[q_rows, lanes], kcat, (((1,), (1,)), ((), ())),
                                     preferred_element_type=F32)
                s_sc[len(tiles)] = jnp.where(valid, sc, NEG)
                tiles.append((q_rows, lanes, p, n_v))
            n_v += 1
    if after_scores is not None:
        after_scores()
    for t, (_, _, p, _) in enumerate(tiles):
        for hh in range(2):
            m = jnp.maximum(jnp.max(s_sc[t, :, halves[hh]], axis=-1, keepdims=True), sink_ref[2 * p + hh])
            m_sc[2 * t + hh] = jnp.broadcast_to(m, (n_rows, LANES))
    for t in range(len(tiles)):
        p_sc[t] = jnp.concatenate(
            [jnp.exp(s_sc[t, :, halves[hh]] - jnp.tile(m_sc[2 * t + hh], (1, n_keys // LANES))).astype(BF16)
             for hh in range(2)], axis=1)
    for t, (q_rows, lanes, p, vi) in enumerate(tiles):
        e = p_sc[t]
        num_den = jnp.dot(e, v_sc[vi], preferred_element_type=F32)
        num = num_den[:, :LANES]
        den = num_den[:, LANES:]
        sink_term = jnp.where(lo_lanes, jnp.exp(sink_ref[2 * p] - m_sc[2 * t]),
                              jnp.exp(sink_ref[2 * p + 1] - m_sc[2 * t + 1]))
        o_ref[q_rows, lanes] = (num / (den + sink_term)).astype(BF16)


def _attn_scratch(n_blocks, n_rows, n_keys):
    n_tiles = n_blocks * N_HEADS // 2
    return [
        pltpu.VMEM((n_tiles, n_rows, 2 * n_keys), F32),
        pltpu.VMEM((2 * n_tiles, n_rows, LANES), F32),
        pltpu.VMEM((n_tiles, n_rows, 2 * n_keys), BF16),
        pltpu.VMEM((n_blocks * N_KV_HEADS, 2 * n_keys, 2 * LANES), BF16),
    ]


def _attn_scratch_bytes(n_blocks, n_rows, n_keys):
    n_tiles = n_blocks * N_HEADS // 2
    return (n_tiles * n_rows * 2 * n_keys * 6 + 2 * n_tiles * n_rows * LANES * 4
            + n_blocks * N_KV_HEADS * 2 * n_keys * 2 * LANES * 2)


def _prompt_attention(first, sink_ref, q_ref, kc_ref, kp_ref, vc_ref, vp_ref, o_ref, scratch, after_scores=None):
    n_sub = q_ref.shape[0] // ATT_BLK
    n_keys = 2 * ATT_BLK
    shape = (ATT_BLK, 2 * n_keys)
    q_chunk = lax.broadcasted_iota(jnp.int32, shape, 0) // CHUNK
    k_chunk = (lax.broadcasted_iota(jnp.int32, shape, 1) & (n_keys - 1)) // CHUNK
    band = (k_chunk >= q_chunk) & (k_chunk <= q_chunk + WINDOW // CHUNK)
    first_valid = band & (k_chunk >= jnp.where(first, ATT_BLK // CHUNK, 0))
    blocks = []
    for s in range(n_sub):
        rows = slice(s * ATT_BLK, (s + 1) * ATT_BLK)
        if s == 0:
            k_prev, v_prev, valid = kp_ref[...], vp_ref[...], first_valid
        else:
            prev = slice((s - 1) * ATT_BLK, s * ATT_BLK)
            k_prev, v_prev, valid = kc_ref[prev, :], vc_ref[prev, :], band
        k_all = jnp.concatenate([k_prev, kc_ref[rows, :]], axis=0)
        v_all = jnp.concatenate([v_prev, vc_ref[rows, :]], axis=0)
        blocks.append((rows, k_all, v_all, valid))
    _attend(q_ref, o_ref, sink_ref, blocks, *scratch, after_scores=after_scores)


def _attn_sample_body(sink_ref, q_ref, k_ref, v_ref, o_ref, *scratch, n_valid, n_keys, rows_per_seq):
    shape = (rows_per_seq, 2 * n_keys)
    valid = (lax.broadcasted_iota(jnp.int32, shape, 1) & (n_keys - 1)) < n_valid
    blocks = []
    for b in range(q_ref.shape[0] // rows_per_seq):
        keys = slice(b * n_keys, (b + 1) * n_keys)
        blocks.append((slice(b * rows_per_seq, (b + 1) * rows_per_seq), k_ref[keys, :], v_ref[keys, :], valid))
    _attend(q_ref, o_ref, sink_ref, blocks, *scratch)


def _attn_sample(sinks, q, k_all, v_all, n_valid, n_keys, rows_per_seq, seqs_per_step):
    t = q.shape[0]
    tq = rows_per_seq * seqs_per_step
    tk = n_keys * seqs_per_step
    nbytes = 2 * (2 * tq * Q_W * 2 + 2 * tk * KV_W * 4) + _attn_scratch_bytes(seqs_per_step, rows_per_seq, n_keys)
    return pl.pallas_call(
        functools.partial(_attn_sample_body, n_valid=n_valid, n_keys=n_keys, rows_per_seq=rows_per_seq),
        grid=(t // tq,),
        in_specs=[
            pl.BlockSpec(memory_space=pltpu.SMEM),
            pl.BlockSpec((tq, Q_W), lambda i: (i, 0)),
            pl.BlockSpec((tk, KV_W), lambda i: (i, 0)),
            pl.BlockSpec((tk, KV_W), lambda i: (i, 0)),
        ],
        out_specs=pl.BlockSpec((tq, Q_W), lambda i: (i, 0)),
        out_shape=jax.ShapeDtypeStruct((t, Q_W), BF16),
        scratch_shapes=_attn_scratch(seqs_per_step, rows_per_seq, n_keys),
        compiler_params=_params(1, nbytes),
        name="attn_sample",
    )(sinks, q, k_all, v_all)


def _mix_body(*refs, tiles_per_seq, hist_period, fused_attn, n_bg, n_steps):
    refs = list(refs)
    if fused_attn:
        sink_ref, x_ref, q_ref, kc_ref, kp_ref, vc_ref, vp_ref = refs[:7]
        del refs[:7]
    else:
        x_ref, oa_ref = refs[:2]
        del refs[:2]
    u_ref, b_ref, ga_ref, gc_ref = refs[:4]
    del refs[:4]
    if hist_period:
        p1_ref, p2_ref = refs[:2]
        del refs[:2]
    cw_ref, wao_ref, wco_ref, wmo_ref, g_ref = refs[:5]
    bg_src = refs[5:5 + n_bg]
    del refs[:5 + n_bg]
    h_ref = refs.pop(0)
    bg_dst = refs[:n_bg]
    del refs[:n_bg]
    if not hist_period:
        carry_sc = refs.pop(0)
    if fused_attn:
        oa_ref = refs.pop(0)
        attn_scratch = refs[:4]
        del refs[:4]
    if n_bg:
        _background_cast(list(zip(bg_src, bg_dst)), *refs, pl.program_id(0), n_steps)
    u = u_ref[...]
    tm = u.shape[0]
    if hist_period:
        p1, p2 = _shifted_rows(u, p1_ref[...], p2_ref[...], p2_ref[...], hist_period)
    else:
        @pl.when(pl.program_id(0) % tiles_per_seq == 0)
        def _():
            carry_sc[...] = jnp.zeros_like(carry_sc)

        last1 = carry_sc[HALO - 1:HALO, :]
        last2 = carry_sc[HALO - 2:HALO - 1, :]
        p1, p2 = _shifted_rows(u, last1, last2, last1, tm)
        carry_sc[...] = u[tm - HALO:tm, :]
    cy = cw_ref[0:1, :] * p2 + cw_ref[1:2, :] * p1 + cw_ref[2:3, :] * u
    y_conv = (b_ref[...].astype(F32) * cy).astype(BF16)
    if fused_attn:
        conv_branch = []
        first = (pl.program_id(0) % tiles_per_seq) == 0
        _prompt_attention(
            first, sink_ref, q_ref, kc_ref, kp_ref, vc_ref, vp_ref, oa_ref, attn_scratch,
            after_scores=lambda: conv_branch.append(jnp.dot(y_conv, wco_ref[...], preferred_element_type=F32)))
        conv_o = conv_branch[0]
    else:
        conv_o = jnp.dot(y_conv, wco_ref[...], preferred_element_type=F32)
    attn_o = jnp.dot(oa_ref[...], wao_ref[...], preferred_element_type=F32)
    mixed = (jax.nn.sigmoid(ga_ref[...].astype(F32)) * attn_o
             + jax.nn.sigmoid(gc_ref[...].astype(F32)) * conv_o)
    mo = jnp.dot(mixed.astype(BF16), wmo_ref[...], preferred_element_type=F32)
    h_ref[...] = x_ref[...] + _rms_norm(mo, g_ref[...])


def _mix(x, oa, u, b, ga, gc, hist, cw, wao, wco, wmo, g, seq, tm, background=(), attn=None):
    t = x.shape[0]
    row = lambda i: (i, 0)
    fixed = lambda i: (0, 0)
    scratch = []
    if attn is None:
        act_specs = [pl.BlockSpec((tm, D_MODEL), row), pl.BlockSpec((tm, Q_W), row)]
        acts = [x, oa]
    else:
        sinks, q, k, v = attn
        ratio = tm // ATT_BLK
        prev = lambda i: (jnp.maximum(i * ratio - 1, 0), 0)
        act_specs = [
            pl.BlockSpec(memory_space=pltpu.SMEM),
            pl.BlockSpec((tm, D_MODEL), row),
            pl.BlockSpec((tm, Q_W), row),
            pl.BlockSpec((tm, KV_W), row),
            pl.BlockSpec((ATT_BLK, KV_W), prev),
            pl.BlockSpec((tm, KV_W), row),
            pl.BlockSpec((ATT_BLK, KV_W), prev),
        ]
        acts = [sinks, x, q, k, k, v, v]
    act_specs += [
        pl.BlockSpec((tm, D_CONV), row),
        pl.BlockSpec((tm, D_CONV), row),
        pl.BlockSpec((tm, D_MODEL), row),
        pl.BlockSpec((tm, D_MODEL), row),
    ]
    acts += [u, b, ga, gc]
    if hist is None:
        hist_period = 0
        scratch.append(pltpu.VMEM((HALO, D_CONV), F32))
    else:
        hist_period = seq
        act_specs += [pl.BlockSpec((tm, D_CONV), row)] * 2
        acts += list(hist)
    attn_bytes = 0
    if attn is not None:
        scratch.append(pltpu.VMEM((tm, Q_W), BF16))
        scratch += _attn_scratch(tm // ATT_BLK, ATT_BLK, 2 * ATT_BLK)
        attn_bytes = (tm * Q_W * 2 + 4 * (tm + ATT_BLK) * KV_W * 4
                      + _attn_scratch_bytes(tm // ATT_BLK, ATT_BLK, 2 * ATT_BLK))
    weights = (wao, wco, wmo)
    out_specs = [pl.BlockSpec((tm, D_MODEL), row)]
    out_shape = [jax.ShapeDtypeStruct((t, D_MODEL), F32)]
    row_bytes = D_MODEL * 4 * 2 + Q_W * 2 + D_CONV * (4 + 2) + D_MODEL * 2 * 2
    if hist is not None:
        row_bytes += 2 * D_CONV * 4
    nbytes = (2 * tm * row_bytes + (Q_W + D_CONV + D_MODEL) * D_MODEL * 2 + 6 * tm * D_MODEL * 4
              + attn_bytes)
    if background:
        cols = background[0].shape[1]
        out_specs += [pl.BlockSpec(memory_space=pl.ANY)] * len(background)
        out_shape += [jax.ShapeDtypeStruct(bw.shape, BF16) for bw in background]
        scratch += _background_scratch(BG_ROWS_UP, cols)
        nbytes += _background_bytes(BG_ROWS_UP, cols)
    in_specs = (act_specs + [pl.BlockSpec((CONV_W, D_CONV), fixed)]
                + [pl.BlockSpec(w.shape, fixed, pipeline_mode=pl.Buffered(1)) for w in weights]
                + [pl.BlockSpec((1, D_MODEL), fixed)] + [pl.BlockSpec(memory_space=pl.ANY)] * len(background))
    return pl.pallas_call(
        functools.partial(_mix_body, tiles_per_seq=max(seq // tm, 1), hist_period=hist_period,
                          fused_attn=attn is not None, n_bg=len(background), n_steps=t // tm),
        grid=(t // tm,),
        in_specs=in_specs,
        out_specs=tuple(out_specs),
        out_shape=tuple(out_shape),
        scratch_shapes=scratch,
        compiler_params=_params(1, nbytes),
        name="mix",
    )(*acts, cw, wao, wco, wmo, g, *background)


def _ffn_up_body(*refs, tiles_per_seq, hist_period, n_bg, n_steps):
    refs = list(refs)
    h_ref, g_ref, wa_ref, wg_ref, cw_ref = refs[:5]
    del refs[:5]
    if hist_period:
        p1_ref, p2_ref = refs[:2]
        del refs[:2]
    bg_src = refs[:n_bg]
    del refs[:n_bg]
    f_ref, aux_ref = refs[:2]
    bg_dst = refs[2:2 + n_bg]
    del refs[:2 + n_bg]
    xn_sc = refs.pop(0)
    if hist_period:
        a_ref = aux_ref
    else:
        tail_ref, carry_sc = aux_ref, refs.pop(0)
    i = pl.program_id(0)
    j = pl.program_id(1)
    if n_bg:
        _background_cast(list(zip(bg_src, bg_dst)), *refs, i * pl.num_programs(1) + j, n_steps)

    @pl.when(j == 0)
    def _():
        xn_sc[...] = _rms_norm(h_ref[...], g_ref[...]).astype(BF16)

    if not hist_period:
        @pl.when(i % tiles_per_seq == 0)
        def _():
            carry_sc[j] = jnp.zeros(carry_sc.shape[1:], F32)

        tail = carry_sc[j]
    tm = xn_sc.shape[0]
    rb = min(tm, ROW_BLK)
    for r in range(tm // rb):
        rows = slice(r * rb, (r + 1) * rb)
        xn = xn_sc[rows, :]
        a = jnp.dot(xn, wa_ref[...], preferred_element_type=F32)
        gate = jnp.dot(xn, wg_ref[...], preferred_element_type=F32)
        if hist_period:
            p1, p2 = _shifted_rows(a, p1_ref[rows, :], p2_ref[rows, :], p2_ref[rows, :], hist_period)
            a_ref[rows, :] = a
        else:
            last1 = tail[HALO - 1:HALO, :]
            p1, p2 = _shifted_rows(a, last1, tail[HALO - 2:HALO - 1, :], last1, rb)
            tail = a[rb - HALO:rb, :]
        ua = cw_ref[0:1, :] * p2 + cw_ref[1:2, :] * p1 + cw_ref[2:3, :] * a
        f_ref[rows, :] = (jax.nn.gelu(ua, approximate=True) * gate).astype(BF16)
    if not hist_period:
        carry_sc[j] = tail
        tail_ref[...] = tail


def _ffn_up(h, g, w_up, cw, hist, seq, tm, background=()):
    t = h.shape[0]
    n_j = D_FF // TN_FF
    grid = (t // tm, n_j)
    in_specs = [
        pl.BlockSpec((tm, D_MODEL), lambda i, j: (i, 0)),
        pl.BlockSpec((1, D_MODEL), lambda i, j: (0, 0)),
        pl.BlockSpec((D_MODEL, TN_FF), lambda i, j: (0, j)),
        pl.BlockSpec((D_MODEL, TN_FF), lambda i, j: (0, j + n_j)),
        pl.BlockSpec((CONV_W, TN_FF), lambda i, j: (0, j)),
    ]
    args = [h, g, w_up, w_up, cw]
    scratch = [pltpu.VMEM((tm, D_MODEL), BF16)]
    f_spec = pl.BlockSpec((tm, TN_FF), lambda i, j: (i, j))
    f_shape = jax.ShapeDtypeStruct((t, D_FF), BF16)
    if hist is None:
        hist_period = 0
        scratch.append(pltpu.VMEM((n_j, HALO, TN_FF), F32))
        out_specs = (f_spec, pl.BlockSpec((None, HALO, TN_FF), lambda i, j: (i, 0, j)))
        out_shape = (f_shape, jax.ShapeDtypeStruct((t // tm, HALO, D_FF), F32))
    else:
        hist_period = seq
        in_specs += [pl.BlockSpec((tm, TN_FF), lambda i, j: (i, j))] * 2
        args += list(hist)
        out_specs = (f_spec, pl.BlockSpec((tm, TN_FF), lambda i, j: (i, j)))
        out_shape = (f_shape, jax.ShapeDtypeStruct((t, D_FF), F32))
    nbytes = (2 * tm * D_MODEL * 4 + tm * D_MODEL * 2 + 4 * D_MODEL * TN_FF * 2
              + 2 * tm * TN_FF * (2 + 4 * 3) + 8 * tm * TN_FF * 4)
    if background:
        in_specs += [pl.BlockSpec(memory_space=pl.ANY)] * len(background)
        args += list(background)
        out_specs += (pl.BlockSpec(memory_space=pl.ANY),) * len(background)
        out_shape += tuple(jax.ShapeDtypeStruct(bw.shape, BF16) for bw in background)
        scratch += _background_scratch(BG_ROWS_2K, D_MODEL)
        nbytes += _background_bytes(BG_ROWS_2K, D_MODEL)
    return pl.pallas_call(
        functools.partial(_ffn_up_body, tiles_per_seq=max(seq // tm, 1), hist_period=hist_period,
                          n_bg=len(background), n_steps=grid[0] * grid[1]),
        grid=grid,
        in_specs=in_specs,
        out_specs=out_specs,
        out_shape=out_shape,
        scratch_shapes=scratch,
        compiler_params=_params(2, nbytes),
        name="ffn_up",
    )(*args)


def _ffn_down_body(f_ref, w_ref, h_ref, g_ref, o_ref):
    tm = f_ref.shape[0]
    rb = min(tm, ROW_BLK)
    for r in range(tm // rb):
        rows = slice(r * rb, (r + 1) * rb)
        acc = jnp.dot(f_ref[rows, :], w_ref[...], preferred_element_type=F32)
        o_ref[rows, :] = h_ref[rows, :] + _rms_norm(acc, g_ref[...])


def _ffn_down(f, w, h, g, tm):
    t = f.shape[0]
    nbytes = (2 * tm * (D_FF * 2 + D_MODEL * 4 * 2) + D_FF * D_MODEL * 2
              + 2 * min(tm, ROW_BLK) * D_MODEL * 4)
    return pl.pallas_call(
        _ffn_down_body,
        grid=(t // tm,),
        in_specs=[
            pl.BlockSpec((tm, D_FF), lambda i: (i, 0)),
            pl.BlockSpec((D_FF, D_MODEL), lambda i: (0, 0), pipeline_mode=pl.Buffered(1)),
            pl.BlockSpec((tm, D_MODEL), lambda i: (i, 0)),
            pl.BlockSpec((1, D_MODEL), lambda i: (0, 0)),
        ],
        out_specs=pl.BlockSpec((tm, D_MODEL), lambda i: (i, 0)),
        out_shape=jax.ShapeDtypeStruct((t, D_MODEL), F32),
        compiler_params=_params(1, nbytes),
        name="ffn_down",
    )(f, w, h, g)


def _rope_tables(pos):
    half = ROT_DIM // 2
    inv = jnp.power(jnp.float32(ROPE_THETA), -jnp.arange(half, dtype=F32) * (2.0 / ROT_DIM))
    ang = pos.astype(F32)[:, None] * inv[None, :]
    cos = jnp.cos(ang)
    sin = jnp.sin(ang)
    rest = HEAD_DIM - ROT_DIM
    cos_h = jnp.concatenate([cos, cos, jnp.ones((pos.shape[0], rest), F32)], axis=1)
    sin_h = jnp.concatenate([-sin, sin, jnp.zeros((pos.shape[0], rest), F32)], axis=1)
    reps = LANES // HEAD_DIM
    return jnp.tile(cos_h, (1, reps)), jnp.tile(sin_h, (1, reps))


def _row_history(state, seq):
    b, _, c = state.shape
    p1 = jnp.concatenate([state[:, 1:2], jnp.zeros((b, seq - 1, c), F32)], axis=1)
    p2 = jnp.concatenate([state[:, 0:1], state[:, 1:2], jnp.zeros((b, seq - 2, c), F32)], axis=1)
    return p1.reshape(b * seq, c), p2.reshape(b * seq, c)


def kernel(x_prompt, x_sample, cache_k, cache_v, state_conv_mix, state_conv_ffn, norm_mix_pre, w_in,
           attn_sinks, conv_mix_w, w_attn_o, w_conv_o, w_mix_out, norm_mix_post, norm_ffn_pre,
           w_ffn_up, conv_ffn_w, w_ffn_down, norm_ffn_post):
    assert w_in.shape == (1, D_MODEL, D_IN), "one layer only"
    batch, seq, _ = x_prompt.shape
    dec_batch, dec_seq, _ = x_sample.shape
    assert seq % 1024 == 0 and (dec_batch * dec_seq) % 256 == 0 and dec_seq % 16 == 0
    assert cache_k.shape[2] == WINDOW

    g_pre = norm_mix_pre[0][None]
    g_post = norm_mix_post[0][None]
    g_ffn_pre = norm_ffn_pre[0][None]
    g_ffn_post = norm_ffn_post[0][None]
    sinks = attn_sinks[0]
    cw_mix = conv_mix_w[0]
    cw_ffn = conv_ffn_w[0]

    tp = batch * seq
    xp = x_prompt.reshape(tp, D_MODEL)
    cos_p, sin_p = _rope_tables(jnp.arange(seq, dtype=F32))
    q, k, v, u, b, ga, gc, w_in_b, wao, wco, wmo = _inproj(
        xp, g_pre, cos_p, sin_p, w_in[0], tm=256, background=(w_attn_o[0], w_conv_o[0], w_mix_out[0]))
    hp, w_up = _mix(xp, None, u, b, ga, gc, None, cw_mix, wao, wco, wmo, g_post, seq, tm=256,
                    background=(w_ffn_up[0],), attn=(sinks, q, k, v))
    f, a_tail, w_down = _ffn_up(hp, g_ffn_pre, w_up, cw_ffn, None, seq, tm=FF_UP_TM,
                                background=(w_ffn_down[0],))
    yp = _ffn_down(f, w_down, hp, g_ffn_post, tm=512)

    keep = min(WINDOW, seq)
    new_k_prompt = k.reshape(batch, seq, KV_W)[:, seq - keep:].reshape(1, batch, keep, N_KV_HEADS, HEAD_DIM)
    new_v_prompt = v.reshape(batch, seq, KV_W)[:, seq - keep:].reshape(1, batch, keep, N_KV_HEADS, HEAD_DIM)
    conv_mix_prompt = u.reshape(batch, seq, D_CONV)[:, seq - (CONV_W - 1):][None]
    a_tail = a_tail.reshape(batch, seq // FF_UP_TM, HALO, D_FF)
    conv_ffn_prompt = a_tail[:, -1, HALO - (CONV_W - 1):][None]

    ts = dec_batch * dec_seq
    xs = x_sample.reshape(ts, D_MODEL)
    pos_s = PAST_LEN + jnp.arange(dec_seq, dtype=F32)
    cos_s, sin_s = _rope_tables(jnp.tile(pos_s, dec_batch))
    qs, ks, vs, us, bs, gas, gcs = _inproj(xs, g_pre, cos_s, sin_s, w_in_b, tm=ts)
    n_valid = WINDOW + dec_seq
    n_keys = 2 * WINDOW
    pad = jnp.zeros((dec_batch, n_keys - n_valid, KV_W), F32)
    k_all = jnp.concatenate([cache_k[0].reshape(dec_batch, WINDOW, KV_W),
                             ks.reshape(dec_batch, dec_seq, KV_W), pad], axis=1)
    v_all = jnp.concatenate([cache_v[0].reshape(dec_batch, WINDOW, KV_W),
                             vs.reshape(dec_batch, dec_seq, KV_W), pad], axis=1)
    oas = _attn_sample(sinks, qs, k_all.reshape(dec_batch * n_keys, KV_W),
                       v_all.reshape(dec_batch * n_keys, KV_W), n_valid, n_keys, dec_seq, seqs_per_step=4)
    (hs,) = _mix(xs, oas, us, bs, gas, gcs, _row_history(state_conv_mix[0], dec_seq),
                 cw_mix, wao, wco, wmo, g_post, dec_seq, tm=ts)
    fs, a_s = _ffn_up(hs, g_ffn_pre, w_up, cw_ffn, _row_history(state_conv_ffn[0], dec_seq), dec_seq, tm=ts)
    ys = _ffn_down(fs, w_down, hs, g_ffn_post, tm=ts)

    new_k_sample = ks.reshape(1, dec_batch, dec_seq, N_KV_HEADS, HEAD_DIM)
    new_v_sample = vs.reshape(1, dec_batch, dec_seq, N_KV_HEADS, HEAD_DIM)
    conv_mix_sample = us.reshape(dec_batch, dec_seq, D_CONV)[:, dec_seq - (CONV_W - 1):][None]
    conv_ffn_sample = a_s.reshape(dec_batch, dec_seq, D_FF)[:, dec_seq - (CONV_W - 1):][None]

    return (yp.reshape(batch, seq, D_MODEL), ys.reshape(dec_batch, dec_seq, D_MODEL),
            new_k_prompt, new_v_prompt, conv_mix_prompt, conv_ffn_prompt,
            new_k_sample, new_v_sample, conv_mix_sample, conv_ffn_sample)
```

```python
import functools

import jax
import jax.numpy as jnp
from jax import lax
from jax.experimental import pallas as pl
from jax.experimental.pallas import tpu as pltpu

F32 = jnp.float32
BF16 = jnp.bfloat16

D_MODEL = 2048
CHUNK = 64
N_HEADS = 16
N_KV_HEADS = 4
GROUP = N_HEADS // N_KV_HEADS
HEAD_DIM = 64
ROT_DIM = HEAD_DIM // 4
ROPE_THETA = 500000.0
WINDOW = 128
D_CONV = 1024
CONV_W = 3
D_FF = 5632
EPS = 1e-6
NEG = -1e30
PAST_LEN = 4096
Q_W = N_HEADS * HEAD_DIM
KV_W = N_KV_HEADS * HEAD_DIM
D_IN = Q_W + 2 * KV_W + 3 * D_CONV + 2 * D_MODEL

LANES = 128
SUBLANES = 8
VMEM_LIMIT_CAP = 60 * 1024 * 1024
VMEM_COMPILER_SLACK = 6 * 1024 * 1024

TN_IN = 512
TN_FF = 512
FF_UP_TM = 1024
IN_STAGE_ROWS = 64
BACKGROUND_DMA_PRIORITY = 1
BG_ROWS_2K = 256
BG_ROWS_UP = 128
ATT_BLK = 2 * CHUNK
HALO = SUBLANES
ROW_BLK = 256

_T_Q = (0, 2)
_T_KV = (2, 3)
_T_XIN = (3, 5)
_T_B = (5, 7)
_T_C = (7, 9)
_T_GA = (9, 13)
_T_GC = (13, 17)


def _vmem_limit(nbytes):
    return int(min(nbytes + VMEM_COMPILER_SLACK, VMEM_LIMIT_CAP))


def _params(n_axes, nbytes):
    return pltpu.CompilerParams(
        dimension_semantics=("arbitrary",) * n_axes,
        vmem_limit_bytes=_vmem_limit(nbytes),
    )


def _rms_norm(x, g):
    return x * lax.rsqrt(jnp.mean(x * x, axis=-1, keepdims=True) + EPS) * g


def _rope128(x, cos, sin):
    dh = lax.broadcasted_iota(jnp.int32, x.shape, 1) & (HEAD_DIM - 1)
    half = ROT_DIM // 2
    partner = jnp.where(dh < half, pltpu.roll(x, LANES - half, axis=1), pltpu.roll(x, half, axis=1))
    return x * cos + partner * sin


def _shifted_rows(a, prev1_row0, prev2_row0, prev2_row1, period):
    assert period & (period - 1) == 0
    rows = lax.broadcasted_iota(jnp.int32, (a.shape[0], 1), 0) & (period - 1)
    p1 = jnp.where(rows == 0, prev1_row0, pltpu.roll(a, 1, axis=0))
    p2 = jnp.where(rows == 0, prev2_row0, jnp.where(rows == 1, prev2_row1, pltpu.roll(a, 2, axis=0)))
    return p1, p2


def _stage_weight(w_hbm, w_sc, w_out, stage, in_sems, out_sem, step, n_steps):
    chunk = stage.shape[1]
    n_chunks = w_hbm.shape[0] // chunk

    def fetch(c, slot):
        return pltpu.make_async_copy(w_hbm.at[pl.ds(c * chunk, chunk), :], stage.at[slot], in_sems.at[slot])

    export = pltpu.make_async_copy(w_sc, w_out, out_sem)

    @pl.when(step == 0)
    def _():
        fetch(0, 0).start()

        def body(c, carry):
            slot = lax.rem(c, 2)

            @pl.when(c + 1 < n_chunks)
            def _():
                fetch(c + 1, 1 - slot).start()

            fetch(c, slot).wait()
            w_sc[pl.ds(pl.multiple_of(c * chunk, chunk), chunk), :] = stage[slot].astype(BF16)
            return carry

        lax.fori_loop(0, n_chunks, body, 0)
        export.start(priority=BACKGROUND_DMA_PRIORITY)

    @pl.when(step == n_steps - 1)
    def _():
        export.wait()


def _background_cast(jobs, buf_in, buf_out, sems, step, n_steps):
    chunk = buf_in.shape[0]
    first = 0
    for src, dst in jobs:
        n = src.shape[0] // chunk
        assert src.shape[0] % chunk == 0 and first + n + 2 <= n_steps

        def fetch(c, src=src):
            return pltpu.make_async_copy(src.at[pl.ds(c * chunk, chunk), :], buf_in, sems.at[0])

        def send(c, dst=dst):
            return pltpu.make_async_copy(buf_out, dst.at[pl.ds(c * chunk, chunk), :], sems.at[1])

        c_fetch = step - first
        c_cast = c_fetch - 1
        c_done = c_fetch - 2

        @pl.when((c_done >= 0) & (c_done < n))
        def _():
            send(c_done).wait()

        @pl.when((c_cast >= 0) & (c_cast < n))
        def _():
            fetch(c_cast).wait()
            buf_out[...] = buf_in[...].astype(BF16)
            send(c_cast).start(priority=BACKGROUND_DMA_PRIORITY)

        @pl.when((c_fetch >= 0) & (c_fetch < n))
        def _():
            fetch(c_fetch).start(priority=BACKGROUND_DMA_PRIORITY)

        first += n


def _background_scratch(chunk, cols):
    return [
        pltpu.VMEM((chunk, cols), F32),
        pltpu.VMEM((chunk, cols), BF16),
        pltpu.SemaphoreType.DMA((2,)),
    ]


def _background_bytes(chunk, cols):
    return chunk * cols * (4 + 2)


def _staging_scratch(w_shape, chunk):
    assert w_shape[0] % chunk == 0
    return [
        pltpu.VMEM(w_shape, BF16),
        pltpu.VMEM((2, chunk, w_shape[1]), F32),
        pltpu.SemaphoreType.DMA((2,)),
        pltpu.SemaphoreType.DMA(()),
    ]


def _staging_bytes(w_shape, chunk):
    return w_shape[0] * w_shape[1] * 2 + 2 * chunk * w_shape[1] * 4


def _inproj_body(*refs, staged, n_bg, n_steps):
    refs = list(refs)
    x_ref, g_ref, cos_ref, sin_ref, w_ref = refs[:5]
    bg_src = refs[5:5 + n_bg]
    del refs[:5 + n_bg]
    q_ref, k_ref, v_ref, u_ref, b_ref, ga_ref, gc_ref = refs[:7]
    del refs[:7]
    if staged:
        w_hbm, w_out = w_ref, refs.pop(0)
    bg_dst = refs[:n_bg]
    del refs[:n_bg]
    xn_sc = refs.pop(0)
    step = pl.program_id(0)
    if staged:
        w_ref, stage, in_sems, out_sem = refs[:4]
        del refs[:4]
        _stage_weight(w_hbm, w_ref, w_out, stage, in_sems, out_sem, step, n_steps)
    if n_bg:
        _background_cast(list(zip(bg_src, bg_dst)), *refs, step, n_steps)
    xn_sc[...] = _rms_norm(x_ref[...], g_ref[...]).astype(BF16)
    cos = cos_ref[...]
    sin = sin_ref[...]

    def col(t):
        return jnp.dot(xn_sc[...], w_ref[:, t * TN_IN:(t + 1) * TN_IN], preferred_element_type=F32)

    def blocks(n):
        return [slice(blk * LANES, (blk + 1) * LANES) for blk in range(n)]

    for t in range(*_T_Q):
        acc = col(t)
        base = (t - _T_Q[0]) * TN_IN
        for sl in blocks(TN_IN // LANES):
            q_ref[:, base + sl.start:base + sl.stop] = (
                _rope128(acc[:, sl], cos, sin) * (HEAD_DIM ** -0.5)).astype(BF16)
    acc = col(_T_KV[0])
    for sl in blocks(KV_W // LANES):
        k_ref[:, sl] = _rope128(acc[:, sl], cos, sin)
    v_ref[...] = acc[:, KV_W:2 * KV_W]
    for c in range(_T_C[1] - _T_C[0]):
        cols = slice(c * TN_IN, (c + 1) * TN_IN)
        u_ref[:, cols] = col(_T_C[0] + c) * col(_T_XIN[0] + c)
        b_ref[:, cols] = col(_T_B[0] + c).astype(BF16)
    for c in range(_T_GA[1] - _T_GA[0]):
        cols = slice(c * TN_IN, (c + 1) * TN_IN)
        ga_ref[:, cols] = col(_T_GA[0] + c).astype(BF16)
        gc_ref[:, cols] = col(_T_GC[0] + c).astype(BF16)


def _inproj(x, g, cos, sin, w, tm, background=()):
    t = x.shape[0]
    n_pos = cos.shape[0] // tm
    row = lambda i: (i, 0)
    fixed = lambda i: (0, 0)
    pos = lambda i: (i % n_pos, 0)
    widths = (Q_W, KV_W, KV_W, D_CONV, D_CONV, D_MODEL, D_MODEL)
    dtypes = (BF16, F32, F32, F32, BF16, BF16, BF16)
    out_row_bytes = sum(wd * jnp.dtype(dt).itemsize for wd, dt in zip(widths, dtypes))
    nbytes = (2 * tm * (D_MODEL * 4 + 2 * LANES * 4 + out_row_bytes)
              + tm * D_MODEL * 2 + 6 * tm * TN_IN * 4)
    out_specs = [pl.BlockSpec((tm, wd), row) for wd in widths]
    out_shape = [jax.ShapeDtypeStruct((t, wd), dt) for wd, dt in zip(widths, dtypes)]
    scratch = [pltpu.VMEM((tm, D_MODEL), BF16)]
    staged = w.dtype == F32
    if staged:
        w_spec = pl.BlockSpec(memory_space=pl.ANY)
        out_specs.append(pl.BlockSpec(memory_space=pl.ANY))
        out_shape.append(jax.ShapeDtypeStruct(w.shape, BF16))
        scratch += _staging_scratch(w.shape, IN_STAGE_ROWS)
        nbytes += _staging_bytes(w.shape, IN_STAGE_ROWS)
    else:
        w_spec = pl.BlockSpec((D_MODEL, D_IN), fixed, pipeline_mode=pl.Buffered(1))
        nbytes += D_MODEL * D_IN * 2
    if background:
        out_specs += [pl.BlockSpec(memory_space=pl.ANY)] * len(background)
        out_shape += [jax.ShapeDtypeStruct(bw.shape, BF16) for bw in background]
        scratch += _background_scratch(BG_ROWS_2K, D_MODEL)
        nbytes += _background_bytes(BG_ROWS_2K, D_MODEL)
    return pl.pallas_call(
        functools.partial(_inproj_body, staged=staged, n_bg=len(background), n_steps=t // tm),
        grid=(t // tm,),
        in_specs=[
            pl.BlockSpec((tm, D_MODEL), row),
            pl.BlockSpec((1, D_MODEL), fixed),
            pl.BlockSpec((tm, LANES), pos),
            pl.BlockSpec((tm, LANES), pos),
            w_spec,
        ] + [pl.BlockSpec(memory_space=pl.ANY)] * len(background),
        out_specs=tuple(out_specs),
        out_shape=tuple(out_shape),
        scratch_shapes=scratch,
        compiler_params=_params(1, nbytes),
        name="inproj",
    )(x, g, cos, sin, w, *background)


def _head_pair_operands(x256, h):
    blk = x256[:, (h // 2) * LANES:(h // 2 + 1) * LANES]
    lo_lanes = lax.broadcasted_iota(jnp.int32, blk.shape, 1) < HEAD_DIM
    if h % 2 == 0:
        lo = jnp.where(lo_lanes, blk, 0.0)
        hi = pltpu.roll(lo, HEAD_DIM, axis=1)
    else:
        hi = jnp.where(lo_lanes, 0.0, blk)
        lo = pltpu.roll(hi, HEAD_DIM, axis=1)
    return jnp.concatenate([lo, hi], axis=0).astype(BF16)


def _attend(q_ref, o_ref, sink_ref, blocks, s_sc, m_sc, p_sc, v_sc, after_scores=None):
    n_keys = blocks[0][1].shape[0]
    n_rows = s_sc.shape[1]
    key_row = lax.broadcasted_iota(jnp.int32, (2 * n_keys, LANES), 0)
    key_lane = lax.broadcasted_iota(jnp.int32, (2 * n_keys, LANES), 1)
    ones_cat = jnp.where((key_row >= n_keys) == (key_lane >= HEAD_DIM), 1.0, 0.0).astype(BF16)
    lo_lanes = lax.broadcasted_iota(jnp.int32, (n_rows, LANES), 1) < HEAD_DIM
    halves = [slice(hh * n_keys, (hh + 1) * n_keys) for hh in range(2)]

    tiles = []
    n_v = 0
    for q_rows, k_all, v_all, valid in blocks:
        for h in range(N_KV_HEADS):
            kcat = _head_pair_operands(k_all, h)
            v_sc[n_v] = jnp.concatenate([_head_pair_operands(v_all, h), ones_cat], axis=1)
            for p in range(2 * h, 2 * h + 2):
                lanes = slice(p * LANES, (p + 1) * LANES)
                sc = lax.dot_general(q_ref[q_rows, lanes], kcat, (((1,), (1,)), ((), ())),
                                     preferred_element_type=F32)
                s_sc[len(tiles)] = jnp.where(valid, sc, NEG)
                tiles.append((q_rows, lanes, p, n_v))
            n_v += 1
    if after_scores is not None:
        after_scores()
    for t, (_, _, p, _) in enumerate(tiles):
        for hh in range(2):
            m = jnp.maximum(jnp.max(s_sc[t, :, halves[hh]], axis=-1, keepdims=True), sink_ref[2 * p + hh])
            m_sc[2 * t + hh] = jnp.broadcast_to(m, (n_rows, LANES))
    for t in range(len(tiles)):
        p_sc[t] = jnp.concatenate(
            [jnp.exp(s_sc[t, :, halves[hh]] - jnp.tile(m_sc[2 * t + hh], (1, n_keys // LANES))).astype(BF16)
             for hh in range(2)], axis=1)
    for t, (q_rows, lanes, p, vi) in enumerate(tiles):
        e = p_sc[t]
        num_den = jnp.dot(e, v_sc[vi], preferred_element_type=F32)
        num = num_den[:, :LANES]
        den = num_den[:, LANES:]
        sink_term = jnp.where(lo_lanes, jnp.exp(sink_ref[2 * p] - m_sc[2 * t]),
                              jnp.exp(sink_ref[2 * p + 1] - m_sc[2 * t + 1]))
        o_ref[q_rows, lanes] = (num / (den + sink_term)).astype(BF16)


def _attn_scratch(n_blocks, n_rows, n_keys):
    n_tiles = n_blocks * N_HEADS // 2
    return [
        pltpu.VMEM((n_tiles, n_rows, 2 * n_keys), F32),
        pltpu.VMEM((2 * n_tiles, n_rows, LANES), F32),
        pltpu.VMEM((n_tiles, n_rows, 2 * n_keys), BF16),
        pltpu.VMEM((n_blocks * N_KV_HEADS, 2 * n_keys, 2 * LANES), BF16),
    ]


def _attn_scratch_bytes(n_blocks, n_rows, n_keys):
    n_tiles = n_blocks * N_HEADS // 2
    return (n_tiles * n_rows * 2 * n_keys * 6 + 2 * n_tiles * n_rows * LANES * 4
            + n_blocks * N_KV_HEADS * 2 * n_keys * 2 * LANES * 2)


def _prompt_attention(first, sink_ref, q_ref, kc_ref, kp_ref, vc_ref, vp_ref, o_ref, scratch, after_scores=None):
    n_sub = q_ref.shape[0] // ATT_BLK
    n_keys = 2 * ATT_BLK
    shape = (ATT_BLK, 2 * n_keys)
    q_chunk = lax.broadcasted_iota(jnp.int32, shape, 0) // CHUNK
    k_chunk = (lax.broadcasted_iota(jnp.int32, shape, 1) & (n_keys - 1)) // CHUNK
    band = (k_chunk >= q_chunk) & (k_chunk <= q_chunk + WINDOW // CHUNK)
    first_valid = band & (k_chunk >= jnp.where(first, ATT_BLK // CHUNK, 0))
    blocks = []
    for s in range(n_sub):
        rows = slice(s * ATT_BLK, (s + 1) * ATT_BLK)
        if s == 0:
            k_prev, v_prev, valid = kp_ref[...], vp_ref[...], first_valid
        else:
            prev = slice((s - 1) * ATT_BLK, s * ATT_BLK)
            k_prev, v_prev, valid = kc_ref[prev, :], vc_ref[prev, :], band
        k_all = jnp.concatenate([k_prev, kc_ref[rows, :]], axis=0)
        v_all = jnp.concatenate([v_prev, vc_ref[rows, :]], axis=0)
        blocks.append((rows, k_all, v_all, valid))
    _attend(q_ref, o_ref, sink_ref, blocks, *scratch, after_scores=after_scores)


def _attn_sample_body(sink_ref, q_ref, k_ref, v_ref, o_ref, *scratch, n_valid, n_keys, rows_per_seq):
    shape = (rows_per_seq, 2 * n_keys)
    valid = (lax.broadcasted_iota(jnp.int32, shape, 1) & (n_keys - 1)) < n_valid
    blocks = []
    for b in range(q_ref.shape[0] // rows_per_seq):
        keys = slice(b * n_keys, (b + 1) * n_keys)
        blocks.append((slice(b * rows_per_seq, (b + 1) * rows_per_seq), k_ref[keys, :], v_ref[keys, :], valid))
    _attend(q_ref, o_ref, sink_ref, blocks, *scratch)


def _attn_sample(sinks, q, k_all, v_all, n_valid, n_keys, rows_per_seq, seqs_per_step):
    t = q.shape[0]
    tq = rows_per_seq * seqs_per_step
    tk = n_keys * seqs_per_step
    nbytes = 2 * (2 * tq * Q_W * 2 + 2 * tk * KV_W * 4) + _attn_scratch_bytes(seqs_per_step, rows_per_seq, n_keys)
    return pl.pallas_call(
        functools.partial(_attn_sample_body, n_valid=n_valid, n_keys=n_keys, rows_per_seq=rows_per_seq),
        grid=(t // tq,),
        in_specs=[
            pl.BlockSpec(memory_space=pltpu.SMEM),
            pl.BlockSpec((tq, Q_W), lambda i: (i, 0)),
            pl.BlockSpec((tk, KV_W), lambda i: (i, 0)),
            pl.BlockSpec((tk, KV_W), lambda i: (i, 0)),
        ],
        out_specs=pl.BlockSpec((tq, Q_W), lambda i: (i, 0)),
        out_shape=jax.ShapeDtypeStruct((t, Q_W), BF16),
        scratch_shapes=_attn_scratch(seqs_per_step, rows_per_seq, n_keys),
        compiler_params=_params(1, nbytes),
        name="attn_sample",
    )(sinks, q, k_all, v_all)


def _mix_body(*refs, tiles_per_seq, hist_period, fused_attn, n_bg, n_steps):
    refs = list(refs)
    if fused_attn:
        sink_ref, x_ref, q_ref, kc_ref, kp_ref, vc_ref, vp_ref = refs[:7]
        del refs[:7]
    else:
        x_ref, oa_ref = refs[:2]
        del refs[:2]
    u_ref, b_ref, ga_ref, gc_ref = refs[:4]
    del refs[:4]
    if hist_period:
        p1_ref, p2_ref = refs[:2]
        del refs[:2]
    cw_ref, wao_ref, wco_ref, wmo_ref, g_ref = refs[:5]
    bg_src = refs[5:5 + n_bg]
    del refs[:5 + n_bg]
    h_ref = refs.pop(0)
    bg_dst = refs[:n_bg]
    del refs[:n_bg]
    if not hist_period:
        carry_sc = refs.pop(0)
    if fused_attn:
        oa_ref = refs.pop(0)
        attn_scratch = refs[:4]
        del refs[:4]
    if n_bg:
        _background_cast(list(zip(bg_src, bg_dst)), *refs, pl.program_id(0), n_steps)
    u = u_ref[...]
    tm = u.shape[0]
    if hist_period:
        p1, p2 = _shifted_rows(u, p1_ref[...], p2_ref[...], p2_ref[...], hist_period)
    else:
        @pl.when(pl.program_id(0) % tiles_per_seq == 0)
        def _():
            carry_sc[...] = jnp.zeros_like(carry_sc)

        last1 = carry_sc[HALO - 1:HALO, :]
        last2 = carry_sc[HALO - 2:HALO - 1, :]
        p1, p2 = _shifted_rows(u, last1, last2, last1, tm)
        carry_sc[...] = u[tm - HALO:tm, :]
    cy = cw_ref[0:1, :] * p2 + cw_ref[1:2, :] * p1 + cw_ref[2:3, :] * u
    y_conv = (b_ref[...].astype(F32) * cy).astype(BF16)
    if fused_attn:
        conv_branch = []
        first = (pl.program_id(0) % tiles_per_seq) == 0
        _prompt_attention(
            first, sink_ref, q_ref, kc_ref, kp_ref, vc_ref, vp_ref, oa_ref, attn_scratch,
            after_scores=lambda: conv_branch.append(jnp.dot(y_conv, wco_ref[...], preferred_element_type=F32)))
        conv_o = conv_branch[0]
    else:
        conv_o = jnp.dot(y_conv, wco_ref[...], preferred_element_type=F32)
    attn_o = jnp.dot(oa_ref[...], wao_ref[...], preferred_element_type=F32)
    mixed = (jax.nn.sigmoid(ga_ref[...].astype(F32)) * attn_o
             + jax.nn.sigmoid(gc_ref[...].astype(F32)) * conv_o)
    mo = jnp.dot(mixed.astype(BF16), wmo_ref[...], preferred_element_type=F32)
    h_ref[...] = x_ref[...] + _rms_norm(mo, g_ref[...])


def _mix(x, oa, u, b, ga, gc, hist, cw, wao, wco, wmo, g, seq, tm, background=(), attn=None):
    t = x.shape[0]
    row = lambda i: (i, 0)
    fixed = lambda i: (0, 0)
    scratch = []
    if attn is None:
        act_specs = [pl.BlockSpec((tm, D_MODEL), row), pl.BlockSpec((tm, Q_W), row)]
        acts = [x, oa]
    else:
        sinks, q, k, v = attn
        ratio = tm // ATT_BLK
        prev = lambda i: (jnp.maximum(i * ratio - 1, 0), 0)
        act_specs = [
            pl.BlockSpec(memory_space=pltpu.SMEM),
            pl.BlockSpec((tm, D_MODEL), row),
            pl.BlockSpec((tm, Q_W), row),
            pl.BlockSpec((tm, KV_W), row),
            pl.BlockSpec((ATT_BLK, KV_W), prev),
            pl.BlockSpec((tm, KV_W), row),
            pl.BlockSpec((ATT_BLK, KV_W), prev),
        ]
        acts = [sinks, x, q, k, k, v, v]
    act_specs += [
        pl.BlockSpec((tm, D_CONV), row),
        pl.BlockSpec((tm, D_CONV), row),
        pl.BlockSpec((tm, D_MODEL), row),
        pl.BlockSpec((tm, D_MODEL), row),
    ]
    acts += [u, b, ga, gc]
    if hist is None:
        hist_period = 0
        scratch.append(pltpu.VMEM((HALO, D_CONV), F32))
    else:
        hist_period = seq
        act_specs += [pl.BlockSpec((tm, D_CONV), row)] * 2
        acts += list(hist)
    attn_bytes = 0
    if attn is not None:
        scratch.append(pltpu.VMEM((tm, Q_W), BF16))
        scratch += _attn_scratch(tm // ATT_BLK, ATT_BLK, 2 * ATT_BLK)
        attn_bytes = (tm * Q_W * 2 + 4 * (tm + ATT_BLK) * KV_W * 4
                      + _attn_scratch_bytes(tm // ATT_BLK, ATT_BLK, 2 * ATT_BLK))
    weights = (wao, wco, wmo)
    out_specs = [pl.BlockSpec((tm, D_MODEL), row)]
    out_shape = [jax.ShapeDtypeStruct((t, D_MODEL), F32)]
    row_bytes = D_MODEL * 4 * 2 + Q_W * 2 + D_CONV * (4 + 2) + D_MODEL * 2 * 2
    if hist is not None:
        row_bytes += 2 * D_CONV * 4
    nbytes = (2 * tm * row_bytes + (Q_W + D_CONV + D_MODEL) * D_MODEL * 2 + 6 * tm * D_MODEL * 4
              + attn_bytes)
    if background:
        cols = background[0].shape[1]
        out_specs += [pl.BlockSpec(memory_space=pl.ANY)] * len(background)
        out_shape += [jax.ShapeDtypeStruct(bw.shape, BF16) for bw in background]
        scratch += _background_scratch(BG_ROWS_UP, cols)
        nbytes += _background_bytes(BG_ROWS_UP, cols)
    in_specs = (act_specs + [pl.BlockSpec((CONV_W, D_CONV), fixed)]
                + [pl.BlockSpec(w.shape, fixed, pipeline_mode=pl.Buffered(1)) for w in weights]
                + [pl.BlockSpec((1, D_MODEL), fixed)] + [pl.BlockSpec(memory_space=pl.ANY)] * len(background))
    return pl.pallas_call(
        functools.partial(_mix_body, tiles_per_seq=max(seq // tm, 1), hist_period=hist_period,
                          fused_attn=attn is not None, n_bg=len(background), n_steps=t // tm),
        grid=(t // tm,),
        in_specs=in_specs,
        out_specs=tuple(out_specs),
        out_shape=tuple(out_shape),
        scratch_shapes=scratch,
        compiler_params=_params(1, nbytes),
        name="mix",
    )(*acts, cw, wao, wco, wmo, g, *background)


def _ffn_up_body(*refs, tiles_per_seq, hist_period, n_bg, n_steps):
    refs = list(refs)
    h_ref, g_ref, wa_ref, wg_ref, cw_ref = refs[:5]
    del refs[:5]
    if hist_period:
        p1_ref, p2_ref = refs[:2]
        del refs[:2]
    bg_src = refs[:n_bg]
    del refs[:n_bg]
    f_ref, aux_ref = refs[:2]
    bg_dst = refs[2:2 + n_bg]
    del refs[:2 + n_bg]
    xn_sc = refs.pop(0)
    if hist_period:
        a_ref = aux_ref
    else:
        tail_ref, carry_sc = aux_ref, refs.pop(0)
    i = pl.program_id(0)
    j = pl.program_id(1)
    if n_bg:
        _background_cast(list(zip(bg_src, bg_dst)), *refs, i * pl.num_programs(1) + j, n_steps)

    @pl.when(j == 0)
    def _():
        xn_sc[...] = _rms_norm(h_ref[...], g_ref[...]).astype(BF16)

    if not hist_period:
        @pl.when(i % tiles_per_seq == 0)
        def _():
            carry_sc[j] = jnp.zeros(carry_sc.shape[1:], F32)

        tail = carry_sc[j]
    tm = xn_sc.shape[0]
    rb = min(tm, ROW_BLK)
    for r in range(tm // rb):
        rows = slice(r * rb, (r + 1) * rb)
        xn = xn_sc[rows, :]
        a = jnp.dot(xn, wa_ref[...], preferred_element_type=F32)
        gate = jnp.dot(xn, wg_ref[...], preferred_element_type=F32)
        if hist_period:
            p1, p2 = _shifted_rows(a, p1_ref[rows, :], p2_ref[rows, :], p2_ref[rows, :], hist_period)
            a_ref[rows, :] = a
        else:
            last1 = tail[HALO - 1:HALO, :]
            p1, p2 = _shifted_rows(a, last1, tail[HALO - 2:HALO - 1, :], last1, rb)
            tail = a[rb - HALO:rb, :]
        ua = cw_ref[0:1, :] * p2 + cw_ref[1:2, :] * p1 + cw_ref[2:3, :] * a
        f_ref[rows, :] = (jax.nn.gelu(ua, approximate=True) * gate).astype(BF16)
    if not hist_period:
        carry_sc[j] = tail
        tail_ref[...] = tail


def _ffn_up(h, g, w_up, cw, hist, seq, tm, background=()):
    t = h.shape[0]
    n_j = D_FF // TN_FF
    grid = (t // tm, n_j)
    in_specs = [
        pl.BlockSpec((tm, D_MODEL), lambda i, j: (i, 0)),
        pl.BlockSpec((1, D_MODEL), lambda i, j: (0, 0)),
        pl.BlockSpec((D_MODEL, TN_FF), lambda i, j: (0, j)),
        pl.BlockSpec((D_MODEL, TN_FF), lambda i, j: (0, j + n_j)),
        pl.BlockSpec((CONV_W, TN_FF), lambda i, j: (0, j)),
    ]
    args = [h, g, w_up, w_up, cw]
    scratch = [pltpu.VMEM((tm, D_MODEL), BF16)]
    f_spec = pl.BlockSpec((tm, TN_FF), lambda i, j: (i, j))
    f_shape = jax.ShapeDtypeStruct((t, D_FF), BF16)
    if hist is None:
        hist_period = 0
        scratch.append(pltpu.VMEM((n_j, HALO, TN_FF), F32))
        out_specs = (f_spec, pl.BlockSpec((None, HALO, TN_FF), lambda i, j: (i, 0, j)))
        out_shape = (f_shape, jax.ShapeDtypeStruct((t // tm, HALO, D_FF), F32))
    else:
        hist_period = seq
        in_specs += [pl.BlockSpec((tm, TN_FF), lambda i, j: (i, j))] * 2
        args += list(hist)
        out_specs = (f_spec, pl.BlockSpec((tm, TN_FF), lambda i, j: (i, j)))
        out_shape = (f_shape, jax.ShapeDtypeStruct((t, D_FF), F32))
    nbytes = (2 * tm * D_MODEL * 4 + tm * D_MODEL * 2 + 4 * D_MODEL * TN_FF * 2
              + 2 * tm * TN_FF * (2 + 4 * 3) + 8 * tm * TN_FF * 4)
    if background:
        in_specs += [pl.BlockSpec(memory_space=pl.ANY)] * len(background)
        args += list(background)
        out_specs += (pl.BlockSpec(memory_space=pl.ANY),) * len(background)
        out_shape += tuple(jax.ShapeDtypeStruct(bw.shape, BF16) for bw in background)
        scratch += _background_scratch(BG_ROWS_2K, D_MODEL)
        nbytes += _background_bytes(BG_ROWS_2K, D_MODEL)
    return pl.pallas_call(
        functools.partial(_ffn_up_body, tiles_per_seq=max(seq // tm, 1), hist_period=hist_period,
                          n_bg=len(background), n_steps=grid[0] * grid[1]),
        grid=grid,
        in_specs=in_specs,
        out_specs=out_specs,
        out_shape=out_shape,
        scratch_shapes=scratch,
        compiler_params=_params(2, nbytes),
        name="ffn_up",
    )(*args)


def _ffn_down_body(f_ref, w_ref, h_ref, g_ref, o_ref):
    tm = f_ref.shape[0]
    rb = min(tm, ROW_BLK)
    for r in range(tm // rb):
        rows = slice(r * rb, (r + 1) * rb)
        acc = jnp.dot(f_ref[rows, :], w_ref[...], preferred_element_type=F32)
        o_ref[rows, :] = h_ref[rows, :] + _rms_norm(acc, g_ref[...])


def _ffn_down(f, w, h, g, tm):
    t = f.shape[0]
    nbytes = (2 * tm * (D_FF * 2 + D_MODEL * 4 * 2) + D_FF * D_MODEL * 2
              + 2 * min(tm, ROW_BLK) * D_MODEL * 4)
    return pl.pallas_call(
        _ffn_down_body,
        grid=(t // tm,),
        in_specs=[
            pl.BlockSpec((tm, D_FF), lambda i: (i, 0)),
            pl.BlockSpec((D_FF, D_MODEL), lambda i: (0, 0), pipeline_mode=pl.Buffered(1)),
            pl.BlockSpec((tm, D_MODEL), lambda i: (i, 0)),
            pl.BlockSpec((1, D_MODEL), lambda i: (0, 0)),
        ],
        out_specs=pl.BlockSpec((tm, D_MODEL), lambda i: (i, 0)),
        out_shape=jax.ShapeDtypeStruct((t, D_MODEL), F32),
        compiler_params=_params(1, nbytes),
        name="ffn_down",
    )(f, w, h, g)


def _rope_tables(pos):
    half = ROT_DIM // 2
    inv = jnp.power(jnp.float32(ROPE_THETA), -jnp.arange(half, dtype=F32) * (2.0 / ROT_DIM))
    ang = pos.astype(F32)[:, None] * inv[None, :]
    cos = jnp.cos(ang)
    sin = jnp.sin(ang)
    rest = HEAD_DIM - ROT_DIM
    cos_h = jnp.concatenate([cos, cos, jnp.ones((pos.shape[0], rest), F32)], axis=1)
    sin_h = jnp.concatenate([-sin, sin, jnp.zeros((pos.shape[0], rest), F32)], axis=1)
    reps = LANES // HEAD_DIM
    return jnp.tile(cos_h, (1, reps)), jnp.tile(sin_h, (1, reps))


def _row_history(state, seq):
    b, _, c = state.shape
    p1 = jnp.concatenate([state[:, 1:2], jnp.zeros((b, seq - 1, c), F32)], axis=1)
    p2 = jnp.concatenate([state[:, 0:1], state[:, 1:2], jnp.zeros((b, seq - 2, c), F32)], axis=1)
    return p1.reshape(b * seq, c), p2.reshape(b * seq, c)


def kernel(x_prompt, x_sample, cache_k, cache_v, state_conv_mix, state_conv_ffn, norm_mix_pre, w_in,
           attn_sinks, conv_mix_w, w_attn_o, w_conv_o, w_mix_out, norm_mix_post, norm_ffn_pre,
           w_ffn_up, conv_ffn_w, w_ffn_down, norm_ffn_post):
    assert w_in.shape == (1, D_MODEL, D_IN), "one layer only"
    batch, seq, _ = x_prompt.shape
    dec_batch, dec_seq, _ = x_sample.shape
    assert seq % 1024 == 0 and (dec_batch * dec_seq) % 256 == 0 and dec_seq % 16 == 0
    assert cache_k.shape[2] == WINDOW

    g_pre = norm_mix_pre[0][None]
    g_post = norm_mix_post[0][None]
    g_ffn_pre = norm_ffn_pre[0][None]
    g_ffn_post = norm_ffn_post[0][None]
    sinks = attn_sinks[0]
    cw_mix = conv_mix_w[0]
    cw_ffn = conv_ffn_w[0]

    tp = batch * seq
    xp = x_prompt.reshape(tp, D_MODEL)
    cos_p, sin_p = _rope_tables(jnp.arange(seq, dtype=F32))
    q, k, v, u, b, ga, gc, w_in_b, wao, wco, wmo = _inproj(
        xp, g_pre, cos_p, sin_p, w_in[0], tm=256, background=(w_attn_o[0], w_conv_o[0], w_mix_out[0]))
    hp, w_up = _mix(xp, None, u, b, ga, gc, None, cw_mix, wao, wco, wmo, g_post, seq, tm=256,
                    background=(w_ffn_up[0],), attn=(sinks, q, k, v))
    f, a_tail, w_down = _ffn_up(hp, g_ffn_pre, w_up, cw_ffn, None, seq, tm=FF_UP_TM,
                                background=(w_ffn_down[0],))
    yp = _ffn_down(f, w_down, hp, g_ffn_post, tm=512)

    keep = min(WINDOW, seq)
    new_k_prompt = k.reshape(batch, seq, KV_W)[:, seq - keep:].reshape(1, batch, keep, N_KV_HEADS, HEAD_DIM)
    new_v_prompt = v.reshape(batch, seq, KV_W)[:, seq - keep:].reshape(1, batch, keep, N_KV_HEADS, HEAD_DIM)
    conv_mix_prompt = u.reshape(batch, seq, D_CONV)[:, seq - (CONV_W - 1):][None]
    a_tail = a_tail.reshape(batch, seq // FF_UP_TM, HALO, D_FF)
    conv_ffn_prompt = a_tail[:, -1, HALO - (CONV_W - 1):][None]

    ts = dec_batch * dec_seq
    xs = x_sample.reshape(ts, D_MODEL)
    pos_s = PAST_LEN + jnp.arange(dec_seq, dtype=F32)
    cos_s, sin_s = _rope_tables(jnp.tile(pos_s, dec_batch))
    qs, ks, vs, us, bs, gas, gcs = _inproj(xs, g_pre, cos_s, sin_s, w_in_b, tm=ts)
    n_valid = WINDOW + dec_seq
    n_keys = 2 * WINDOW
    pad = jnp.zeros((dec_batch, n_keys - n_valid, KV_W), F32)
    k_all = jnp.concatenate([cache_k[0].reshape(dec_batch, WINDOW, KV_W),
                             ks.reshape(dec_batch, dec_seq, KV_W), pad], axis=1)
    v_all = jnp.concatenate([cache_v[0].reshape(dec_batch, WINDOW, KV_W),
                             vs.reshape(dec_batch, dec_seq, KV_W), pad], axis=1)
    oas = _attn_sample(sinks, qs, k_all.reshape(dec_batch * n_keys, KV_W),
                       v_all.reshape(dec_batch * n_keys, KV_W), n_valid, n_keys, dec_seq, seqs_per_step=4)
    (hs,) = _mix(xs, oas, us, bs, gas, gcs, _row_history(state_conv_mix[0], dec_seq),
                 cw_mix, wao, wco, wmo, g_post, dec_seq, tm=ts)
    fs, a_s = _ffn_up(hs, g_ffn_pre, w_up, cw_ffn, _row_history(state_conv_ffn[0], dec_seq), dec_seq, tm=ts)
    ys = _ffn_down(fs, w_down, hs, g_ffn_post, tm=ts)

    new_k_sample = ks.reshape(1, dec_batch, dec_seq, N_KV_HEADS, HEAD_DIM)
    new_v_sample = vs.reshape(1, dec_batch, dec_seq, N_KV_HEADS, HEAD_DIM)
    conv_mix_sample = us.reshape(dec_batch, dec_seq, D_CONV)[:, dec_seq - (CONV_W - 1):][None]
    conv_ffn_sample = a_s.reshape(dec_batch, dec_seq, D_FF)[:, dec_seq - (CONV_W - 1):][None]

    return (yp.reshape(batch, seq, D_MODEL), ys.reshape(dec_batch, dec_seq, D_MODEL),
            new_k_prompt, new_v_prompt, conv_mix_prompt, conv_ffn_prompt,
            new_k_sample, new_v_sample, conv_mix_sample, conv_ffn_sample)
```

```python
import functools

import jax
import jax.numpy as jnp
from jax import lax
from jax.experimental import pallas as pl
from jax.experimental.pallas import tpu as pltpu

F32 = jnp.float32
BF16 = jnp.bfloat16

D_MODEL = 2048
CHUNK = 64
N_HEADS = 16
N_KV_HEADS = 4
GROUP = N_HEADS // N_KV_HEADS
HEAD_DIM = 64
ROT_DIM = HEAD_DIM // 4
ROPE_THETA = 500000.0
WINDOW = 128
D_CONV = 1024
CONV_W = 3
D_FF = 5632
EPS = 1e-6
NEG = -1e30
PAST_LEN = 4096
Q_W = N_HEADS * HEAD_DIM
KV_W = N_KV_HEADS * HEAD_DIM
D_IN = Q_W + 2 * KV_W + 3 * D_CONV + 2 * D_MODEL

LANES = 128
SUBLANES = 8
VMEM_LIMIT_CAP = 60 * 1024 * 1024
VMEM_COMPILER_SLACK = 6 * 1024 * 1024

TN_IN = 512
TN_FF = 512
FF_UP_TM = 1024
IN_STAGE_ROWS = 64
BG_ROWS_2K = 256
BG_ROWS_UP = 128
ATT_BLK = 2 * CHUNK
HALO = SUBLANES
ROW_BLK = 256

_T_Q = (0, 2)
_T_KV = (2, 3)
_T_XIN = (3, 5)
_T_B = (5, 7)
_T_C = (7, 9)
_T_GA = (9, 13)
_T_GC = (13, 17)


def _vmem_limit(nbytes):
    return int(min(nbytes + VMEM_COMPILER_SLACK, VMEM_LIMIT_CAP))


def _params(n_axes, nbytes):
    return pltpu.CompilerParams(
        dimension_semantics=("arbitrary",) * n_axes,
        vmem_limit_bytes=_vmem_limit(nbytes),
    )


def _rms_norm(x, g):
    return x * lax.rsqrt(jnp.mean(x * x, axis=-1, keepdims=True) + EPS) * g


def _rope128(x, cos, sin):
    dh = lax.broadcasted_iota(jnp.int32, x.shape, 1) & (HEAD_DIM - 1)
    half = ROT_DIM // 2
    partner = jnp.where(dh < half, pltpu.roll(x, LANES - half, axis=1), pltpu.roll(x, half, axis=1))
    return x * cos + partner * sin


def _shifted_rows(a, prev1_row0, prev2_row0, prev2_row1, period):
    assert period & (period - 1) == 0
    rows = lax.broadcasted_iota(jnp.int32, (a.shape[0], 1), 0) & (period - 1)
    p1 = jnp.where(rows == 0, prev1_row0, pltpu.roll(a, 1, axis=0))
    p2 = jnp.where(rows == 0, prev2_row0, jnp.where(rows == 1, prev2_row1, pltpu.roll(a, 2, axis=0)))
    return p1, p2


def _stage_weight(w_hbm, w_sc, w_out, stage, in_sems, out_sem, step, n_steps):
    chunk = stage.shape[1]
    n_chunks = w_hbm.shape[0] // chunk

    def fetch(c, slot):
        return pltpu.make_async_copy(w_hbm.at[pl.ds(c * chunk, chunk), :], stage.at[slot], in_sems.at[slot])

    export = pltpu.make_async_copy(w_sc, w_out, out_sem)

    @pl.when(step == 0)
    def _():
        fetch(0, 0).start()

        def body(c, carry):
            slot = lax.rem(c, 2)

            @pl.when(c + 1 < n_chunks)
            def _():
                fetch(c + 1, 1 - slot).start()

            fetch(c, slot).wait()
            w_sc[pl.ds(pl.multiple_of(c * chunk, chunk), chunk), :] = stage[slot].astype(BF16)
            return carry

        lax.fori_loop(0, n_chunks, body, 0)
        export.start()

    @pl.when(step == n_steps - 1)
    def _():
        export.wait()


def _background_cast(jobs, buf_in, buf_out, sems, step, n_steps):
    chunk = buf_in.shape[0]
    assert all(src.shape[0] % chunk == 0 for src, _ in jobs)
    total = sum(src.shape[0] // chunk for src, _ in jobs)
    assert total + 2 <= n_steps

    @pl.when(step < total + 2)
    def _():
        first = 0
        for src, dst in jobs:
            n = src.shape[0] // chunk

            def fetch(c, src=src):
                return pltpu.make_async_copy(src.at[pl.ds(c * chunk, chunk), :], buf_in, sems.at[0])

            def send(c, dst=dst):
                return pltpu.make_async_copy(buf_out, dst.at[pl.ds(c * chunk, chunk), :], sems.at[1])

            c_fetch = step - first
            c_cast = c_fetch - 1
            c_done = c_fetch - 2

            @pl.when((c_done >= 0) & (c_done < n))
            def _():
                send(c_done).wait()

            @pl.when((c_cast >= 0) & (c_cast < n))
            def _():
                fetch(c_cast).wait()
                buf_out[...] = buf_in[...].astype(BF16)
                send(c_cast).start()

            @pl.when((c_fetch >= 0) & (c_fetch < n))
            def _():
                fetch(c_fetch).start()

            first += n


def _background_scratch(chunk, cols):
    return [
        pltpu.VMEM((chunk, cols), F32),
        pltpu.VMEM((chunk, cols), BF16),
        pltpu.SemaphoreType.DMA((2,)),
    ]


def _background_bytes(chunk, cols):
    return chunk * cols * (4 + 2)


def _staging_scratch(w_shape, chunk):
    assert w_shape[0] % chunk == 0
    return [
        pltpu.VMEM(w_shape, BF16),
        pltpu.VMEM((2, chunk, w_shape[1]), F32),
        pltpu.SemaphoreType.DMA((2,)),
        pltpu.SemaphoreType.DMA(()),
    ]


def _staging_bytes(w_shape, chunk):
    return w_shape[0] * w_shape[1] * 2 + 2 * chunk * w_shape[1] * 4


def _inproj_body(*refs, staged, n_bg, n_steps):
    refs = list(refs)
    x_ref, g_ref, cos_ref, sin_ref, w_ref = refs[:5]
    bg_src = refs[5:5 + n_bg]
    del refs[:5 + n_bg]
    q_ref, k_ref, v_ref, u_ref, b_ref, ga_ref, gc_ref = refs[:7]
    del refs[:7]
    if staged:
        w_hbm, w_out = w_ref, refs.pop(0)
    bg_dst = refs[:n_bg]
    del refs[:n_bg]
    xn_sc = refs.pop(0)
    step = pl.program_id(0)
    if staged:
        w_ref, stage, in_sems, out_sem = refs[:4]
        del refs[:4]
        _stage_weight(w_hbm, w_ref, w_out, stage, in_sems, out_sem, step, n_steps)
    if n_bg:
        _background_cast(list(zip(bg_src, bg_dst)), *refs, step, n_steps)
    xn_sc[...] = _rms_norm(x_ref[...], g_ref[...]).astype(BF16)
    cos = cos_ref[...]
    sin = sin_ref[...]

    def col(t):
        return jnp.dot(xn_sc[...], w_ref[:, t * TN_IN:(t + 1) * TN_IN], preferred_element_type=F32)

    def blocks(n):
        return [slice(blk * LANES, (blk + 1) * LANES) for blk in range(n)]

    for t in range(*_T_Q):
        acc = col(t)
        base = (t - _T_Q[0]) * TN_IN
        for sl in blocks(TN_IN // LANES):
            q_ref[:, base + sl.start:base + sl.stop] = (
                _rope128(acc[:, sl], cos, sin) * (HEAD_DIM ** -0.5)).astype(BF16)
    acc = col(_T_KV[0])
    for sl in blocks(KV_W // LANES):
        k_ref[:, sl] = _rope128(acc[:, sl], cos, sin)
    v_ref[...] = acc[:, KV_W:2 * KV_W]
    for c in range(_T_C[1] - _T_C[0]):
        cols = slice(c * TN_IN, (c + 1) * TN_IN)
        u_ref[:, cols] = col(_T_C[0] + c) * col(_T_XIN[0] + c)
        b_ref[:, cols] = col(_T_B[0] + c).astype(BF16)
    for c in range(_T_GA[1] - _T_GA[0]):
        cols = slice(c * TN_IN, (c + 1) * TN_IN)
        ga_ref[:, cols] = col(_T_GA[0] + c).astype(BF16)
        gc_ref[:, cols] = col(_T_GC[0] + c).astype(BF16)


def _inproj(x, g, cos, sin, w, tm, background=()):
    t = x.shape[0]
    n_pos = cos.shape[0] // tm
    row = lambda i: (i, 0)
    fixed = lambda i: (0, 0)
    pos = lambda i: (i % n_pos, 0)
    widths = (Q_W, KV_W, KV_W, D_CONV, D_CONV, D_MODEL, D_MODEL)
    dtypes = (BF16, F32, F32, F32, BF16, BF16, BF16)
    out_row_bytes = sum(wd * jnp.dtype(dt).itemsize for wd, dt in zip(widths, dtypes))
    nbytes = (2 * tm * (D_MODEL * 4 + 2 * LANES * 4 + out_row_bytes)
              + tm * D_MODEL * 2 + 6 * tm * TN_IN * 4)
    out_specs = [pl.BlockSpec((tm, wd), row) for wd in widths]
    out_shape = [jax.ShapeDtypeStruct((t, wd), dt) for wd, dt in zip(widths, dtypes)]
    scratch = [pltpu.VMEM((tm, D_MODEL), BF16)]
    staged = w.dtype == F32
    if staged:
        w_spec = pl.BlockSpec(memory_space=pl.ANY)
        out_specs.append(pl.BlockSpec(memory_space=pl.ANY))
        out_shape.append(jax.ShapeDtypeStruct(w.shape, BF16))
        scratch += _staging_scratch(w.shape, IN_STAGE_ROWS)
        nbytes += _staging_bytes(w.shape, IN_STAGE_ROWS)
    else:
        w_spec = pl.BlockSpec((D_MODEL, D_IN), fixed, pipeline_mode=pl.Buffered(1))
        nbytes += D_MODEL * D_IN * 2
    if background:
        out_specs += [pl.BlockSpec(memory_space=pl.ANY)] * len(background)
        out_shape += [jax.ShapeDtypeStruct(bw.shape, BF16) for bw in background]
        scratch += _background_scratch(BG_ROWS_2K, D_MODEL)
        nbytes += _background_bytes(BG_ROWS_2K, D_MODEL)
    return pl.pallas_call(
        functools.partial(_inproj_body, staged=staged, n_bg=len(background), n_steps=t // tm),
        grid=(t // tm,),
        in_specs=[
            pl.BlockSpec((tm, D_MODEL), row),
            pl.BlockSpec((1, D_MODEL), fixed),
            pl.BlockSpec((tm, LANES), pos),
            pl.BlockSpec((tm, LANES), pos),
            w_spec,
        ] + [pl.BlockSpec(memory_space=pl.ANY)] * len(background),
        out_specs=tuple(out_specs),
        out_shape=tuple(out_shape),
        scratch_shapes=scratch,
        compiler_params=_params(1, nbytes),
        name="inproj",
    )(x, g, cos, sin, w, *background)


def _head_pair_operands(x256, h):
    blk = x256[:, (h // 2) * LANES:(h // 2 + 1) * LANES]
    lo_lanes = lax.broadcasted_iota(jnp.int32, blk.shape, 1) < HEAD_DIM
    if h % 2 == 0:
        lo = jnp.where(lo_lanes, blk, 0.0)
        hi = pltpu.roll(lo, HEAD_DIM, axis=1)
    else:
        hi = jnp.where(lo_lanes, 0.0, blk)
        lo = pltpu.roll(hi, HEAD_DIM, axis=1)
    return jnp.concatenate([lo, hi], axis=0).astype(BF16)


def _attend(q_ref, o_ref, sink_ref, blocks, s_sc, m_sc, p_sc, v_sc, after_scores=None):
    n_keys = blocks[0][1].shape[0]
    n_rows = s_sc.shape[1]
    key_row = lax.broadcasted_iota(jnp.int32, (2 * n_keys, LANES), 0)
    key_lane = lax.broadcasted_iota(jnp.int32, (2 * n_keys, LANES), 1)
    ones_cat = jnp.where((key_row >= n_keys) == (key_lane >= HEAD_DIM), 1.0, 0.0).astype(BF16)
    lo_lanes = lax.broadcasted_iota(jnp.int32, (n_rows, LANES), 1) < HEAD_DIM
    halves = [slice(hh * n_keys, (hh + 1) * n_keys) for hh in range(2)]

    tiles = []
    n_v = 0
    for q_rows, k_all, v_all, valid in blocks:
        for h in range(N_KV_HEADS):
            kcat = _head_pair_operands(k_all, h)
            v_sc[n_v] = jnp.concatenate([_head_pair_operands(v_all, h), ones_cat], axis=1)
            for p in range(2 * h, 2 * h + 2):
                lanes = slice(p * LANES, (p + 1) * LANES)
                sc = lax.dot_general(q_ref[q_rows, lanes], kcat, (((1,), (1,)), ((), ())),
                                     preferred_element_type=F32)
                s_sc[len(tiles)] = jnp.where(valid, sc, NEG)
                tiles.append((q_rows, lanes, p, n_v))
            n_v += 1
    if after_scores is not None:
        after_scores()
    for t, (_, _, p, _) in enumerate(tiles):
        for hh in range(2):
            m = jnp.maximum(jnp.max(s_sc[t, :, halves[hh]], axis=-1, keepdims=True), sink_ref[2 * p + hh])
            m_sc[2 * t + hh] = jnp.broadcast_to(m, (n_rows, LANES))
    for t in range(len(tiles)):
        p_sc[t] = jnp.concatenate(
            [jnp.exp(s_sc[t, :, halves[hh]] - jnp.tile(m_sc[2 * t + hh], (1, n_keys // LANES))).astype(BF16)
             for hh in range(2)], axis=1)
    for t, (q_rows, lanes, p, vi) in enumerate(tiles):
        e = p_sc[t]
        num_den = jnp.dot(e, v_sc[vi], preferred_element_type=F32)
        num = num_den[:, :LANES]
        den = num_den[:, LANES:]
        sink_term = jnp.where(lo_lanes, jnp.exp(sink_ref[2 * p] - m_sc[2 * t]),
                              jnp.exp(sink_ref[2 * p + 1] - m_sc[2 * t + 1]))
        o_ref[q_rows, lanes] = (num / (den + sink_term)).astype(BF16)


def _attn_scratch(n_blocks, n_rows, n_keys):
    n_tiles = n_blocks * N_HEADS // 2
    return [
        pltpu.VMEM((n_tiles, n_rows, 2 * n_keys), F32),
        pltpu.VMEM((2 * n_tiles, n_rows, LANES), F32),
        pltpu.VMEM((n_tiles, n_rows, 2 * n_keys), BF16),
        pltpu.VMEM((n_blocks * N_KV_HEADS, 2 * n_keys, 2 * LANES), BF16),
    ]


def _attn_scratch_bytes(n_blocks, n_rows, n_keys):
    n_tiles = n_blocks * N_HEADS // 2
    return (n_tiles * n_rows * 2 * n_keys * 6 + 2 * n_tiles * n_rows * LANES * 4
            + n_blocks * N_KV_HEADS * 2 * n_keys * 2 * LANES * 2)


def _prompt_attention(first, sink_ref, q_ref, kc_ref, kp_ref, vc_ref, vp_ref, o_ref, scratch, after_scores=None):
    n_sub = q_ref.shape[0] // ATT_BLK
    n_keys = 2 * ATT_BLK
    shape = (ATT_BLK, 2 * n_keys)
    q_chunk = lax.broadcasted_iota(jnp.int32, shape, 0) // CHUNK
    k_chunk = (lax.broadcasted_iota(jnp.int32, shape, 1) & (n_keys - 1)) // CHUNK
    band = (k_chunk >= q_chunk) & (k_chunk <= q_chunk + WINDOW // CHUNK)
    first_valid = band & (k_chunk >= jnp.where(first, ATT_BLK // CHUNK, 0))
    blocks = []
    for s in range(n_sub):
        rows = slice(s * ATT_BLK, (s + 1) * ATT_BLK)
        if s == 0:
            k_prev, v_prev, valid = kp_ref[...], vp_ref[...], first_valid
        else:
            prev = slice((s - 1) * ATT_BLK, s * ATT_BLK)
            k_prev, v_prev, valid = kc_ref[prev, :], vc_ref[prev, :], band
        k_all = jnp.concatenate([k_prev, kc_ref[rows, :]], axis=0)
        v_all = jnp.concatenate([v_prev, vc_ref[rows, :]], axis=0)
        blocks.append((rows, k_all, v_all, valid))
    _attend(q_ref, o_ref, sink_ref, blocks, *scratch, after_scores=after_scores)


def _attn_sample_body(sink_ref, q_ref, k_ref, v_ref, o_ref, *scratch, n_valid, n_keys, rows_per_seq):
    shape = (rows_per_seq, 2 * n_keys)
    valid = (lax.broadcasted_iota(jnp.int32, shape, 1) & (n_keys - 1)) < n_valid
    blocks = []
    for b in range(q_ref.shape[0] // rows_per_seq):
        keys = slice(b * n_keys, (b + 1) * n_keys)
        blocks.append((slice(b * rows_per_seq, (b + 1) * rows_per_seq), k_ref[keys, :], v_ref[keys, :], valid))
    _attend(q_ref, o_ref, sink_ref, blocks, *scratch)


def _attn_sample(sinks, q, k_all, v_all, n_valid, n_keys, rows_per_seq, seqs_per_step):
    t = q.shape[0]
    tq = rows_per_seq * seqs_per_step
    tk = n_keys * seqs_per_step
    nbytes = 2 * (2 * tq * Q_W * 2 + 2 * tk * KV_W * 4) + _attn_scratch_bytes(seqs_per_step, rows_per_seq, n_keys)
    return pl.pallas_call(
        functools.partial(_attn_sample_body, n_valid=n_valid, n_keys=n_keys, rows_per_seq=rows_per_seq),
        grid=(t // tq,),
        in_specs=[
            pl.BlockSpec(memory_space=pltpu.SMEM),
            pl.BlockSpec((tq, Q_W), lambda i: (i, 0)),
            pl.BlockSpec((tk, KV_W), lambda i: (i, 0)),
            pl.BlockSpec((tk, KV_W), lambda i: (i, 0)),
        ],
        out_specs=pl.BlockSpec((tq, Q_W), lambda i: (i, 0)),
        out_shape=jax.ShapeDtypeStruct((t, Q_W), BF16),
        scratch_shapes=_attn_scratch(seqs_per_step, rows_per_seq, n_keys),
        compiler_params=_params(1, nbytes),
        name="attn_sample",
    )(sinks, q, k_all, v_all)


def _mix_body(*refs, tiles_per_seq, hist_period, fused_attn, n_bg, n_steps):
    refs = list(refs)
    if fused_attn:
        sink_ref, x_ref, q_ref, kc_ref, kp_ref, vc_ref, vp_ref = refs[:7]
        del refs[:7]
    else:
        x_ref, oa_ref = refs[:2]
        del refs[:2]
    u_ref, b_ref, ga_ref, gc_ref = refs[:4]
    del refs[:4]
    if hist_period:
        p1_ref, p2_ref = refs[:2]
        del refs[:2]
    cw_ref, wao_ref, wco_ref, wmo_ref, g_ref = refs[:5]
    bg_src = refs[5:5 + n_bg]
    del refs[:5 + n_bg]
    h_ref = refs.pop(0)
    bg_dst = refs[:n_bg]
    del refs[:n_bg]
    if not hist_period:
        carry_sc = refs.pop(0)
    if fused_attn:
        oa_ref = refs.pop(0)
        attn_scratch = refs[:4]
        del refs[:4]
    if n_bg:
        _background_cast(list(zip(bg_src, bg_dst)), *refs, pl.program_id(0), n_steps)
    u = u_ref[...]
    tm = u.shape[0]
    if hist_period:
        p1, p2 = _shifted_rows(u, p1_ref[...], p2_ref[...], p2_ref[...], hist_period)
    else:
        @pl.when(pl.program_id(0) % tiles_per_seq == 0)
        def _():
            carry_sc[...] = jnp.zeros_like(carry_sc)

        last1 = carry_sc[HALO - 1:HALO, :]
        last2 = carry_sc[HALO - 2:HALO - 1, :]
        p1, p2 = _shifted_rows(u, last1, last2, last1, tm)
        carry_sc[...] = u[tm - HALO:tm, :]
    cy = cw_ref[0:1, :] * p2 + cw_ref[1:2, :] * p1 + cw_ref[2:3, :] * u
    y_conv = (b_ref[...].astype(F32) * cy).astype(BF16)
    if fused_attn:
        conv_branch = []
        first = (pl.program_id(0) % tiles_per_seq) == 0
        _prompt_attention(
            first, sink_ref, q_ref, kc_ref, kp_ref, vc_ref, vp_ref, oa_ref, attn_scratch,
            after_scores=lambda: conv_branch.append(jnp.dot(y_conv, wco_ref[...], preferred_element_type=F32)))
        conv_o = conv_branch[0]
    else:
        conv_o = jnp.dot(y_conv, wco_ref[...], preferred_element_type=F32)
    attn_o = jnp.dot(oa_ref[...], wao_ref[...], preferred_element_type=F32)
    mixed = (jax.nn.sigmoid(ga_ref[...].astype(F32)) * attn_o
             + jax.nn.sigmoid(gc_ref[...].astype(F32)) * conv_o)
    mo = jnp.dot(mixed.astype(BF16), wmo_ref[...], preferred_element_type=F32)
    h_ref[...] = x_ref[...] + _rms_norm(mo, g_ref[...])


def _mix(x, oa, u, b, ga, gc, hist, cw, wao, wco, wmo, g, seq, tm, background=(), attn=None):
    t = x.shape[0]
    row = lambda i: (i, 0)
    fixed = lambda i: (0, 0)
    scratch = []
    if attn is None:
        act_specs = [pl.BlockSpec((tm, D_MODEL), row), pl.BlockSpec((tm, Q_W), row)]
        acts = [x, oa]
    else:
        sinks, q, k, v = attn
        ratio = tm // ATT_BLK
        prev = lambda i: (jnp.maximum(i * ratio - 1, 0), 0)
        act_specs = [
            pl.BlockSpec(memory_space=pltpu.SMEM),
            pl.BlockSpec((tm, D_MODEL), row),
            pl.BlockSpec((tm, Q_W), row),
            pl.BlockSpec((tm, KV_W), row),
            pl.BlockSpec((ATT_BLK, KV_W), prev),
            pl.BlockSpec((tm, KV_W), row),
            pl.BlockSpec((ATT_BLK, KV_W), prev),
        ]
        acts = [sinks, x, q, k, k, v, v]
    act_specs += [
        pl.BlockSpec((tm, D_CONV), row),
        pl.BlockSpec((tm, D_CONV), row),
        pl.BlockSpec((tm, D_MODEL), row),
        pl.BlockSpec((tm, D_MODEL), row),
    ]
    acts += [u, b, ga, gc]
    if hist is None:
        hist_period = 0
        scratch.append(pltpu.VMEM((HALO, D_CONV), F32))
    else:
        hist_period = seq
        act_specs += [pl.BlockSpec((tm, D_CONV), row)] * 2
        acts += list(hist)
    attn_bytes = 0
    if attn is not None:
        scratch.append(pltpu.VMEM((tm, Q_W), BF16))
        scratch += _attn_scratch(tm // ATT_BLK, ATT_BLK, 2 * ATT_BLK)
        attn_bytes = (tm * Q_W * 2 + 4 * (tm + ATT_BLK) * KV_W * 4
                      + _attn_scratch_bytes(tm // ATT_BLK, ATT_BLK, 2 * ATT_BLK))
    weights = (wao, wco, wmo)
    out_specs = [pl.BlockSpec((tm, D_MODEL), row)]
    out_shape = [jax.ShapeDtypeStruct((t, D_MODEL), F32)]
    row_bytes = D_MODEL * 4 * 2 + Q_W * 2 + D_CONV * (4 + 2) + D_MODEL * 2 * 2
    if hist is not None:
        row_bytes += 2 * D_CONV * 4
    nbytes = (2 * tm * row_bytes + (Q_W + D_CONV + D_MODEL) * D_MODEL * 2 + 6 * tm * D_MODEL * 4
              + attn_bytes)
    if background:
        cols = background[0].shape[1]
        out_specs += [pl.BlockSpec(memory_space=pl.ANY)] * len(background)
        out_shape += [jax.ShapeDtypeStruct(bw.shape, BF16) for bw in background]
        scratch += _background_scratch(BG_ROWS_UP, cols)
        nbytes += _background_bytes(BG_ROWS_UP, cols)
    in_specs = (act_specs + [pl.BlockSpec((CONV_W, D_CONV), fixed)]
                + [pl.BlockSpec(w.shape, fixed, pipeline_mode=pl.Buffered(1)) for w in weights]
                + [pl.BlockSpec((1, D_MODEL), fixed)] + [pl.BlockSpec(memory_space=pl.ANY)] * len(background))
    return pl.pallas_call(
        functools.partial(_mix_body, tiles_per_seq=max(seq // tm, 1), hist_period=hist_period,
                          fused_attn=attn is not None, n_bg=len(background), n_steps=t // tm),
        grid=(t // tm,),
        in_specs=in_specs,
        out_specs=tuple(out_specs),
        out_shape=tuple(out_shape),
        scratch_shapes=scratch,
        compiler_params=_params(1, nbytes),
        name="mix",
    )(*acts, cw, wao, wco, wmo, g, *background)


def _ffn_up_body(*refs, tiles_per_seq, hist_period, n_bg, n_steps):
    refs = list(refs)
    h_ref, g_ref, wa_ref, wg_ref, cw_ref = refs[:5]
    del refs[:5]
    if hist_period:
        p1_ref, p2_ref = refs[:2]
        del refs[:2]
    bg_src = refs[:n_bg]
    del refs[:n_bg]
    f_ref, aux_ref = refs[:2]
    bg_dst = refs[2:2 + n_bg]
    del refs[:2 + n_bg]
    xn_sc = refs.pop(0)
    if hist_period:
        a_ref = aux_ref
    else:
        tail_ref, carry_sc = aux_ref, refs.pop(0)
    i = pl.program_id(0)
    j = pl.program_id(1)
    if n_bg:
        _background_cast(list(zip(bg_src, bg_dst)), *refs, i * pl.num_programs(1) + j, n_steps)

    @pl.when(j == 0)
    def _():
        xn_sc[...] = _rms_norm(h_ref[...], g_ref[...]).astype(BF16)

    if not hist_period:
        @pl.when(i % tiles_per_seq == 0)
        def _():
            carry_sc[j] = jnp.zeros(carry_sc.shape[1:], F32)

        tail = carry_sc[j]
    tm = xn_sc.shape[0]
    rb = min(tm, ROW_BLK)
    for r in range(tm // rb):
        rows = slice(r * rb, (r + 1) * rb)
        xn = xn_sc[rows, :]
        a = jnp.dot(xn, wa_ref[...], preferred_element_type=F32)
        gate = jnp.dot(xn, wg_ref[...], preferred_element_type=F32)
        if hist_period:
            p1, p2 = _shifted_rows(a, p1_ref[rows, :], p2_ref[rows, :], p2_ref[rows, :], hist_period)
            a_ref[rows, :] = a
        else:
            last1 = tail[HALO - 1:HALO, :]
            p1, p2 = _shifted_rows(a, last1, tail[HALO - 2:HALO - 1, :], last1, rb)
            tail = a[rb - HALO:rb, :]
        ua = cw_ref[0:1, :] * p2 + cw_ref[1:2, :] * p1 + cw_ref[2:3, :] * a
        f_ref[rows, :] = (jax.nn.gelu(ua, approximate=True) * gate).astype(BF16)
    if not hist_period:
        carry_sc[j] = tail
        tail_ref[...] = tail


def _ffn_up(h, g, w_up, cw, hist, seq, tm, background=()):
    t = h.shape[0]
    n_j = D_FF // TN_FF
    grid = (t // tm, n_j)
    in_specs = [
        pl.BlockSpec((tm, D_MODEL), lambda i, j: (i, 0)),
        pl.BlockSpec((1, D_MODEL), lambda i, j: (0, 0)),
        pl.BlockSpec((D_MODEL, TN_FF), lambda i, j: (0, j)),
        pl.BlockSpec((D_MODEL, TN_FF), lambda i, j: (0, j + n_j)),
        pl.BlockSpec((CONV_W, TN_FF), lambda i, j: (0, j)),
    ]
    args = [h, g, w_up, w_up, cw]
    scratch = [pltpu.VMEM((tm, D_MODEL), BF16)]
    f_spec = pl.BlockSpec((tm, TN_FF), lambda i, j: (i, j))
    f_shape = jax.ShapeDtypeStruct((t, D_FF), BF16)
    if hist is None:
        hist_period = 0
        scratch.append(pltpu.VMEM((n_j, HALO, TN_FF), F32))
        out_specs = (f_spec, pl.BlockSpec((None, HALO, TN_FF), lambda i, j: (i, 0, j)))
        out_shape = (f_shape, jax.ShapeDtypeStruct((t // tm, HALO, D_FF), F32))
    else:
        hist_period = seq
        in_specs += [pl.BlockSpec((tm, TN_FF), lambda i, j: (i, j))] * 2
        args += list(hist)
        out_specs = (f_spec, pl.BlockSpec((tm, TN_FF), lambda i, j: (i, j)))
        out_shape = (f_shape, jax.ShapeDtypeStruct((t, D_FF), F32))
    nbytes = (2 * tm * D_MODEL * 4 + tm * D_MODEL * 2 + 4 * D_MODEL * TN_FF * 2
              + 2 * tm * TN_FF * (2 + 4 * 3) + 8 * tm * TN_FF * 4)
    if background:
        in_specs += [pl.BlockSpec(memory_space=pl.ANY)] * len(background)
        args += list(background)
        out_specs += (pl.BlockSpec(memory_space=pl.ANY),) * len(background)
        out_shape += tuple(jax.ShapeDtypeStruct(bw.shape, BF16) for bw in background)
        scratch += _background_scratch(BG_ROWS_2K, D_MODEL)
        nbytes += _background_bytes(BG_ROWS_2K, D_MODEL)
    return pl.pallas_call(
        functools.partial(_ffn_up_body, tiles_per_seq=max(seq // tm, 1), hist_period=hist_period,
                          n_bg=len(background), n_steps=grid[0] * grid[1]),
        grid=grid,
        in_specs=in_specs,
        out_specs=out_specs,
        out_shape=out_shape,
        scratch_shapes=scratch,
        compiler_params=_params(2, nbytes),
        name="ffn_up",
    )(*args)


def _ffn_down_body(f_ref, w_ref, h_ref, g_ref, o_ref):
    tm = f_ref.shape[0]
    rb = min(tm, ROW_BLK)
    for r in range(tm // rb):
        rows = slice(r * rb, (r + 1) * rb)
        acc = jnp.dot(f_ref[rows, :], w_ref[...], preferred_element_type=F32)
        o_ref[rows, :] = h_ref[rows, :] + _rms_norm(acc, g_ref[...])


def _ffn_down(f, w, h, g, tm):
    t = f.shape[0]
    nbytes = (2 * tm * (D_FF * 2 + D_MODEL * 4 * 2) + D_FF * D_MODEL * 2
              + 2 * min(tm, ROW_BLK) * D_MODEL * 4)
    return pl.pallas_call(
        _ffn_down_body,
        grid=(t // tm,),
        in_specs=[
            pl.BlockSpec((tm, D_FF), lambda i: (i, 0)),
            pl.BlockSpec((D_FF, D_MODEL), lambda i: (0, 0), pipeline_mode=pl.Buffered(1)),
            pl.BlockSpec((tm, D_MODEL), lambda i: (i, 0)),
            pl.BlockSpec((1, D_MODEL), lambda i: (0, 0)),
        ],
        out_specs=pl.BlockSpec((tm, D_MODEL), lambda i: (i, 0)),
        out_shape=jax.ShapeDtypeStruct((t, D_MODEL), F32),
        compiler_params=_params(1, nbytes),
        name="ffn_down",
    )(f, w, h, g)


def _rope_tables(pos):
    half = ROT_DIM // 2
    inv = jnp.power(jnp.float32(ROPE_THETA), -jnp.arange(half, dtype=F32) * (2.0 / ROT_DIM))
    ang = pos.astype(F32)[:, None] * inv[None, :]
    cos = jnp.cos(ang)
    sin = jnp.sin(ang)
    rest = HEAD_DIM - ROT_DIM
    cos_h = jnp.concatenate([cos, cos, jnp.ones((pos.shape[0], rest), F32)], axis=1)
    sin_h = jnp.concatenate([-sin, sin, jnp.zeros((pos.shape[0], rest), F32)], axis=1)
    reps = LANES // HEAD_DIM
    return jnp.tile(cos_h, (1, reps)), jnp.tile(sin_h, (1, reps))


def _row_history(state, seq):
    b, _, c = state.shape
    p1 = jnp.concatenate([state[:, 1:2], jnp.zeros((b, seq - 1, c), F32)], axis=1)
    p2 = jnp.concatenate([state[:, 0:1], state[:, 1:2], jnp.zeros((b, seq - 2, c), F32)], axis=1)
    return p1.reshape(b * seq, c), p2.reshape(b * seq, c)


def kernel(x_prompt, x_sample, cache_k, cache_v, state_conv_mix, state_conv_ffn, norm_mix_pre, w_in,
           attn_sinks, conv_mix_w, w_attn_o, w_conv_o, w_mix_out, norm_mix_post, norm_ffn_pre,
           w_ffn_up, conv_ffn_w, w_ffn_down, norm_ffn_post):
    assert w_in.shape == (1, D_MODEL, D_IN), "one layer only"
    batch, seq, _ = x_prompt.shape
    dec_batch, dec_seq, _ = x_sample.shape
    assert seq % 1024 == 0 and (dec_batch * dec_seq) % 256 == 0 and dec_seq % 16 == 0
    assert cache_k.shape[2] == WINDOW

    g_pre = norm_mix_pre[0][None]
    g_post = norm_mix_post[0][None]
    g_ffn_pre = norm_ffn_pre[0][None]
    g_ffn_post = norm_ffn_post[0][None]
    sinks = attn_sinks[0]
    cw_mix = conv_mix_w[0]
    cw_ffn = conv_ffn_w[0]

    tp = batch * seq
    xp = x_prompt.reshape(tp, D_MODEL)
    cos_p, sin_p = _rope_tables(jnp.arange(seq, dtype=F32))
    q, k, v, u, b, ga, gc, w_in_b, wao, wco, wmo = _inproj(
        xp, g_pre, cos_p, sin_p, w_in[0], tm=256, background=(w_attn_o[0], w_conv_o[0], w_mix_out[0]))
    hp, w_up = _mix(xp, None, u, b, ga, gc, None, cw_mix, wao, wco, wmo, g_post, seq, tm=256,
                    background=(w_ffn_up[0],), attn=(sinks, q, k, v))
    f, a_tail, w_down = _ffn_up(hp, g_ffn_pre, w_up, cw_ffn, None, seq, tm=FF_UP_TM,
                                background=(w_ffn_down[0],))
    yp = _ffn_down(f, w_down, hp, g_ffn_post, tm=512)

    keep = min(WINDOW, seq)
    new_k_prompt = k.reshape(batch, seq, KV_W)[:, seq - keep:].reshape(1, batch, keep, N_KV_HEADS, HEAD_DIM)
    new_v_prompt = v.reshape(batch, seq, KV_W)[:, seq - keep:].reshape(1, batch, keep, N_KV_HEADS, HEAD_DIM)
    conv_mix_prompt = u.reshape(batch, seq, D_CONV)[:, seq - (CONV_W - 1):][None]
    a_tail = a_tail.reshape(batch, seq // FF_UP_TM, HALO, D_FF)
    conv_ffn_prompt = a_tail[:, -1, HALO - (CONV_W - 1):][None]

    ts = dec_batch * dec_seq
    xs = x_sample.reshape(ts, D_MODEL)
    pos_s = PAST_LEN + jnp.arange(dec_seq, dtype=F32)
    cos_s, sin_s = _rope_tables(jnp.tile(pos_s, dec_batch))
    qs, ks, vs, us, bs, gas, gcs = _inproj(xs, g_pre, cos_s, sin_s, w_in_b, tm=ts)
    n_valid = WINDOW + dec_seq
    n_keys = 2 * WINDOW
    pad = jnp.zeros((dec_batch, n_keys - n_valid, KV_W), F32)
    k_all = jnp.concatenate([cache_k[0].reshape(dec_batch, WINDOW, KV_W),
                             ks.reshape(dec_batch, dec_seq, KV_W), pad], axis=1)
    v_all = jnp.concatenate([cache_v[0].reshape(dec_batch, WINDOW, KV_W),
                             vs.reshape(dec_batch, dec_seq, KV_W), pad], axis=1)
    oas = _attn_sample(sinks, qs, k_all.reshape(dec_batch * n_keys, KV_W),
                       v_all.reshape(dec_batch * n_keys, KV_W), n_valid, n_keys, dec_seq, seqs_per_step=4)
    (hs,) = _mix(xs, oas, us, bs, gas, gcs, _row_history(state_conv_mix[0], dec_seq),
                 cw_mix, wao, wco, wmo, g_post, dec_seq, tm=ts)
    fs, a_s = _ffn_up(hs, g_ffn_pre, w_up, cw_ffn, _row_history(state_conv_ffn[0], dec_seq), dec_seq, tm=ts)
    ys = _ffn_down(fs, w_down, hs, g_ffn_post, tm=ts)

    new_k_sample = ks.reshape(1, dec_batch, dec_seq, N_KV_HEADS, HEAD_DIM)
    new_v_sample = vs.reshape(1, dec_batch, dec_seq, N_KV_HEADS, HEAD_DIM)
    conv_mix_sample = us.reshape(dec_batch, dec_seq, D_CONV)[:, dec_seq - (CONV_W - 1):][None]
    conv_ffn_sample = a_s.reshape(dec_batch, dec_seq, D_FF)[:, dec_seq - (CONV_W - 1):][None]

    return (yp.reshape(batch, seq, D_MODEL), ys.reshape(dec_batch, dec_seq, D_MODEL),
            new_k_prompt, new_v_prompt, conv_mix_prompt, conv_ffn_prompt,
            new_k_sample, new_v_sample, conv_mix_sample, conv_ffn_sample)
```

```python
import functools

import jax
import jax.numpy as jnp
from jax import lax
from jax.experimental import pallas as pl
from jax.experimental.pallas import tpu as pltpu

F32 = jnp.float32
BF16 = jnp.bfloat16

D_MODEL = 2048
CHUNK = 64
N_HEADS = 16
N_KV_HEADS = 4
HEAD_DIM = 64
ROT_DIM = HEAD_DIM // 4
ROPE_THETA = 500000.0
WINDOW = 128
D_CONV = 1024
CONV_W = 3
D_FF = 5632
EPS = 1e-6
NEG = -1e30
PAST_LEN = 4096
Q_W = N_HEADS * HEAD_DIM
KV_W = N_KV_HEADS * HEAD_DIM
D_IN = Q_W + 2 * KV_W + 3 * D_CONV + 2 * D_MODEL

LANES = 128
SUBLANES = 8
VMEM_LIMIT_CAP = 60 * 1024 * 1024
VMEM_COMPILER_SLACK = 6 * 1024 * 1024

ROW_TILE = 256
FF_DOWN_TM = 512
SAMPLE_SEQS_PER_STEP = 4
TN_IN = 512
TN_FF = 512
FF_UP_TM = 1024
IN_STAGE_ROWS = 64
BG_ROWS_2K = 256
BG_ROWS_UP = 128
ATT_BLK = 2 * CHUNK
HALO = SUBLANES
ROW_BLK = 256

_T_Q = (0, 2)
_T_KV = (2, 3)
_T_XIN = (3, 5)
_T_B = (5, 7)
_T_C = (7, 9)
_T_GA = (9, 13)
_T_GC = (13, 17)


def _vmem_limit(nbytes):
    return int(min(nbytes + VMEM_COMPILER_SLACK, VMEM_LIMIT_CAP))


def _params(n_axes, nbytes):
    return pltpu.CompilerParams(
        dimension_semantics=("arbitrary",) * n_axes,
        vmem_limit_bytes=_vmem_limit(nbytes),
    )


def _rms_norm(x, g):
    return x * lax.rsqrt(jnp.mean(x * x, axis=-1, keepdims=True) + EPS) * g


def _rope128(x, cos, sin):
    dh = lax.broadcasted_iota(jnp.int32, x.shape, 1) & (HEAD_DIM - 1)
    half = ROT_DIM // 2
    partner = jnp.where(dh < half, pltpu.roll(x, LANES - half, axis=1), pltpu.roll(x, half, axis=1))
    return x * cos + partner * sin


def _shifted_rows(a, prev1_row0, prev2_row0, prev2_row1, period):
    assert period & (period - 1) == 0
    rows = lax.broadcasted_iota(jnp.int32, (a.shape[0], 1), 0) & (period - 1)
    p1 = jnp.where(rows == 0, prev1_row0, pltpu.roll(a, 1, axis=0))
    p2 = jnp.where(rows == 0, prev2_row0, jnp.where(rows == 1, prev2_row1, pltpu.roll(a, 2, axis=0)))
    return p1, p2


def _stage_weight(w_hbm, w_sc, w_out, stage, in_sems, out_sem, step, n_steps):
    chunk = stage.shape[1]
    n_chunks = w_hbm.shape[0] // chunk

    def fetch(c, slot):
        return pltpu.make_async_copy(w_hbm.at[pl.ds(c * chunk, chunk), :], stage.at[slot], in_sems.at[slot])

    export = pltpu.make_async_copy(w_sc, w_out, out_sem)

    @pl.when(step == 0)
    def _():
        fetch(0, 0).start()

        def body(c, carry):
            slot = lax.rem(c, 2)

            @pl.when(c + 1 < n_chunks)
            def _():
                fetch(c + 1, 1 - slot).start()

            fetch(c, slot).wait()
            w_sc[pl.ds(pl.multiple_of(c * chunk, chunk), chunk), :] = stage[slot].astype(BF16)
            return carry

        lax.fori_loop(0, n_chunks, body, 0)
        export.start()

    @pl.when(step == n_steps - 1)
    def _():
        export.wait()


def _background_cast(jobs, buf_in, buf_out, sems, step, n_steps):
    chunk = buf_in.shape[0]
    assert all(src.shape[0] % chunk == 0 for src, _ in jobs)
    total = sum(src.shape[0] // chunk for src, _ in jobs)
    assert total + 2 <= n_steps

    @pl.when(step < total + 2)
    def _():
        first = 0
        for src, dst in jobs:
            n = src.shape[0] // chunk

            def fetch(c, src=src):
                return pltpu.make_async_copy(src.at[pl.ds(c * chunk, chunk), :], buf_in, sems.at[0])

            def send(c, dst=dst):
                return pltpu.make_async_copy(buf_out, dst.at[pl.ds(c * chunk, chunk), :], sems.at[1])

            c_fetch = step - first
            c_cast = c_fetch - 1
            c_done = c_fetch - 2

            @pl.when((c_done >= 0) & (c_done < n))
            def _():
                send(c_done).wait()

            @pl.when((c_cast >= 0) & (c_cast < n))
            def _():
                fetch(c_cast).wait()
                buf_out[...] = buf_in[...].astype(BF16)
                send(c_cast).start()

            @pl.when((c_fetch >= 0) & (c_fetch < n))
            def _():
                fetch(c_fetch).start()

            first += n


def _background_scratch(chunk, cols):
    return [
        pltpu.VMEM((chunk, cols), F32),
        pltpu.VMEM((chunk, cols), BF16),
        pltpu.SemaphoreType.DMA((2,)),
    ]


def _background_bytes(chunk, cols):
    return chunk * cols * (4 + 2)


def _staging_scratch(w_shape, chunk):
    assert w_shape[0] % chunk == 0
    return [
        pltpu.VMEM(w_shape, BF16),
        pltpu.VMEM((2, chunk, w_shape[1]), F32),
        pltpu.SemaphoreType.DMA((2,)),
        pltpu.SemaphoreType.DMA(()),
    ]


def _staging_bytes(w_shape, chunk):
    return w_shape[0] * w_shape[1] * 2 + 2 * chunk * w_shape[1] * 4


def _inproj_body(*refs, staged, n_bg, n_steps):
    refs = list(refs)
    x_ref, g_ref, cos_ref, sin_ref, w_ref = refs[:5]
    bg_src = refs[5:5 + n_bg]
    del refs[:5 + n_bg]
    q_ref, k_ref, v_ref, u_ref, b_ref, ga_ref, gc_ref = refs[:7]
    del refs[:7]
    if staged:
        w_hbm, w_out = w_ref, refs.pop(0)
    bg_dst = refs[:n_bg]
    del refs[:n_bg]
    xn_sc = refs.pop(0)
    step = pl.program_id(0)
    if staged:
        w_ref, stage, in_sems, out_sem = refs[:4]
        del refs[:4]
        _stage_weight(w_hbm, w_ref, w_out, stage, in_sems, out_sem, step, n_steps)
    if n_bg:
        _background_cast(list(zip(bg_src, bg_dst)), *refs, step, n_steps)
    xn_sc[...] = _rms_norm(x_ref[...], g_ref[...]).astype(BF16)
    cos = cos_ref[...]
    sin = sin_ref[...]

    def col(t):
        return jnp.dot(xn_sc[...], w_ref[:, t * TN_IN:(t + 1) * TN_IN], preferred_element_type=F32)

    def blocks(n):
        return [slice(blk * LANES, (blk + 1) * LANES) for blk in range(n)]

    for t in range(*_T_Q):
        acc = col(t)
        base = (t - _T_Q[0]) * TN_IN
        for sl in blocks(TN_IN // LANES):
            q_ref[:, base + sl.start:base + sl.stop] = (
                _rope128(acc[:, sl], cos, sin) * (HEAD_DIM ** -0.5)).astype(BF16)
    acc = col(_T_KV[0])
    for sl in blocks(KV_W // LANES):
        k_ref[:, sl] = _rope128(acc[:, sl], cos, sin)
    v_ref[...] = acc[:, KV_W:2 * KV_W]
    for c in range(_T_C[1] - _T_C[0]):
        cols = slice(c * TN_IN, (c + 1) * TN_IN)
        u_ref[:, cols] = col(_T_C[0] + c) * col(_T_XIN[0] + c)
        b_ref[:, cols] = col(_T_B[0] + c).astype(BF16)
    for c in range(_T_GA[1] - _T_GA[0]):
        cols = slice(c * TN_IN, (c + 1) * TN_IN)
        ga_ref[:, cols] = col(_T_GA[0] + c).astype(BF16)
        gc_ref[:, cols] = col(_T_GC[0] + c).astype(BF16)


def _inproj(x, g, cos, sin, w, tm, background=()):
    t = x.shape[0]
    n_pos = cos.shape[0] // tm
    row = lambda i: (i, 0)
    fixed = lambda i: (0, 0)
    pos = lambda i: (i % n_pos, 0)
    widths = (Q_W, KV_W, KV_W, D_CONV, D_CONV, D_MODEL, D_MODEL)
    dtypes = (BF16, F32, F32, F32, BF16, BF16, BF16)
    out_row_bytes = sum(wd * jnp.dtype(dt).itemsize for wd, dt in zip(widths, dtypes))
    nbytes = (2 * tm * (D_MODEL * 4 + 2 * LANES * 4 + out_row_bytes)
              + tm * D_MODEL * 2 + 6 * tm * TN_IN * 4)
    out_specs = [pl.BlockSpec((tm, wd), row) for wd in widths]
    out_shape = [jax.ShapeDtypeStruct((t, wd), dt) for wd, dt in zip(widths, dtypes)]
    scratch = [pltpu.VMEM((tm, D_MODEL), BF16)]
    staged = w.dtype == F32
    if staged:
        w_spec = pl.BlockSpec(memory_space=pl.ANY)
        out_specs.append(pl.BlockSpec(memory_space=pl.ANY))
        out_shape.append(jax.ShapeDtypeStruct(w.shape, BF16))
        scratch += _staging_scratch(w.shape, IN_STAGE_ROWS)
        nbytes += _staging_bytes(w.shape, IN_STAGE_ROWS)
    else:
        w_spec = pl.BlockSpec((D_MODEL, D_IN), fixed, pipeline_mode=pl.Buffered(1))
        nbytes += D_MODEL * D_IN * 2
    if background:
        out_specs += [pl.BlockSpec(memory_space=pl.ANY)] * len(background)
        out_shape += [jax.ShapeDtypeStruct(bw.shape, BF16) for bw in background]
        scratch += _background_scratch(BG_ROWS_2K, D_MODEL)
        nbytes += _background_bytes(BG_ROWS_2K, D_MODEL)
    return pl.pallas_call(
        functools.partial(_inproj_body, staged=staged, n_bg=len(background), n_steps=t // tm),
        grid=(t // tm,),
        in_specs=[
            pl.BlockSpec((tm, D_MODEL), row),
            pl.BlockSpec((1, D_MODEL), fixed),
            pl.BlockSpec((tm, LANES), pos),
            pl.BlockSpec((tm, LANES), pos),
            w_spec,
        ] + [pl.BlockSpec(memory_space=pl.ANY)] * len(background),
        out_specs=tuple(out_specs),
        out_shape=tuple(out_shape),
        scratch_shapes=scratch,
        compiler_params=_params(1, nbytes),
        name="inproj",
    )(x, g, cos, sin, w, *background)


def _head_pair_operands(x256, h):
    blk = x256[:, (h // 2) * LANES:(h // 2 + 1) * LANES]
    lo_lanes = lax.broadcasted_iota(jnp.int32, blk.shape, 1) < HEAD_DIM
    if h % 2 == 0:
        lo = jnp.where(lo_lanes, blk, 0.0)
        hi = pltpu.roll(lo, HEAD_DIM, axis=1)
    else:
        hi = jnp.where(lo_lanes, 0.0, blk)
        lo = pltpu.roll(hi, HEAD_DIM, axis=1)
    return jnp.concatenate([lo, hi], axis=0).astype(BF16)


def _attend(q_ref, o_ref, sink_ref, blocks, s_sc, m_sc, p_sc, v_sc, after_scores=None):
    n_keys = blocks[0][1].shape[0]
    n_rows = s_sc.shape[1]
    key_row = lax.broadcasted_iota(jnp.int32, (2 * n_keys, LANES), 0)
    key_lane = lax.broadcasted_iota(jnp.int32, (2 * n_keys, LANES), 1)
    ones_cat = jnp.where((key_row >= n_keys) == (key_lane >= HEAD_DIM), 1.0, 0.0).astype(BF16)
    lo_lanes = lax.broadcasted_iota(jnp.int32, (n_rows, LANES), 1) < HEAD_DIM
    halves = [slice(hh * n_keys, (hh + 1) * n_keys) for hh in range(2)]

    tiles = []
    n_v = 0
    for q_rows, k_all, v_all, valid in blocks:
        for h in range(N_KV_HEADS):
            kcat = _head_pair_operands(k_all, h)
            v_sc[n_v] = jnp.concatenate([_head_pair_operands(v_all, h), ones_cat], axis=1)
            for p in range(2 * h, 2 * h + 2):
                lanes = slice(p * LANES, (p + 1) * LANES)
                sc = lax.dot_general(q_ref[q_rows, lanes], kcat, (((1,), (1,)), ((), ())),
                                     preferred_element_type=F32)
                s_sc[len(tiles)] = jnp.where(valid, sc, NEG)
                tiles.append((q_rows, lanes, p, n_v))
            n_v += 1
    if after_scores is not None:
        after_scores()
    for t, (_, _, p, _) in enumerate(tiles):
        for hh in range(2):
            m = jnp.maximum(jnp.max(s_sc[t, :, halves[hh]], axis=-1, keepdims=True), sink_ref[2 * p + hh])
            m_sc[2 * t + hh] = jnp.broadcast_to(m, (n_rows, LANES))
    for t in range(len(tiles)):
        p_sc[t] = jnp.concatenate(
            [jnp.exp(s_sc[t, :, halves[hh]] - jnp.tile(m_sc[2 * t + hh], (1, n_keys // LANES))).astype(BF16)
             for hh in range(2)], axis=1)
    for t, (q_rows, lanes, p, vi) in enumerate(tiles):
        e = p_sc[t]
        num_den = jnp.dot(e, v_sc[vi], preferred_element_type=F32)
        num = num_den[:, :LANES]
        den = num_den[:, LANES:]
        sink_term = jnp.where(lo_lanes, jnp.exp(sink_ref[2 * p] - m_sc[2 * t]),
                              jnp.exp(sink_ref[2 * p + 1] - m_sc[2 * t + 1]))
        o_ref[q_rows, lanes] = (num / (den + sink_term)).astype(BF16)


def _attn_scratch(n_blocks, n_rows, n_keys):
    n_tiles = n_blocks * N_HEADS // 2
    return [
        pltpu.VMEM((n_tiles, n_rows, 2 * n_keys), F32),
        pltpu.VMEM((2 * n_tiles, n_rows, LANES), F32),
        pltpu.VMEM((n_tiles, n_rows, 2 * n_keys), BF16),
        pltpu.VMEM((n_blocks * N_KV_HEADS, 2 * n_keys, 2 * LANES), BF16),
    ]


def _attn_scratch_bytes(n_blocks, n_rows, n_keys):
    n_tiles = n_blocks * N_HEADS // 2
    return (n_tiles * n_rows * 2 * n_keys * 6 + 2 * n_tiles * n_rows * LANES * 4
            + n_blocks * N_KV_HEADS * 2 * n_keys * 2 * LANES * 2)


def _prompt_attention(first, sink_ref, q_ref, kc_ref, kp_ref, vc_ref, vp_ref, o_ref, scratch, after_scores=None):
    n_sub = q_ref.shape[0] // ATT_BLK
    n_keys = 2 * ATT_BLK
    shape = (ATT_BLK, 2 * n_keys)
    q_chunk = lax.broadcasted_iota(jnp.int32, shape, 0) // CHUNK
    k_chunk = (lax.broadcasted_iota(jnp.int32, shape, 1) & (n_keys - 1)) // CHUNK
    band = (k_chunk >= q_chunk) & (k_chunk <= q_chunk + WINDOW // CHUNK)
    first_valid = band & (k_chunk >= jnp.where(first, ATT_BLK // CHUNK, 0))
    blocks = []
    for s in range(n_sub):
        rows = slice(s * ATT_BLK, (s + 1) * ATT_BLK)
        if s == 0:
            k_prev, v_prev, valid = kp_ref[...], vp_ref[...], first_valid
        else:
            prev = slice((s - 1) * ATT_BLK, s * ATT_BLK)
            k_prev, v_prev, valid = kc_ref[prev, :], vc_ref[prev, :], band
        k_all = jnp.concatenate([k_prev, kc_ref[rows, :]], axis=0)
        v_all = jnp.concatenate([v_prev, vc_ref[rows, :]], axis=0)
        blocks.append((rows, k_all, v_all, valid))
    _attend(q_ref, o_ref, sink_ref, blocks, *scratch, after_scores=after_scores)


def _attn_sample_body(sink_ref, q_ref, k_ref, v_ref, o_ref, *scratch, n_valid, n_keys, rows_per_seq):
    shape = (rows_per_seq, 2 * n_keys)
    valid = (lax.broadcasted_iota(jnp.int32, shape, 1) & (n_keys - 1)) < n_valid
    blocks = []
    for b in range(q_ref.shape[0] // rows_per_seq):
        keys = slice(b * n_keys, (b + 1) * n_keys)
        blocks.append((slice(b * rows_per_seq, (b + 1) * rows_per_seq), k_ref[keys, :], v_ref[keys, :], valid))
    _attend(q_ref, o_ref, sink_ref, blocks, *scratch)


def _attn_sample(sinks, q, k_all, v_all, n_valid, n_keys, rows_per_seq, seqs_per_step):
    t = q.shape[0]
    tq = rows_per_seq * seqs_per_step
    tk = n_keys * seqs_per_step
    nbytes = 2 * (2 * tq * Q_W * 2 + 2 * tk * KV_W * 4) + _attn_scratch_bytes(seqs_per_step, rows_per_seq, n_keys)
    return pl.pallas_call(
        functools.partial(_attn_sample_body, n_valid=n_valid, n_keys=n_keys, rows_per_seq=rows_per_seq),
        grid=(t // tq,),
        in_specs=[
            pl.BlockSpec(memory_space=pltpu.SMEM),
            pl.BlockSpec((tq, Q_W), lambda i: (i, 0)),
            pl.BlockSpec((tk, KV_W), lambda i: (i, 0)),
            pl.BlockSpec((tk, KV_W), lambda i: (i, 0)),
        ],
        out_specs=pl.BlockSpec((tq, Q_W), lambda i: (i, 0)),
        out_shape=jax.ShapeDtypeStruct((t, Q_W), BF16),
        scratch_shapes=_attn_scratch(seqs_per_step, rows_per_seq, n_keys),
        compiler_params=_params(1, nbytes),
        name="attn_sample",
    )(sinks, q, k_all, v_all)


def _mix_body(*refs, tiles_per_seq, hist_period, fused_attn, n_bg, n_steps):
    refs = list(refs)
    if fused_attn:
        sink_ref, x_ref, q_ref, kc_ref, kp_ref, vc_ref, vp_ref = refs[:7]
        del refs[:7]
    else:
        x_ref, oa_ref = refs[:2]
        del refs[:2]
    u_ref, b_ref, ga_ref, gc_ref = refs[:4]
    del refs[:4]
    if hist_period:
        p1_ref, p2_ref = refs[:2]
        del refs[:2]
    cw_ref, wao_ref, wco_ref, wmo_ref, g_ref = refs[:5]
    bg_src = refs[5:5 + n_bg]
    del refs[:5 + n_bg]
    h_ref = refs.pop(0)
    bg_dst = refs[:n_bg]
    del refs[:n_bg]
    if not hist_period:
        carry_sc = refs.pop(0)
    if fused_attn:
        oa_ref = refs.pop(0)
        attn_scratch = refs[:4]
        del refs[:4]
    if n_bg:
        _background_cast(list(zip(bg_src, bg_dst)), *refs, pl.program_id(0), n_steps)
    u = u_ref[...]
    tm = u.shape[0]
    if hist_period:
        p1, p2 = _shifted_rows(u, p1_ref[...], p2_ref[...], p2_ref[...], hist_period)
    else:
        @pl.when(pl.program_id(0) % tiles_per_seq == 0)
        def _():
            carry_sc[...] = jnp.zeros_like(carry_sc)

        last1 = carry_sc[HALO - 1:HALO, :]
        last2 = carry_sc[HALO - 2:HALO - 1, :]
        p1, p2 = _shifted_rows(u, last1, last2, last1, tm)
        carry_sc[...] = u[tm - HALO:tm, :]
    cy = cw_ref[0:1, :] * p2 + cw_ref[1:2, :] * p1 + cw_ref[2:3, :] * u
    y_conv = (b_ref[...].astype(F32) * cy).astype(BF16)
    if fused_attn:
        conv_branch = []
        first = (pl.program_id(0) % tiles_per_seq) == 0
        _prompt_attention(
            first, sink_ref, q_ref, kc_ref, kp_ref, vc_ref, vp_ref, oa_ref, attn_scratch,
            after_scores=lambda: conv_branch.append(jnp.dot(y_conv, wco_ref[...], preferred_element_type=F32)))
        conv_o = conv_branch[0]
    else:
        conv_o = jnp.dot(y_conv, wco_ref[...], preferred_element_type=F32)
    attn_o = jnp.dot(oa_ref[...], wao_ref[...], preferred_element_type=F32)
    mixed = (jax.nn.sigmoid(ga_ref[...].astype(F32)) * attn_o
             + jax.nn.sigmoid(gc_ref[...].astype(F32)) * conv_o)
    mo = jnp.dot(mixed.astype(BF16), wmo_ref[...], preferred_element_type=F32)
    h_ref[...] = x_ref[...] + _rms_norm(mo, g_ref[...])


def _mix(x, oa, u, b, ga, gc, hist, cw, wao, wco, wmo, g, seq, tm, background=(), attn=None):
    t = x.shape[0]
    row = lambda i: (i, 0)
    fixed = lambda i: (0, 0)
    scratch = []
    if attn is None:
        act_specs = [pl.BlockSpec((tm, D_MODEL), row), pl.BlockSpec((tm, Q_W), row)]
        acts = [x, oa]
    else:
        sinks, q, k, v = attn
        ratio = tm // ATT_BLK
        prev = lambda i: (jnp.maximum(i * ratio - 1, 0), 0)
        act_specs = [
            pl.BlockSpec(memory_space=pltpu.SMEM),
            pl.BlockSpec((tm, D_MODEL), row),
            pl.BlockSpec((tm, Q_W), row),
            pl.BlockSpec((tm, KV_W), row),
            pl.BlockSpec((ATT_BLK, KV_W), prev),
            pl.BlockSpec((tm, KV_W), row),
            pl.BlockSpec((ATT_BLK, KV_W), prev),
        ]
        acts = [sinks, x, q, k, k, v, v]
    act_specs += [
        pl.BlockSpec((tm, D_CONV), row),
        pl.BlockSpec((tm, D_CONV), row),
        pl.BlockSpec((tm, D_MODEL), row),
        pl.BlockSpec((tm, D_MODEL), row),
    ]
    acts += [u, b, ga, gc]
    if hist is None:
        hist_period = 0
        scratch.append(pltpu.VMEM((HALO, D_CONV), F32))
    else:
        hist_period = seq
        act_specs += [pl.BlockSpec((tm, D_CONV), row)] * 2
        acts += list(hist)
    attn_bytes = 0
    if attn is not None:
        scratch.append(pltpu.VMEM((tm, Q_W), BF16))
        scratch += _attn_scratch(tm // ATT_BLK, ATT_BLK, 2 * ATT_BLK)
        attn_bytes = (tm * Q_W * 2 + 4 * (tm + ATT_BLK) * KV_W * 4
                      + _attn_scratch_bytes(tm // ATT_BLK, ATT_BLK, 2 * ATT_BLK))
    weights = (wao, wco, wmo)
    out_specs = [pl.BlockSpec((tm, D_MODEL), row)]
    out_shape = [jax.ShapeDtypeStruct((t, D_MODEL), F32)]
    row_bytes = D_MODEL * 4 * 2 + Q_W * 2 + D_CONV * (4 + 2) + D_MODEL * 2 * 2
    if hist is not None:
        row_bytes += 2 * D_CONV * 4
    nbytes = (2 * tm * row_bytes + (Q_W + D_CONV + D_MODEL) * D_MODEL * 2 + 6 * tm * D_MODEL * 4
              + attn_bytes)
    if background:
        cols = background[0].shape[1]
        out_specs += [pl.BlockSpec(memory_space=pl.ANY)] * len(background)
        out_shape += [jax.ShapeDtypeStruct(bw.shape, BF16) for bw in background]
        scratch += _background_scratch(BG_ROWS_UP, cols)
        nbytes += _background_bytes(BG_ROWS_UP, cols)
    in_specs = (act_specs + [pl.BlockSpec((CONV_W, D_CONV), fixed)]
                + [pl.BlockSpec(w.shape, fixed, pipeline_mode=pl.Buffered(1)) for w in weights]
                + [pl.BlockSpec((1, D_MODEL), fixed)] + [pl.BlockSpec(memory_space=pl.ANY)] * len(background))
    return pl.pallas_call(
        functools.partial(_mix_body, tiles_per_seq=max(seq // tm, 1), hist_period=hist_period,
                          fused_attn=attn is not None, n_bg=len(background), n_steps=t // tm),
        grid=(t // tm,),
        in_specs=in_specs,
        out_specs=tuple(out_specs),
        out_shape=tuple(out_shape),
        scratch_shapes=scratch,
        compiler_params=_params(1, nbytes),
        name="mix",
    )(*acts, cw, wao, wco, wmo, g, *background)


def _ffn_up_body(*refs, tiles_per_seq, hist_period, n_bg, n_steps):
    refs = list(refs)
    h_ref, g_ref, wa_ref, wg_ref, cw_ref = refs[:5]
    del refs[:5]
    if hist_period:
        p1_ref, p2_ref = refs[:2]
        del refs[:2]
    bg_src = refs[:n_bg]
    del refs[:n_bg]
    f_ref, aux_ref = refs[:2]
    bg_dst = refs[2:2 + n_bg]
    del refs[:2 + n_bg]
    xn_sc = refs.pop(0)
    if hist_period:
        a_ref = aux_ref
    else:
        tail_ref, carry_sc = aux_ref, refs.pop(0)
    i = pl.program_id(0)
    j = pl.program_id(1)
    if n_bg:
        _background_cast(list(zip(bg_src, bg_dst)), *refs, i * pl.num_programs(1) + j, n_steps)

    @pl.when(j == 0)
    def _():
        xn_sc[...] = _rms_norm(h_ref[...], g_ref[...]).astype(BF16)
        if not hist_period:
            @pl.when(i % tiles_per_seq == 0)
            def _():
                carry_sc[...] = jnp.zeros_like(carry_sc)

    if not hist_period:
        tail = carry_sc[j]
    tm = xn_sc.shape[0]
    rb = min(tm, ROW_BLK)
    for r in range(tm // rb):
        rows = slice(r * rb, (r + 1) * rb)
        xn = xn_sc[rows, :]
        a = jnp.dot(xn, wa_ref[...], preferred_element_type=F32)
        gate = jnp.dot(xn, wg_ref[...], preferred_element_type=F32)
        if hist_period:
            p1, p2 = _shifted_rows(a, p1_ref[rows, :], p2_ref[rows, :], p2_ref[rows, :], hist_period)
            a_ref[rows, :] = a
        else:
            last1 = tail[HALO - 1:HALO, :]
            p1, p2 = _shifted_rows(a, last1, tail[HALO - 2:HALO - 1, :], last1, rb)
            tail = a[rb - HALO:rb, :]
        ua = cw_ref[0:1, :] * p2 + cw_ref[1:2, :] * p1 + cw_ref[2:3, :] * a
        f_ref[rows, :] = (jax.nn.gelu(ua, approximate=True) * gate).astype(BF16)
    if not hist_period:
        carry_sc[j] = tail
        tail_ref[...] = tail


def _ffn_up(h, g, w_up, cw, hist, seq, tm, background=()):
    t = h.shape[0]
    n_j = D_FF // TN_FF
    grid = (t // tm, n_j)
    in_specs = [
        pl.BlockSpec((tm, D_MODEL), lambda i, j: (i, 0)),
        pl.BlockSpec((1, D_MODEL), lambda i, j: (0, 0)),
        pl.BlockSpec((D_MODEL, TN_FF), lambda i, j: (0, j)),
        pl.BlockSpec((D_MODEL, TN_FF), lambda i, j: (0, j + n_j)),
        pl.BlockSpec((CONV_W, TN_FF), lambda i, j: (0, j)),
    ]
    args = [h, g, w_up, w_up, cw]
    scratch = [pltpu.VMEM((tm, D_MODEL), BF16)]
    f_spec = pl.BlockSpec((tm, TN_FF), lambda i, j: (i, j))
    f_shape = jax.ShapeDtypeStruct((t, D_FF), BF16)
    if hist is None:
        hist_period = 0
        scratch.append(pltpu.VMEM((n_j, HALO, TN_FF), F32))
        out_specs = (f_spec, pl.BlockSpec((None, HALO, TN_FF), lambda i, j: (i, 0, j)))
        out_shape = (f_shape, jax.ShapeDtypeStruct((t // tm, HALO, D_FF), F32))
    else:
        hist_period = seq
        in_specs += [pl.BlockSpec((tm, TN_FF), lambda i, j: (i, j))] * 2
        args += list(hist)
        out_specs = (f_spec, pl.BlockSpec((tm, TN_FF), lambda i, j: (i, j)))
        out_shape = (f_shape, jax.ShapeDtypeStruct((t, D_FF), F32))
    nbytes = (2 * tm * D_MODEL * 4 + tm * D_MODEL * 2 + 4 * D_MODEL * TN_FF * 2
              + 2 * tm * TN_FF * (2 + 4 * 3) + 8 * tm * TN_FF * 4)
    if background:
        in_specs += [pl.BlockSpec(memory_space=pl.ANY)] * len(background)
        args += list(background)
        out_specs += (pl.BlockSpec(memory_space=pl.ANY),) * len(background)
        out_shape += tuple(jax.ShapeDtypeStruct(bw.shape, BF16) for bw in background)
        scratch += _background_scratch(BG_ROWS_2K, D_MODEL)
        nbytes += _background_bytes(BG_ROWS_2K, D_MODEL)
    return pl.pallas_call(
        functools.partial(_ffn_up_body, tiles_per_seq=max(seq // tm, 1), hist_period=hist_period,
                          n_bg=len(background), n_steps=grid[0] * grid[1]),
        grid=grid,
        in_specs=in_specs,
        out_specs=out_specs,
        out_shape=out_shape,
        scratch_shapes=scratch,
        compiler_params=_params(2, nbytes),
        name="ffn_up",
    )(*args)


def _ffn_down_body(f_ref, w_ref, h_ref, g_ref, o_ref):
    tm = f_ref.shape[0]
    rb = min(tm, ROW_BLK)
    for r in range(tm // rb):
        rows = slice(r * rb, (r + 1) * rb)
        acc = jnp.dot(f_ref[rows, :], w_ref[...], preferred_element_type=F32)
        o_ref[rows, :] = h_ref[rows, :] + _rms_norm(acc, g_ref[...])


def _ffn_down(f, w, h, g, tm):
    t = f.shape[0]
    nbytes = (2 * tm * (D_FF * 2 + D_MODEL * 4 * 2) + D_FF * D_MODEL * 2
              + 2 * min(tm, ROW_BLK) * D_MODEL * 4)
    return pl.pallas_call(
        _ffn_down_body,
        grid=(t // tm,),
        in_specs=[
            pl.BlockSpec((tm, D_FF), lambda i: (i, 0)),
            pl.BlockSpec((D_FF, D_MODEL), lambda i: (0, 0), pipeline_mode=pl.Buffered(1)),
            pl.BlockSpec((tm, D_MODEL), lambda i: (i, 0)),
            pl.BlockSpec((1, D_MODEL), lambda i: (0, 0)),
        ],
        out_specs=pl.BlockSpec((tm, D_MODEL), lambda i: (i, 0)),
        out_shape=jax.ShapeDtypeStruct((t, D_MODEL), F32),
        compiler_params=_params(1, nbytes),
        name="ffn_down",
    )(f, w, h, g)


def _rope_tables(pos):
    half = ROT_DIM // 2
    inv = jnp.power(jnp.float32(ROPE_THETA), -jnp.arange(half, dtype=F32) * (2.0 / ROT_DIM))
    ang = pos.astype(F32)[:, None] * inv[None, :]
    cos = jnp.cos(ang)
    sin = jnp.sin(ang)
    rest = HEAD_DIM - ROT_DIM
    cos_h = jnp.concatenate([cos, cos, jnp.ones((pos.shape[0], rest), F32)], axis=1)
    sin_h = jnp.concatenate([-sin, sin, jnp.zeros((pos.shape[0], rest), F32)], axis=1)
    reps = LANES // HEAD_DIM
    return jnp.tile(cos_h, (1, reps)), jnp.tile(sin_h, (1, reps))


def _row_history(state, seq):
    b, _, c = state.shape
    p1 = jnp.concatenate([state[:, 1:2], jnp.zeros((b, seq - 1, c), F32)], axis=1)
    p2 = jnp.concatenate([state[:, 0:1], state[:, 1:2], jnp.zeros((b, seq - 2, c), F32)], axis=1)
    return p1.reshape(b * seq, c), p2.reshape(b * seq, c)


def kernel(x_prompt, x_sample, cache_k, cache_v, state_conv_mix, state_conv_ffn, norm_mix_pre, w_in,
           attn_sinks, conv_mix_w, w_attn_o, w_conv_o, w_mix_out, norm_mix_post, norm_ffn_pre,
           w_ffn_up, conv_ffn_w, w_ffn_down, norm_ffn_post):
    assert w_in.shape == (1, D_MODEL, D_IN), "one layer only"
    batch, seq, _ = x_prompt.shape
    dec_batch, dec_seq, _ = x_sample.shape
    assert seq % FF_UP_TM == 0 and dec_batch * dec_seq == ROW_TILE and dec_seq % 16 == 0
    assert dec_batch % SAMPLE_SEQS_PER_STEP == 0
    assert cache_k.shape[2] == WINDOW

    g_pre = norm_mix_pre[0][None]
    g_post = norm_mix_post[0][None]
    g_ffn_pre = norm_ffn_pre[0][None]
    g_ffn_post = norm_ffn_post[0][None]
    sinks = attn_sinks[0]
    cw_mix = conv_mix_w[0]
    cw_ffn = conv_ffn_w[0]

    tp = batch * seq
    xp = x_prompt.reshape(tp, D_MODEL)
    cos_p, sin_p = _rope_tables(jnp.arange(seq, dtype=F32))
    q, k, v, u, b, ga, gc, w_in_b, wao, wco, wmo = _inproj(
        xp, g_pre, cos_p, sin_p, w_in[0], tm=ROW_TILE, background=(w_attn_o[0], w_conv_o[0], w_mix_out[0]))
    hp, w_up = _mix(xp, None, u, b, ga, gc, None, cw_mix, wao, wco, wmo, g_post, seq, tm=ROW_TILE,
                    background=(w_ffn_up[0],), attn=(sinks, q, k, v))
    f, a_tail, w_down = _ffn_up(hp, g_ffn_pre, w_up, cw_ffn, None, seq, tm=FF_UP_TM,
                                background=(w_ffn_down[0],))
    yp = _ffn_down(f, w_down, hp, g_ffn_post, tm=FF_DOWN_TM)

    keep = min(WINDOW, seq)
    new_k_prompt = k.reshape(batch, seq, KV_W)[:, seq - keep:].reshape(1, batch, keep, N_KV_HEADS, HEAD_DIM)
    new_v_prompt = v.reshape(batch, seq, KV_W)[:, seq - keep:].reshape(1, batch, keep, N_KV_HEADS, HEAD_DIM)
    conv_mix_prompt = u.reshape(batch, seq, D_CONV)[:, seq - (CONV_W - 1):][None]
    a_tail = a_tail.reshape(batch, seq // FF_UP_TM, HALO, D_FF)
    conv_ffn_prompt = a_tail[:, -1, HALO - (CONV_W - 1):][None]

    ts = dec_batch * dec_seq
    xs = x_sample.reshape(ts, D_MODEL)
    pos_s = PAST_LEN + jnp.arange(dec_seq, dtype=F32)
    cos_s, sin_s = _rope_tables(jnp.tile(pos_s, dec_batch))
    qs, ks, vs, us, bs, gas, gcs = _inproj(xs, g_pre, cos_s, sin_s, w_in_b, tm=ts)
    n_valid = WINDOW + dec_seq
    n_keys = 2 * WINDOW
    pad = jnp.zeros((dec_batch, n_keys - n_valid, KV_W), F32)
    k_all = jnp.concatenate([cache_k[0].reshape(dec_batch, WINDOW, KV_W),
                             ks.reshape(dec_batch, dec_seq, KV_W), pad], axis=1)
    v_all = jnp.concatenate([cache_v[0].reshape(dec_batch, WINDOW, KV_W),
                             vs.reshape(dec_batch, dec_seq, KV_W), pad], axis=1)
    oas = _attn_sample(sinks, qs, k_all.reshape(dec_batch * n_keys, KV_W),
                       v_all.reshape(dec_batch * n_keys, KV_W), n_valid, n_keys, dec_seq,
                       seqs_per_step=SAMPLE_SEQS_PER_STEP)
    (hs,) = _mix(xs, oas, us, bs, gas, gcs, _row_history(state_conv_mix[0], dec_seq),
                 cw_mix, wao, wco, wmo, g_post, dec_seq, tm=ts)
    fs, a_s = _ffn_up(hs, g_ffn_pre, w_up, cw_ffn, _row_history(state_conv_ffn[0], dec_seq), dec_seq, tm=ts)
    ys = _ffn_down(fs, w_down, hs, g_ffn_post, tm=ts)

    new_k_sample = ks.reshape(1, dec_batch, dec_seq, N_KV_HEADS, HEAD_DIM)
    new_v_sample = vs.reshape(1, dec_batch, dec_seq, N_KV_HEADS, HEAD_DIM)
    conv_mix_sample = us.reshape(dec_batch, dec_seq, D_CONV)[:, dec_seq - (CONV_W - 1):][None]
    conv_ffn_sample = a_s.reshape(dec_batch, dec_seq, D_FF)[:, dec_seq - (CONV_W - 1):][None]

    return (yp.reshape(batch, seq, D_MODEL), ys.reshape(dec_batch, dec_seq, D_MODEL),
            new_k_prompt, new_v_prompt, conv_mix_prompt, conv_ffn_prompt,
            new_k_sample, new_v_sample, conv_mix_sample, conv_ffn_sample)
```

```python
import functools

import jax
import jax.numpy as jnp
from jax import lax
from jax.experimental import pallas as pl
from jax.experimental.pallas import tpu as pltpu

F32 = jnp.float32
BF16 = jnp.bfloat16

D_MODEL = 2048
CHUNK = 64
N_HEADS = 16
N_KV_HEADS = 4
HEAD_DIM = 64
ROT_DIM = HEAD_DIM // 4
ROPE_THETA = 500000.0
WINDOW = 128
D_CONV = 1024
CONV_W = 3
D_FF = 5632
EPS = 1e-6
NEG = -1e30
PAST_LEN = 4096
Q_W = N_HEADS * HEAD_DIM
KV_W = N_KV_HEADS * HEAD_DIM
D_IN = Q_W + 2 * KV_W + 3 * D_CONV + 2 * D_MODEL

LANES = 128
SUBLANES = 8
VMEM_LIMIT_CAP = 60 * 1024 * 1024
VMEM_COMPILER_SLACK = 6 * 1024 * 1024

ROW_TILE = 256
FF_DOWN_TM = 512
SAMPLE_SEQS_PER_STEP = 4
TN_IN = 512
TN_FF = 512
FF_UP_TM = 1024
IN_STAGE_ROWS = 64
BG_ROWS_2K = 256
BG_ROWS_UP = 128
ATT_BLK = 2 * CHUNK
HALO = SUBLANES
ROW_BLK = 256

_T_Q = (0, 2)
_T_KV = (2, 3)
_T_XIN = (3, 5)
_T_B = (5, 7)
_T_C = (7, 9)
_T_GA = (9, 13)
_T_GC = (13, 17)


_RESIDENT = pl.BlockSpec(memory_space=pltpu.VMEM)


def _vmem_limit(nbytes):
    return int(min(nbytes + VMEM_COMPILER_SLACK, VMEM_LIMIT_CAP))


def _params(n_axes, nbytes):
    return pltpu.CompilerParams(
        dimension_semantics=("arbitrary",) * n_axes,
        vmem_limit_bytes=_vmem_limit(nbytes),
    )


def _rms_norm(x, g):
    return x * lax.rsqrt(jnp.mean(x * x, axis=-1, keepdims=True) + EPS) * g


def _rope128(x, cos, sin):
    dh = lax.broadcasted_iota(jnp.int32, x.shape, 1) & (HEAD_DIM - 1)
    half = ROT_DIM // 2
    partner = jnp.where(dh < half, pltpu.roll(x, LANES - half, axis=1), pltpu.roll(x, half, axis=1))
    return x * cos + partner * sin


def _shifted_rows(a, prev1_row0, prev2_row0, prev2_row1, period):
    assert period & (period - 1) == 0
    rows = lax.broadcasted_iota(jnp.int32, (a.shape[0], 1), 0) & (period - 1)
    p1 = jnp.where(rows == 0, prev1_row0, pltpu.roll(a, 1, axis=0))
    p2 = jnp.where(rows == 0, prev2_row0, jnp.where(rows == 1, prev2_row1, pltpu.roll(a, 2, axis=0)))
    return p1, p2


def _stage_weight(w_hbm, w_sc, w_out, stage, in_sems, out_sem, step, n_steps):
    chunk = stage.shape[1]
    n_chunks = w_hbm.shape[0] // chunk

    def fetch(c, slot):
        return pltpu.make_async_copy(w_hbm.at[pl.ds(c * chunk, chunk), :], stage.at[slot], in_sems.at[slot])

    export = pltpu.make_async_copy(w_sc, w_out, out_sem)

    @pl.when(step == 0)
    def _():
        fetch(0, 0).start()

        def body(c, carry):
            slot = lax.rem(c, 2)

            @pl.when(c + 1 < n_chunks)
            def _():
                fetch(c + 1, 1 - slot).start()

            fetch(c, slot).wait()
            w_sc[pl.ds(pl.multiple_of(c * chunk, chunk), chunk), :] = stage[slot].astype(BF16)
            return carry

        lax.fori_loop(0, n_chunks, body, 0)
        export.start()

    @pl.when(step == n_steps - 1)
    def _():
        export.wait()


def _background_cast(jobs, buf_in, buf_out, sems, step, n_steps):
    chunk = buf_in.shape[0]
    assert all(src.shape[0] % chunk == 0 for src, _ in jobs)
    total = sum(src.shape[0] // chunk for src, _ in jobs)
    assert total + 2 <= n_steps

    @pl.when(step < total + 2)
    def _():
        first = 0
        for src, dst in jobs:
            n = src.shape[0] // chunk

            def fetch(c, src=src):
                return pltpu.make_async_copy(src.at[pl.ds(c * chunk, chunk), :], buf_in, sems.at[0])

            def send(c, dst=dst):
                return pltpu.make_async_copy(buf_out, dst.at[pl.ds(c * chunk, chunk), :], sems.at[1])

            c_fetch = step - first
            c_cast = c_fetch - 1
            c_done = c_fetch - 2

            @pl.when((c_done >= 0) & (c_done < n))
            def _():
                send(c_done).wait()

            @pl.when((c_cast >= 0) & (c_cast < n))
            def _():
                fetch(c_cast).wait()
                buf_out[...] = buf_in[...].astype(BF16)
                send(c_cast).start()

            @pl.when((c_fetch >= 0) & (c_fetch < n))
            def _():
                fetch(c_fetch).start()

            first += n


def _background_scratch(chunk, cols):
    return [
        pltpu.VMEM((chunk, cols), F32),
        pltpu.VMEM((chunk, cols), BF16),
        pltpu.SemaphoreType.DMA((2,)),
    ]


def _background_bytes(chunk, cols):
    return chunk * cols * (4 + 2)


def _staging_scratch(w_shape, chunk):
    assert w_shape[0] % chunk == 0
    return [
        pltpu.VMEM(w_shape, BF16),
        pltpu.VMEM((2, chunk, w_shape[1]), F32),
        pltpu.SemaphoreType.DMA((2,)),
        pltpu.SemaphoreType.DMA(()),
    ]


def _staging_bytes(w_shape, chunk):
    return w_shape[0] * w_shape[1] * 2 + 2 * chunk * w_shape[1] * 4


def _inproj_body(*refs, staged, n_bg, n_steps):
    refs = list(refs)
    x_ref, g_ref, cos_ref, sin_ref, w_ref = refs[:5]
    bg_src = refs[5:5 + n_bg]
    del refs[:5 + n_bg]
    q_ref, k_ref, v_ref, u_ref, b_ref, ga_ref, gc_ref = refs[:7]
    del refs[:7]
    if staged:
        w_hbm, w_out = w_ref, refs.pop(0)
    bg_dst = refs[:n_bg]
    del refs[:n_bg]
    xn_sc = refs.pop(0)
    step = pl.program_id(0)
    if staged:
        w_ref, stage, in_sems, out_sem = refs[:4]
        del refs[:4]
        _stage_weight(w_hbm, w_ref, w_out, stage, in_sems, out_sem, step, n_steps)
    if n_bg:
        _background_cast(list(zip(bg_src, bg_dst)), *refs, step, n_steps)
    xn_sc[...] = _rms_norm(x_ref[...], g_ref[...]).astype(BF16)
    cos = cos_ref[...]
    sin = sin_ref[...]

    def col(t):
        return jnp.dot(xn_sc[...], w_ref[:, t * TN_IN:(t + 1) * TN_IN], preferred_element_type=F32)

    def blocks(n):
        return [slice(blk * LANES, (blk + 1) * LANES) for blk in range(n)]

    for t in range(*_T_Q):
        acc = col(t)
        base = (t - _T_Q[0]) * TN_IN
        for sl in blocks(TN_IN // LANES):
            q_ref[:, base + sl.start:base + sl.stop] = (
                _rope128(acc[:, sl], cos, sin) * (HEAD_DIM ** -0.5)).astype(BF16)
    acc = col(_T_KV[0])
    for sl in blocks(KV_W // LANES):
        k_ref[:, sl] = _rope128(acc[:, sl], cos, sin)
    v_ref[...] = acc[:, KV_W:2 * KV_W]
    for c in range(_T_C[1] - _T_C[0]):
        cols = slice(c * TN_IN, (c + 1) * TN_IN)
        u_ref[:, cols] = col(_T_C[0] + c) * col(_T_XIN[0] + c)
        b_ref[:, cols] = col(_T_B[0] + c).astype(BF16)
    for c in range(_T_GA[1] - _T_GA[0]):
        cols = slice(c * TN_IN, (c + 1) * TN_IN)
        ga_ref[:, cols] = col(_T_GA[0] + c).astype(BF16)
        gc_ref[:, cols] = col(_T_GC[0] + c).astype(BF16)


def _inproj(x, g, cos, sin, w, tm, background=()):
    t = x.shape[0]
    n_pos = cos.shape[0] // tm
    row = lambda i: (i, 0)
    fixed = lambda i: (0, 0)
    pos = lambda i: (i % n_pos, 0)
    widths = (Q_W, KV_W, KV_W, D_CONV, D_CONV, D_MODEL, D_MODEL)
    dtypes = (BF16, F32, F32, F32, BF16, BF16, BF16)
    out_row_bytes = sum(wd * jnp.dtype(dt).itemsize for wd, dt in zip(widths, dtypes))
    nbytes = (2 * tm * (D_MODEL * 4 + 2 * LANES * 4 + out_row_bytes)
              + tm * D_MODEL * 2 + 6 * tm * TN_IN * 4)
    out_specs = [pl.BlockSpec((tm, wd), row) for wd in widths]
    out_shape = [jax.ShapeDtypeStruct((t, wd), dt) for wd, dt in zip(widths, dtypes)]
    scratch = [pltpu.VMEM((tm, D_MODEL), BF16)]
    staged = w.dtype == F32
    if staged:
        w_spec = pl.BlockSpec(memory_space=pl.ANY)
        out_specs.append(pl.BlockSpec(memory_space=pl.ANY))
        out_shape.append(jax.ShapeDtypeStruct(w.shape, BF16))
        scratch += _staging_scratch(w.shape, IN_STAGE_ROWS)
        nbytes += _staging_bytes(w.shape, IN_STAGE_ROWS)
    else:
        w_spec = pl.BlockSpec((D_MODEL, D_IN), fixed, pipeline_mode=pl.Buffered(1))
        nbytes += D_MODEL * D_IN * 2
    if background:
        out_specs += [pl.BlockSpec(memory_space=pl.ANY)] * len(background)
        out_shape += [jax.ShapeDtypeStruct(bw.shape, BF16) for bw in background]
        scratch += _background_scratch(BG_ROWS_2K, D_MODEL)
        nbytes += _background_bytes(BG_ROWS_2K, D_MODEL)
    return pl.pallas_call(
        functools.partial(_inproj_body, staged=staged, n_bg=len(background), n_steps=t // tm),
        grid=(t // tm,),
        in_specs=[
            pl.BlockSpec((tm, D_MODEL), row),
            _RESIDENT,
            pl.BlockSpec((tm, LANES), pos),
            pl.BlockSpec((tm, LANES), pos),
            w_spec,
        ] + [pl.BlockSpec(memory_space=pl.ANY)] * len(background),
        out_specs=tuple(out_specs),
        out_shape=tuple(out_shape),
        scratch_shapes=scratch,
        compiler_params=_params(1, nbytes),
        name="inproj",
    )(x, g, cos, sin, w, *background)


def _head_pair_operands(x256, h):
    blk = x256[:, (h // 2) * LANES:(h // 2 + 1) * LANES]
    lo_lanes = lax.broadcasted_iota(jnp.int32, blk.shape, 1) < HEAD_DIM
    if h % 2 == 0:
        lo = jnp.where(lo_lanes, blk, 0.0)
        hi = pltpu.roll(lo, HEAD_DIM, axis=1)
    else:
        hi = jnp.where(lo_lanes, 0.0, blk)
        lo = pltpu.roll(hi, HEAD_DIM, axis=1)
    return jnp.concatenate([lo, hi], axis=0).astype(BF16)


def _attend(q_ref, o_ref, sink_ref, blocks, s_sc, m_sc, p_sc, v_sc, after_scores=None):
    n_keys = blocks[0][1].shape[0]
    n_rows = s_sc.shape[1]
    key_row = lax.broadcasted_iota(jnp.int32, (2 * n_keys, LANES), 0)
    key_lane = lax.broadcasted_iota(jnp.int32, (2 * n_keys, LANES), 1)
    ones_cat = jnp.where((key_row >= n_keys) == (key_lane >= HEAD_DIM), 1.0, 0.0).astype(BF16)
    lo_lanes = lax.broadcasted_iota(jnp.int32, (n_rows, LANES), 1) < HEAD_DIM
    halves = [slice(hh * n_keys, (hh + 1) * n_keys) for hh in range(2)]

    tiles = []
    n_v = 0
    for q_rows, k_all, v_all, valid in blocks:
        for h in range(N_KV_HEADS):
            kcat = _head_pair_operands(k_all, h)
            v_sc[n_v] = jnp.concatenate([_head_pair_operands(v_all, h), ones_cat], axis=1)
            for p in range(2 * h, 2 * h + 2):
                lanes = slice(p * LANES, (p + 1) * LANES)
                sc = lax.dot_general(q_ref[q_rows, lanes], kcat, (((1,), (1,)), ((), ())),
                                     preferred_element_type=F32)
                s_sc[len(tiles)] = jnp.where(valid, sc, NEG)
                tiles.append((q_rows, lanes, p, n_v))
            n_v += 1
    if after_scores is not None:
        after_scores()
    for t, (_, _, p, _) in enumerate(tiles):
        for hh in range(2):
            m = jnp.maximum(jnp.max(s_sc[t, :, halves[hh]], axis=-1, keepdims=True), sink_ref[2 * p + hh])
            m_sc[2 * t + hh] = jnp.broadcast_to(m, (n_rows, LANES))
    for t in range(len(tiles)):
        p_sc[t] = jnp.concatenate(
            [jnp.exp(s_sc[t, :, halves[hh]] - jnp.tile(m_sc[2 * t + hh], (1, n_keys // LANES))).astype(BF16)
             for hh in range(2)], axis=1)
    for t, (q_rows, lanes, p, vi) in enumerate(tiles):
        e = p_sc[t]
        num_den = jnp.dot(e, v_sc[vi], preferred_element_type=F32)
        num = num_den[:, :LANES]
        den = num_den[:, LANES:]
        sink_term = jnp.where(lo_lanes, jnp.exp(sink_ref[2 * p] - m_sc[2 * t]),
                              jnp.exp(sink_ref[2 * p + 1] - m_sc[2 * t + 1]))
        o_ref[q_rows, lanes] = (num / (den + sink_term)).astype(BF16)


def _attn_scratch(n_blocks, n_rows, n_keys):
    n_tiles = n_blocks * N_HEADS // 2
    return [
        pltpu.VMEM((n_tiles, n_rows, 2 * n_keys), F32),
        pltpu.VMEM((2 * n_tiles, n_rows, LANES), F32),
        pltpu.VMEM((n_tiles, n_rows, 2 * n_keys), BF16),
        pltpu.VMEM((n_blocks * N_KV_HEADS, 2 * n_keys, 2 * LANES), BF16),
    ]


def _attn_scratch_bytes(n_blocks, n_rows, n_keys):
    n_tiles = n_blocks * N_HEADS // 2
    return (n_tiles * n_rows * 2 * n_keys * 6 + 2 * n_tiles * n_rows * LANES * 4
            + n_blocks * N_KV_HEADS * 2 * n_keys * 2 * LANES * 2)


def _prompt_attention(first, sink_ref, q_ref, kc_ref, kp_ref, vc_ref, vp_ref, o_ref, scratch, after_scores=None):
    n_sub = q_ref.shape[0] // ATT_BLK
    n_keys = 2 * ATT_BLK
    shape = (ATT_BLK, 2 * n_keys)
    q_chunk = lax.broadcasted_iota(jnp.int32, shape, 0) // CHUNK
    k_chunk = (lax.broadcasted_iota(jnp.int32, shape, 1) & (n_keys - 1)) // CHUNK
    band = (k_chunk >= q_chunk) & (k_chunk <= q_chunk + WINDOW // CHUNK)
    first_valid = band & (k_chunk >= jnp.where(first, ATT_BLK // CHUNK, 0))
    blocks = []
    for s in range(n_sub):
        rows = slice(s * ATT_BLK, (s + 1) * ATT_BLK)
        if s == 0:
            k_prev, v_prev, valid = kp_ref[...], vp_ref[...], first_valid
        else:
            prev = slice((s - 1) * ATT_BLK, s * ATT_BLK)
            k_prev, v_prev, valid = kc_ref[prev, :], vc_ref[prev, :], band
        k_all = jnp.concatenate([k_prev, kc_ref[rows, :]], axis=0)
        v_all = jnp.concatenate([v_prev, vc_ref[rows, :]], axis=0)
        blocks.append((rows, k_all, v_all, valid))
    _attend(q_ref, o_ref, sink_ref, blocks, *scratch, after_scores=after_scores)


def _attn_sample_body(sink_ref, q_ref, k_ref, v_ref, o_ref, *scratch, n_valid, n_keys, rows_per_seq):
    shape = (rows_per_seq, 2 * n_keys)
    valid = (lax.broadcasted_iota(jnp.int32, shape, 1) & (n_keys - 1)) < n_valid
    blocks = []
    for b in range(q_ref.shape[0] // rows_per_seq):
        keys = slice(b * n_keys, (b + 1) * n_keys)
        blocks.append((slice(b * rows_per_seq, (b + 1) * rows_per_seq), k_ref[keys, :], v_ref[keys, :], valid))
    _attend(q_ref, o_ref, sink_ref, blocks, *scratch)


def _attn_sample(sinks, q, k_all, v_all, n_valid, n_keys, rows_per_seq, seqs_per_step):
    t = q.shape[0]
    tq = rows_per_seq * seqs_per_step
    tk = n_keys * seqs_per_step
    nbytes = 2 * (2 * tq * Q_W * 2 + 2 * tk * KV_W * 4) + _attn_scratch_bytes(seqs_per_step, rows_per_seq, n_keys)
    return pl.pallas_call(
        functools.partial(_attn_sample_body, n_valid=n_valid, n_keys=n_keys, rows_per_seq=rows_per_seq),
        grid=(t // tq,),
        in_specs=[
            pl.BlockSpec(memory_space=pltpu.SMEM),
            pl.BlockSpec((tq, Q_W), lambda i: (i, 0)),
            pl.BlockSpec((tk, KV_W), lambda i: (i, 0)),
            pl.BlockSpec((tk, KV_W), lambda i: (i, 0)),
        ],
        out_specs=pl.BlockSpec((tq, Q_W), lambda i: (i, 0)),
        out_shape=jax.ShapeDtypeStruct((t, Q_W), BF16),
        scratch_shapes=_attn_scratch(seqs_per_step, rows_per_seq, n_keys),
        compiler_params=_params(1, nbytes),
        name="attn_sample",
    )(sinks, q, k_all, v_all)


def _mix_body(*refs, tiles_per_seq, hist_period, fused_attn, n_bg, n_steps):
    refs = list(refs)
    if fused_attn:
        sink_ref, x_ref, q_ref, kc_ref, kp_ref, vc_ref, vp_ref = refs[:7]
        del refs[:7]
    else:
        x_ref, oa_ref = refs[:2]
        del refs[:2]
    u_ref, b_ref, ga_ref, gc_ref = refs[:4]
    del refs[:4]
    if hist_period:
        p1_ref, p2_ref = refs[:2]
        del refs[:2]
    cw_ref, wao_ref, wco_ref, wmo_ref, g_ref = refs[:5]
    bg_src = refs[5:5 + n_bg]
    del refs[:5 + n_bg]
    h_ref = refs.pop(0)
    bg_dst = refs[:n_bg]
    del refs[:n_bg]
    if not hist_period:
        carry_sc = refs.pop(0)
    if fused_attn:
        oa_ref = refs.pop(0)
        attn_scratch = refs[:4]
        del refs[:4]
    if n_bg:
        _background_cast(list(zip(bg_src, bg_dst)), *refs, pl.program_id(0), n_steps)
    u = u_ref[...]
    tm = u.shape[0]
    if hist_period:
        p1, p2 = _shifted_rows(u, p1_ref[...], p2_ref[...], p2_ref[...], hist_period)
    else:
        @pl.when(pl.program_id(0) % tiles_per_seq == 0)
        def _():
            carry_sc[...] = jnp.zeros_like(carry_sc)

        last1 = carry_sc[HALO - 1:HALO, :]
        last2 = carry_sc[HALO - 2:HALO - 1, :]
        p1, p2 = _shifted_rows(u, last1, last2, last1, tm)
        carry_sc[...] = u[tm - HALO:tm, :]
    cy = cw_ref[0:1, :] * p2 + cw_ref[1:2, :] * p1 + cw_ref[2:3, :] * u
    y_conv = (b_ref[...].astype(F32) * cy).astype(BF16)
    if fused_attn:
        conv_branch = []
        first = (pl.program_id(0) % tiles_per_seq) == 0
        _prompt_attention(
            first, sink_ref, q_ref, kc_ref, kp_ref, vc_ref, vp_ref, oa_ref, attn_scratch,
            after_scores=lambda: conv_branch.append(jnp.dot(y_conv, wco_ref[...], preferred_element_type=F32)))
        conv_o = conv_branch[0]
    else:
        conv_o = jnp.dot(y_conv, wco_ref[...], preferred_element_type=F32)
    attn_o = jnp.dot(oa_ref[...], wao_ref[...], preferred_element_type=F32)
    mixed = (jax.nn.sigmoid(ga_ref[...].astype(F32)) * attn_o
             + jax.nn.sigmoid(gc_ref[...].astype(F32)) * conv_o)
    mo = jnp.dot(mixed.astype(BF16), wmo_ref[...], preferred_element_type=F32)
    h_ref[...] = x_ref[...] + _rms_norm(mo, g_ref[...])


def _mix(x, oa, u, b, ga, gc, hist, cw, wao, wco, wmo, g, seq, tm, background=(), attn=None):
    t = x.shape[0]
    row = lambda i: (i, 0)
    fixed = lambda i: (0, 0)
    scratch = []
    if attn is None:
        act_specs = [pl.BlockSpec((tm, D_MODEL), row), pl.BlockSpec((tm, Q_W), row)]
        acts = [x, oa]
    else:
        sinks, q, k, v = attn
        ratio = tm // ATT_BLK
        prev = lambda i: (jnp.maximum(i * ratio - 1, 0), 0)
        act_specs = [
            pl.BlockSpec(memory_space=pltpu.SMEM),
            pl.BlockSpec((tm, D_MODEL), row),
            pl.BlockSpec((tm, Q_W), row),
            pl.BlockSpec((tm, KV_W), row),
            pl.BlockSpec((ATT_BLK, KV_W), prev),
            pl.BlockSpec((tm, KV_W), row),
            pl.BlockSpec((ATT_BLK, KV_W), prev),
        ]
        acts = [sinks, x, q, k, k, v, v]
    act_specs += [
        pl.BlockSpec((tm, D_CONV), row),
        pl.BlockSpec((tm, D_CONV), row),
        pl.BlockSpec((tm, D_MODEL), row),
        pl.BlockSpec((tm, D_MODEL), row),
    ]
    acts += [u, b, ga, gc]
    if hist is None:
        hist_period = 0
        scratch.append(pltpu.VMEM((HALO, D_CONV), F32))
    else:
        hist_period = seq
        act_specs += [pl.BlockSpec((tm, D_CONV), row)] * 2
        acts += list(hist)
    attn_bytes = 0
    if attn is not None:
        scratch.append(pltpu.VMEM((tm, Q_W), BF16))
        scratch += _attn_scratch(tm // ATT_BLK, ATT_BLK, 2 * ATT_BLK)
        attn_bytes = (tm * Q_W * 2 + 4 * (tm + ATT_BLK) * KV_W * 4
                      + _attn_scratch_bytes(tm // ATT_BLK, ATT_BLK, 2 * ATT_BLK))
    weights = (wao, wco, wmo)
    out_specs = [pl.BlockSpec((tm, D_MODEL), row)]
    out_shape = [jax.ShapeDtypeStruct((t, D_MODEL), F32)]
    row_bytes = D_MODEL * 4 * 2 + Q_W * 2 + D_CONV * (4 + 2) + D_MODEL * 2 * 2
    if hist is not None:
        row_bytes += 2 * D_CONV * 4
    nbytes = (2 * tm * row_bytes + (Q_W + D_CONV + D_MODEL) * D_MODEL * 2 + 6 * tm * D_MODEL * 4
              + attn_bytes)
    if background:
        cols = background[0].shape[1]
        out_specs += [pl.BlockSpec(memory_space=pl.ANY)] * len(background)
        out_shape += [jax.ShapeDtypeStruct(bw.shape, BF16) for bw in background]
        scratch += _background_scratch(BG_ROWS_UP, cols)
        nbytes += _background_bytes(BG_ROWS_UP, cols)
    in_specs = (act_specs + [_RESIDENT]
                + [pl.BlockSpec(w.shape, fixed, pipeline_mode=pl.Buffered(1)) for w in weights]
                + [_RESIDENT] + [pl.BlockSpec(memory_space=pl.ANY)] * len(background))
    return pl.pallas_call(
        functools.partial(_mix_body, tiles_per_seq=max(seq // tm, 1), hist_period=hist_period,
                          fused_attn=attn is not None, n_bg=len(background), n_steps=t // tm),
        grid=(t // tm,),
        in_specs=in_specs,
        out_specs=tuple(out_specs),
        out_shape=tuple(out_shape),
        scratch_shapes=scratch,
        compiler_params=_params(1, nbytes),
        name="mix",
    )(*acts, cw, wao, wco, wmo, g, *background)


def _ffn_up_body(*refs, tiles_per_seq, hist_period, n_bg, n_steps):
    refs = list(refs)
    h_ref, g_ref, wa_ref, wg_ref, cw_ref = refs[:5]
    del refs[:5]
    if hist_period:
        p1_ref, p2_ref = refs[:2]
        del refs[:2]
    bg_src = refs[:n_bg]
    del refs[:n_bg]
    f_ref, aux_ref = refs[:2]
    bg_dst = refs[2:2 + n_bg]
    del refs[:2 + n_bg]
    xn_sc = refs.pop(0)
    if hist_period:
        a_ref = aux_ref
    else:
        tail_ref, carry_sc = aux_ref, refs.pop(0)
    i = pl.program_id(0)
    j = pl.program_id(1)
    if n_bg:
        _background_cast(list(zip(bg_src, bg_dst)), *refs, i * pl.num_programs(1) + j, n_steps)
    cw = cw_ref[:, pl.ds(pl.multiple_of(j * TN_FF, TN_FF), TN_FF)]

    @pl.when(j == 0)
    def _():
        xn_sc[...] = _rms_norm(h_ref[...], g_ref[...]).astype(BF16)
        if not hist_period:
            @pl.when(i % tiles_per_seq == 0)
            def _():
                carry_sc[...] = jnp.zeros_like(carry_sc)

    if not hist_period:
        tail = carry_sc[j]
    tm = xn_sc.shape[0]
    rb = min(tm, ROW_BLK)
    for r in range(tm // rb):
        rows = slice(r * rb, (r + 1) * rb)
        xn = xn_sc[rows, :]
        a = jnp.dot(xn, wa_ref[...], preferred_element_type=F32)
        gate = jnp.dot(xn, wg_ref[...], preferred_element_type=F32)
        if hist_period:
            p1, p2 = _shifted_rows(a, p1_ref[rows, :], p2_ref[rows, :], p2_ref[rows, :], hist_period)
            a_ref[rows, :] = a
        else:
            last1 = tail[HALO - 1:HALO, :]
            p1, p2 = _shifted_rows(a, last1, tail[HALO - 2:HALO - 1, :], last1, rb)
            tail = a[rb - HALO:rb, :]
        ua = cw[0:1, :] * p2 + cw[1:2, :] * p1 + cw[2:3, :] * a
        f_ref[rows, :] = (jax.nn.gelu(ua, approximate=True) * gate).astype(BF16)
    if not hist_period:
        carry_sc[j] = tail
        tail_ref[...] = tail


def _ffn_up(h, g, w_up, cw, hist, seq, tm, background=()):
    t = h.shape[0]
    n_j = D_FF // TN_FF
    grid = (t // tm, n_j)
    in_specs = [
        pl.BlockSpec((tm, D_MODEL), lambda i, j: (i, 0)),
        _RESIDENT,
        pl.BlockSpec((D_MODEL, TN_FF), lambda i, j: (0, j)),
        pl.BlockSpec((D_MODEL, TN_FF), lambda i, j: (0, j + n_j)),
        _RESIDENT,
    ]
    args = [h, g, w_up, w_up, cw]
    scratch = [pltpu.VMEM((tm, D_MODEL), BF16)]
    f_spec = pl.BlockSpec((tm, TN_FF), lambda i, j: (i, j))
    f_shape = jax.ShapeDtypeStruct((t, D_FF), BF16)
    if hist is None:
        hist_period = 0
        scratch.append(pltpu.VMEM((n_j, HALO, TN_FF), F32))
        out_specs = (f_spec, pl.BlockSpec((None, HALO, TN_FF), lambda i, j: (i, 0, j)))
        out_shape = (f_shape, jax.ShapeDtypeStruct((t // tm, HALO, D_FF), F32))
    else:
        hist_period = seq
        in_specs += [pl.BlockSpec((tm, TN_FF), lambda i, j: (i, j))] * 2
        args += list(hist)
        out_specs = (f_spec, pl.BlockSpec((tm, TN_FF), lambda i, j: (i, j)))
        out_shape = (f_shape, jax.ShapeDtypeStruct((t, D_FF), F32))
    nbytes = (2 * tm * D_MODEL * 4 + tm * D_MODEL * 2 + 4 * D_MODEL * TN_FF * 2
              + 2 * tm * TN_FF * (2 + 4 * 3) + 8 * tm * TN_FF * 4)
    if background:
        in_specs += [pl.BlockSpec(memory_space=pl.ANY)] * len(background)
        args += list(background)
        out_specs += (pl.BlockSpec(memory_space=pl.ANY),) * len(background)
        out_shape += tuple(jax.ShapeDtypeStruct(bw.shape, BF16) for bw in background)
        scratch += _background_scratch(BG_ROWS_2K, D_MODEL)
        nbytes += _background_bytes(BG_ROWS_2K, D_MODEL)
    return pl.pallas_call(
        functools.partial(_ffn_up_body, tiles_per_seq=max(seq // tm, 1), hist_period=hist_period,
                          n_bg=len(background), n_steps=grid[0] * grid[1]),
        grid=grid,
        in_specs=in_specs,
        out_specs=out_specs,
        out_shape=out_shape,
        scratch_shapes=scratch,
        compiler_params=_params(2, nbytes),
        name="ffn_up",
    )(*args)


def _ffn_down_body(f_ref, w_ref, h_ref, g_ref, o_ref):
    tm = f_ref.shape[0]
    rb = min(tm, ROW_BLK)
    for r in range(tm // rb):
        rows = slice(r * rb, (r + 1) * rb)
        acc = jnp.dot(f_ref[rows, :], w_ref[...], preferred_element_type=F32)
        o_ref[rows, :] = h_ref[rows, :] + _rms_norm(acc, g_ref[...])


def _ffn_down(f, w, h, g, tm):
    t = f.shape[0]
    nbytes = (2 * tm * (D_FF * 2 + D_MODEL * 4 * 2) + D_FF * D_MODEL * 2
              + 2 * min(tm, ROW_BLK) * D_MODEL * 4)
    return pl.pallas_call(
        _ffn_down_body,
        grid=(t // tm,),
        in_specs=[
            pl.BlockSpec((tm, D_FF), lambda i: (i, 0)),
            pl.BlockSpec((D_FF, D_MODEL), lambda i: (0, 0), pipeline_mode=pl.Buffered(1)),
            pl.BlockSpec((tm, D_MODEL), lambda i: (i, 0)),
            _RESIDENT,
        ],
        out_specs=pl.BlockSpec((tm, D_MODEL), lambda i: (i, 0)),
        out_shape=jax.ShapeDtypeStruct((t, D_MODEL), F32),
        compiler_params=_params(1, nbytes),
        name="ffn_down",
    )(f, w, h, g)


def _rope_tables(pos):
    half = ROT_DIM // 2
    inv = jnp.power(jnp.float32(ROPE_THETA), -jnp.arange(half, dtype=F32) * (2.0 / ROT_DIM))
    ang = pos.astype(F32)[:, None] * inv[None, :]
    cos = jnp.cos(ang)
    sin = jnp.sin(ang)
    rest = HEAD_DIM - ROT_DIM
    cos_h = jnp.concatenate([cos, cos, jnp.ones((pos.shape[0], rest), F32)], axis=1)
    sin_h = jnp.concatenate([-sin, sin, jnp.zeros((pos.shape[0], rest), F32)], axis=1)
    reps = LANES // HEAD_DIM
    return jnp.tile(cos_h, (1, reps)), jnp.tile(sin_h, (1, reps))


def _row_history(state, seq):
    b, _, c = state.shape
    p1 = jnp.concatenate([state[:, 1:2], jnp.zeros((b, seq - 1, c), F32)], axis=1)
    p2 = jnp.concatenate([state[:, 0:1], state[:, 1:2], jnp.zeros((b, seq - 2, c), F32)], axis=1)
    return p1.reshape(b * seq, c), p2.reshape(b * seq, c)


def kernel(x_prompt, x_sample, cache_k, cache_v, state_conv_mix, state_conv_ffn, norm_mix_pre, w_in,
           attn_sinks, conv_mix_w, w_attn_o, w_conv_o, w_mix_out, norm_mix_post, norm_ffn_pre,
           w_ffn_up, conv_ffn_w, w_ffn_down, norm_ffn_post):
    assert w_in.shape == (1, D_MODEL, D_IN), "one layer only"
    batch, seq, _ = x_prompt.shape
    dec_batch, dec_seq, _ = x_sample.shape
    assert seq % FF_UP_TM == 0 and dec_batch * dec_seq == ROW_TILE and dec_seq % 16 == 0
    assert dec_batch % SAMPLE_SEQS_PER_STEP == 0
    assert cache_k.shape[2] == WINDOW

    g_pre = norm_mix_pre[0][None]
    g_post = norm_mix_post[0][None]
    g_ffn_pre = norm_ffn_pre[0][None]
    g_ffn_post = norm_ffn_post[0][None]
    sinks = attn_sinks[0]
    cw_mix = conv_mix_w[0]
    cw_ffn = conv_ffn_w[0]

    tp = batch * seq
    xp = x_prompt.reshape(tp, D_MODEL)
    cos_p, sin_p = _rope_tables(jnp.arange(seq, dtype=F32))
    q, k, v, u, b, ga, gc, w_in_b, wao, wco, wmo = _inproj(
        xp, g_pre, cos_p, sin_p, w_in[0], tm=ROW_TILE, background=(w_attn_o[0], w_conv_o[0], w_mix_out[0]))
    hp, w_up = _mix(xp, None, u, b, ga, gc, None, cw_mix, wao, wco, wmo, g_post, seq, tm=ROW_TILE,
                    background=(w_ffn_up[0],), attn=(sinks, q, k, v))
    f, a_tail, w_down = _ffn_up(hp, g_ffn_pre, w_up, cw_ffn, None, seq, tm=FF_UP_TM,
                                background=(w_ffn_down[0],))
    yp = _ffn_down(f, w_down, hp, g_ffn_post, tm=FF_DOWN_TM)

    keep = min(WINDOW, seq)
    new_k_prompt = k.reshape(batch, seq, KV_W)[:, seq - keep:].reshape(1, batch, keep, N_KV_HEADS, HEAD_DIM)
    new_v_prompt = v.reshape(batch, seq, KV_W)[:, seq - keep:].reshape(1, batch, keep, N_KV_HEADS, HEAD_DIM)
    conv_mix_prompt = u.reshape(batch, seq, D_CONV)[:, seq - (CONV_W - 1):][None]
    a_tail = a_tail.reshape(batch, seq // FF_UP_TM, HALO, D_FF)
    conv_ffn_prompt = a_tail[:, -1, HALO - (CONV_W - 1):][None]

    ts = dec_batch * dec_seq
    xs = x_sample.reshape(ts, D_MODEL)
    pos_s = PAST_LEN + jnp.arange(dec_seq, dtype=F32)
    cos_s, sin_s = _rope_tables(jnp.tile(pos_s, dec_batch))
    qs, ks, vs, us, bs, gas, gcs = _inproj(xs, g_pre, cos_s, sin_s, w_in_b, tm=ts)
    n_valid = WINDOW + dec_seq
    n_keys = 2 * WINDOW
    pad = jnp.zeros((dec_batch, n_keys - n_valid, KV_W), F32)
    k_all = jnp.concatenate([cache_k[0].reshape(dec_batch, WINDOW, KV_W),
                             ks.reshape(dec_batch, dec_seq, KV_W), pad], axis=1)
    v_all = jnp.concatenate([cache_v[0].reshape(dec_batch, WINDOW, KV_W),
                             vs.reshape(dec_batch, dec_seq, KV_W), pad], axis=1)
    oas = _attn_sample(sinks, qs, k_all.reshape(dec_batch * n_keys, KV_W),
                       v_all.reshape(dec_batch * n_keys, KV_W), n_valid, n_keys, dec_seq,
                       seqs_per_step=SAMPLE_SEQS_PER_STEP)
    (hs,) = _mix(xs, oas, us, bs, gas, gcs, _row_history(state_conv_mix[0], dec_seq),
                 cw_mix, wao, wco, wmo, g_post, dec_seq, tm=ts)
    fs, a_s = _ffn_up(hs, g_ffn_pre, w_up, cw_ffn, _row_history(state_conv_ffn[0], dec_seq), dec_seq, tm=ts)
    ys = _ffn_down(fs, w_down, hs, g_ffn_post, tm=ts)

    new_k_sample = ks.reshape(1, dec_batch, dec_seq, N_KV_HEADS, HEAD_DIM)
    new_v_sample = vs.reshape(1, dec_batch, dec_seq, N_KV_HEADS, HEAD_DIM)
    conv_mix_sample = us.reshape(dec_batch, dec_seq, D_CONV)[:, dec_seq - (CONV_W - 1):][None]
    conv_ffn_sample = a_s.reshape(dec_batch, dec_seq, D_FF)[:, dec_seq - (CONV_W - 1):][None]

    return (yp.reshape(batch, seq, D_MODEL), ys.reshape(dec_batch, dec_seq, D_MODEL),
            new_k_prompt, new_v_prompt, conv_mix_prompt, conv_ffn_prompt,
            new_k_sample, new_v_sample, conv_mix_sample, conv_ffn_sample)
```

```python
import functools

import jax
import jax.numpy as jnp
from jax import lax
from jax.experimental import pallas as pl
from jax.experimental.pallas import tpu as pltpu

F32 = jnp.float32
BF16 = jnp.bfloat16

D_MODEL = 2048
CHUNK = 64
N_HEADS = 16
N_KV_HEADS = 4
HEAD_DIM = 64
ROT_DIM = HEAD_DIM // 4
ROPE_THETA = 500000.0
WINDOW = 128
D_CONV = 1024
CONV_W = 3
D_FF = 5632
EPS = 1e-6
NEG = -1e30
PAST_LEN = 4096
Q_W = N_HEADS * HEAD_DIM
KV_W = N_KV_HEADS * HEAD_DIM
D_IN = Q_W + 2 * KV_W + 3 * D_CONV + 2 * D_MODEL

LANES = 128
SUBLANES = 8
VMEM_LIMIT_CAP = 60 * 1024 * 1024
VMEM_COMPILER_SLACK = 6 * 1024 * 1024

ROW_TILE = 256
FF_DOWN_TM = 512
SAMPLE_SEQS_PER_STEP = 4
TN_IN = 512
TN_FF = 512
FF_UP_TM = 1024
IN_STAGE_ROWS = 64
BG_ROWS_2K = 256
BG_ROWS_UP = 128
BG_ROWS_FF = 512
ATT_BLK = 2 * CHUNK
HALO = SUBLANES
ROW_BLK = 256

_T_Q = (0, 2)
_T_KV = (2, 3)
_T_XIN = (3, 5)
_T_B = (5, 7)
_T_C = (7, 9)
_T_GA = (9, 13)
_T_GC = (13, 17)


def _vmem_limit(nbytes):
    return int(min(nbytes + VMEM_COMPILER_SLACK, VMEM_LIMIT_CAP))


def _params(n_axes, nbytes):
    return pltpu.CompilerParams(
        dimension_semantics=("arbitrary",) * n_axes,
        vmem_limit_bytes=_vmem_limit(nbytes),
    )


def _rms_norm(x, g):
    return x * lax.rsqrt(jnp.mean(x * x, axis=-1, keepdims=True) + EPS) * g


def _rope128(x, cos, sin):
    dh = lax.broadcasted_iota(jnp.int32, x.shape, 1) & (HEAD_DIM - 1)
    half = ROT_DIM // 2
    partner = jnp.where(dh < half, pltpu.roll(x, LANES - half, axis=1), pltpu.roll(x, half, axis=1))
    return x * cos + partner * sin


def _shifted_rows(a, prev1_row0, prev2_row0, prev2_row1, period):
    assert period & (period - 1) == 0
    rows = lax.broadcasted_iota(jnp.int32, (a.shape[0], 1), 0) & (period - 1)
    p1 = jnp.where(rows == 0, prev1_row0, pltpu.roll(a, 1, axis=0))
    p2 = jnp.where(rows == 0, prev2_row0, jnp.where(rows == 1, prev2_row1, pltpu.roll(a, 2, axis=0)))
    return p1, p2


def _stage_weight(w_hbm, w_sc, w_out, stage, in_sems, out_sem, step, n_steps):
    chunk = stage.shape[1]
    n_chunks = w_hbm.shape[0] // chunk

    def fetch(c, slot):
        return pltpu.make_async_copy(w_hbm.at[pl.ds(c * chunk, chunk), :], stage.at[slot], in_sems.at[slot])

    export = pltpu.make_async_copy(w_sc, w_out, out_sem)

    @pl.when(step == 0)
    def _():
        fetch(0, 0).start()

        def body(c, carry):
            slot = lax.rem(c, 2)

            @pl.when(c + 1 < n_chunks)
            def _():
                fetch(c + 1, 1 - slot).start()

            fetch(c, slot).wait()
            w_sc[pl.ds(pl.multiple_of(c * chunk, chunk), chunk), :] = stage[slot].astype(BF16)
            return carry

        lax.fori_loop(0, n_chunks, body, 0)
        export.start()

    @pl.when(step == n_steps - 1)
    def _():
        export.wait()


def _background_cast(jobs, buf_in, buf_out, sems, step, n_steps):
    chunk = buf_in.shape[0]
    assert all(src.shape[0] % chunk == 0 for src, _ in jobs)
    total = sum(src.shape[0] // chunk for src, _ in jobs)
    assert total + 2 <= n_steps

    @pl.when(step < total + 2)
    def _():
        first = 0
        for src, dst in jobs:
            n = src.shape[0] // chunk

            def fetch(c, src=src):
                return pltpu.make_async_copy(src.at[pl.ds(c * chunk, chunk), :], buf_in, sems.at[0])

            def send(c, dst=dst):
                return pltpu.make_async_copy(buf_out, dst.at[pl.ds(c * chunk, chunk), :], sems.at[1])

            c_fetch = step - first
            c_cast = c_fetch - 1
            c_done = c_fetch - 2

            @pl.when((c_done >= 0) & (c_done < n))
            def _():
                send(c_done).wait()

            @pl.when((c_cast >= 0) & (c_cast < n))
            def _():
                fetch(c_cast).wait()
                buf_out[...] = buf_in[...].astype(BF16)
                send(c_cast).start()

            @pl.when((c_fetch >= 0) & (c_fetch < n))
            def _():
                fetch(c_fetch).start()

            first += n


def _background_scratch(chunk, cols):
    return [
        pltpu.VMEM((chunk, cols), F32),
        pltpu.VMEM((chunk, cols), BF16),
        pltpu.SemaphoreType.DMA((2,)),
    ]


def _background_bytes(chunk, cols):
    return chunk * cols * (4 + 2)


def _staging_scratch(w_shape, chunk):
    assert w_shape[0] % chunk == 0
    return [
        pltpu.VMEM(w_shape, BF16),
        pltpu.VMEM((2, chunk, w_shape[1]), F32),
        pltpu.SemaphoreType.DMA((2,)),
        pltpu.SemaphoreType.DMA(()),
    ]


def _staging_bytes(w_shape, chunk):
    return w_shape[0] * w_shape[1] * 2 + 2 * chunk * w_shape[1] * 4


def _inproj_body(*refs, staged, n_bg, n_steps):
    refs = list(refs)
    x_ref, g_ref, cos_ref, sin_ref, w_ref = refs[:5]
    bg_src = refs[5:5 + n_bg]
    del refs[:5 + n_bg]
    q_ref, k_ref, v_ref, u_ref, b_ref, ga_ref, gc_ref = refs[:7]
    del refs[:7]
    if staged:
        w_hbm, w_out = w_ref, refs.pop(0)
    bg_dst = refs[:n_bg]
    del refs[:n_bg]
    xn_sc = refs.pop(0)
    step = pl.program_id(0)
    if staged:
        w_ref, stage, in_sems, out_sem = refs[:4]
        del refs[:4]
        _stage_weight(w_hbm, w_ref, w_out, stage, in_sems, out_sem, step, n_steps)
    if n_bg:
        _background_cast(list(zip(bg_src, bg_dst)), *refs, step, n_steps)
    xn_sc[...] = _rms_norm(x_ref[...], g_ref[...]).astype(BF16)
    cos = cos_ref[...]
    sin = sin_ref[...]

    def col(t):
        return jnp.dot(xn_sc[...], w_ref[:, t * TN_IN:(t + 1) * TN_IN], preferred_element_type=F32)

    def blocks(n):
        return [slice(blk * LANES, (blk + 1) * LANES) for blk in range(n)]

    for t in range(*_T_Q):
        acc = col(t)
        base = (t - _T_Q[0]) * TN_IN
        for sl in blocks(TN_IN // LANES):
            q_ref[:, base + sl.start:base + sl.stop] = (
                _rope128(acc[:, sl], cos, sin) * (HEAD_DIM ** -0.5)).astype(BF16)
    acc = col(_T_KV[0])
    for sl in blocks(KV_W // LANES):
        k_ref[:, sl] = _rope128(acc[:, sl], cos, sin)
    v_ref[...] = acc[:, KV_W:2 * KV_W]
    for c in range(_T_C[1] - _T_C[0]):
        cols = slice(c * TN_IN, (c + 1) * TN_IN)
        u_ref[:, cols] = col(_T_C[0] + c) * col(_T_XIN[0] + c)
        b_ref[:, cols] = col(_T_B[0] + c).astype(BF16)
    for c in range(_T_GA[1] - _T_GA[0]):
        cols = slice(c * TN_IN, (c + 1) * TN_IN)
        ga_ref[:, cols] = col(_T_GA[0] + c).astype(BF16)
        gc_ref[:, cols] = col(_T_GC[0] + c).astype(BF16)


def _inproj(x, g, cos, sin, w, tm, background=()):
    t = x.shape[0]
    n_pos = cos.shape[0] // tm
    row = lambda i: (i, 0)
    fixed = lambda i: (0, 0)
    pos = lambda i: (i % n_pos, 0)
    widths = (Q_W, KV_W, KV_W, D_CONV, D_CONV, D_MODEL, D_MODEL)
    dtypes = (BF16, F32, F32, F32, BF16, BF16, BF16)
    out_row_bytes = sum(wd * jnp.dtype(dt).itemsize for wd, dt in zip(widths, dtypes))
    nbytes = (2 * tm * (D_MODEL * 4 + 2 * LANES * 4 + out_row_bytes)
              + tm * D_MODEL * 2 + 6 * tm * TN_IN * 4)
    out_specs = [pl.BlockSpec((tm, wd), row) for wd in widths]
    out_shape = [jax.ShapeDtypeStruct((t, wd), dt) for wd, dt in zip(widths, dtypes)]
    scratch = [pltpu.VMEM((tm, D_MODEL), BF16)]
    staged = w.dtype == F32
    if staged:
        w_spec = pl.BlockSpec(memory_space=pl.ANY)
        out_specs.append(pl.BlockSpec(memory_space=pl.ANY))
        out_shape.append(jax.ShapeDtypeStruct(w.shape, BF16))
        scratch += _staging_scratch(w.shape, IN_STAGE_ROWS)
        nbytes += _staging_bytes(w.shape, IN_STAGE_ROWS)
    else:
        w_spec = pl.BlockSpec((D_MODEL, D_IN), fixed, pipeline_mode=pl.Buffered(1))
        nbytes += D_MODEL * D_IN * 2
    if background:
        out_specs += [pl.BlockSpec(memory_space=pl.ANY)] * len(background)
        out_shape += [jax.ShapeDtypeStruct(bw.shape, BF16) for bw in background]
        scratch += _background_scratch(BG_ROWS_2K, D_MODEL)
        nbytes += _background_bytes(BG_ROWS_2K, D_MODEL)
    return pl.pallas_call(
        functools.partial(_inproj_body, staged=staged, n_bg=len(background), n_steps=t // tm),
        grid=(t // tm,),
        in_specs=[
            pl.BlockSpec((tm, D_MODEL), row),
            pl.BlockSpec((1, D_MODEL), fixed),
            pl.BlockSpec((tm, LANES), pos),
            pl.BlockSpec((tm, LANES), pos),
            w_spec,
        ] + [pl.BlockSpec(memory_space=pl.ANY)] * len(background),
        out_specs=tuple(out_specs),
        out_shape=tuple(out_shape),
        scratch_shapes=scratch,
        compiler_params=_params(1, nbytes),
        name="inproj",
    )(x, g, cos, sin, w, *background)


def _head_pair_operands(x256, h):
    blk = x256[:, (h // 2) * LANES:(h // 2 + 1) * LANES]
    lo_lanes = lax.broadcasted_iota(jnp.int32, blk.shape, 1) < HEAD_DIM
    if h % 2 == 0:
        lo = jnp.where(lo_lanes, blk, 0.0)
        hi = pltpu.roll(lo, HEAD_DIM, axis=1)
    else:
        hi = jnp.where(lo_lanes, 0.0, blk)
        lo = pltpu.roll(hi, HEAD_DIM, axis=1)
    return jnp.concatenate([lo, hi], axis=0).astype(BF16)


def _attend(q_ref, o_ref, sink_ref, blocks, s_sc, m_sc, p_sc, v_sc, after_scores=None):
    n_keys = blocks[0][1].shape[0]
    n_rows = s_sc.shape[1]
    key_row = lax.broadcasted_iota(jnp.int32, (2 * n_keys, LANES), 0)
    key_lane = lax.broadcasted_iota(jnp.int32, (2 * n_keys, LANES), 1)
    ones_cat = jnp.where((key_row >= n_keys) == (key_lane >= HEAD_DIM), 1.0, 0.0).astype(BF16)
    lo_lanes = lax.broadcasted_iota(jnp.int32, (n_rows, LANES), 1) < HEAD_DIM
    halves = [slice(hh * n_keys, (hh + 1) * n_keys) for hh in range(2)]

    tiles = []
    n_v = 0
    for q_rows, k_all, v_all, valid in blocks:
        for h in range(N_KV_HEADS):
            kcat = _head_pair_operands(k_all, h)
            v_sc[n_v] = jnp.concatenate([_head_pair_operands(v_all, h), ones_cat], axis=1)
            for p in range(2 * h, 2 * h + 2):
                lanes = slice(p * LANES, (p + 1) * LANES)
                sc = lax.dot_general(q_ref[q_rows, lanes], kcat, (((1,), (1,)), ((), ())),
                                     preferred_element_type=F32)
                s_sc[len(tiles)] = jnp.where(valid, sc, NEG)
                tiles.append((q_rows, lanes, p, n_v))
            n_v += 1
    if after_scores is not None:
        after_scores()
    for t, (_, _, p, _) in enumerate(tiles):
        for hh in range(2):
            m = jnp.maximum(jnp.max(s_sc[t, :, halves[hh]], axis=-1, keepdims=True), sink_ref[2 * p + hh])
            m_sc[2 * t + hh] = jnp.broadcast_to(m, (n_rows, LANES))
    for t in range(len(tiles)):
        p_sc[t] = jnp.concatenate(
            [jnp.exp(s_sc[t, :, halves[hh]] - jnp.tile(m_sc[2 * t + hh], (1, n_keys // LANES))).astype(BF16)
             for hh in range(2)], axis=1)
    for t, (q_rows, lanes, p, vi) in enumerate(tiles):
        e = p_sc[t]
        num_den = jnp.dot(e, v_sc[vi], preferred_element_type=F32)
        num = num_den[:, :LANES]
        den = num_den[:, LANES:]
        sink_term = jnp.where(lo_lanes, jnp.exp(sink_ref[2 * p] - m_sc[2 * t]),
                              jnp.exp(sink_ref[2 * p + 1] - m_sc[2 * t + 1]))
        o_ref[q_rows, lanes] = (num / (den + sink_term)).astype(BF16)


def _attn_scratch(n_blocks, n_rows, n_keys):
    n_tiles = n_blocks * N_HEADS // 2
    return [
        pltpu.VMEM((n_tiles, n_rows, 2 * n_keys), F32),
        pltpu.VMEM((2 * n_tiles, n_rows, LANES), F32),
        pltpu.VMEM((n_tiles, n_rows, 2 * n_keys), BF16),
        pltpu.VMEM((n_blocks * N_KV_HEADS, 2 * n_keys, 2 * LANES), BF16),
    ]


def _attn_scratch_bytes(n_blocks, n_rows, n_keys):
    n_tiles = n_blocks * N_HEADS // 2
    return (n_tiles * n_rows * 2 * n_keys * 6 + 2 * n_tiles * n_rows * LANES * 4
            + n_blocks * N_KV_HEADS * 2 * n_keys * 2 * LANES * 2)


def _prompt_attention(first, sink_ref, q_ref, kc_ref, kp_ref, vc_ref, vp_ref, o_ref, scratch, after_scores=None):
    n_sub = q_ref.shape[0] // ATT_BLK
    n_keys = 2 * ATT_BLK
    shape = (ATT_BLK, 2 * n_keys)
    q_chunk = lax.broadcasted_iota(jnp.int32, shape, 0) // CHUNK
    k_chunk = (lax.broadcasted_iota(jnp.int32, shape, 1) & (n_keys - 1)) // CHUNK
    band = (k_chunk >= q_chunk) & (k_chunk <= q_chunk + WINDOW // CHUNK)
    first_valid = band & (k_chunk >= jnp.where(first, ATT_BLK // CHUNK, 0))
    blocks = []
    for s in range(n_sub):
        rows = slice(s * ATT_BLK, (s + 1) * ATT_BLK)
        if s == 0:
            k_prev, v_prev, valid = kp_ref[...], vp_ref[...], first_valid
        else:
            prev = slice((s - 1) * ATT_BLK, s * ATT_BLK)
            k_prev, v_prev, valid = kc_ref[prev, :], vc_ref[prev, :], band
        k_all = jnp.concatenate([k_prev, kc_ref[rows, :]], axis=0)
        v_all = jnp.concatenate([v_prev, vc_ref[rows, :]], axis=0)
        blocks.append((rows, k_all, v_all, valid))
    _attend(q_ref, o_ref, sink_ref, blocks, *scratch, after_scores=after_scores)


def _attn_sample_body(sink_ref, q_ref, k_ref, v_ref, o_ref, *scratch, n_valid, n_keys, rows_per_seq):
    shape = (rows_per_seq, 2 * n_keys)
    valid = (lax.broadcasted_iota(jnp.int32, shape, 1) & (n_keys - 1)) < n_valid
    blocks = []
    for b in range(q_ref.shape[0] // rows_per_seq):
        keys = slice(b * n_keys, (b + 1) * n_keys)
        blocks.append((slice(b * rows_per_seq, (b + 1) * rows_per_seq), k_ref[keys, :], v_ref[keys, :], valid))
    _attend(q_ref, o_ref, sink_ref, blocks, *scratch)


def _attn_sample(sinks, q, k_all, v_all, n_valid, n_keys, rows_per_seq, seqs_per_step):
    t = q.shape[0]
    tq = rows_per_seq * seqs_per_step
    tk = n_keys * seqs_per_step
    nbytes = 2 * (2 * tq * Q_W * 2 + 2 * tk * KV_W * 4) + _attn_scratch_bytes(seqs_per_step, rows_per_seq, n_keys)
    return pl.pallas_call(
        functools.partial(_attn_sample_body, n_valid=n_valid, n_keys=n_keys, rows_per_seq=rows_per_seq),
        grid=(t // tq,),
        in_specs=[
            pl.BlockSpec(memory_space=pltpu.SMEM),
            pl.BlockSpec((tq, Q_W), lambda i: (i, 0)),
            pl.BlockSpec((tk, KV_W), lambda i: (i, 0)),
            pl.BlockSpec((tk, KV_W), lambda i: (i, 0)),
        ],
        out_specs=pl.BlockSpec((tq, Q_W), lambda i: (i, 0)),
        out_shape=jax.ShapeDtypeStruct((t, Q_W), BF16),
        scratch_shapes=_attn_scratch(seqs_per_step, rows_per_seq, n_keys),
        compiler_params=_params(1, nbytes),
        name="attn_sample",
    )(sinks, q, k_all, v_all)


def _mix_body(*refs, tiles_per_seq, hist_period, fused_attn, n_bg, n_steps):
    refs = list(refs)
    if fused_attn:
        sink_ref, x_ref, q_ref, kc_ref, kp_ref, vc_ref, vp_ref = refs[:7]
        del refs[:7]
    else:
        x_ref, oa_ref = refs[:2]
        del refs[:2]
    u_ref, b_ref, ga_ref, gc_ref = refs[:4]
    del refs[:4]
    if hist_period:
        p1_ref, p2_ref = refs[:2]
        del refs[:2]
    cw_ref, wao_ref, wco_ref, wmo_ref, g_ref = refs[:5]
    bg_src = refs[5:5 + n_bg]
    del refs[:5 + n_bg]
    h_ref = refs.pop(0)
    bg_dst = refs[:n_bg]
    del refs[:n_bg]
    if not hist_period:
        carry_sc = refs.pop(0)
    if fused_attn:
        oa_ref = refs.pop(0)
        attn_scratch = refs[:4]
        del refs[:4]
    if n_bg:
        _background_cast(list(zip(bg_src, bg_dst)), *refs, pl.program_id(0), n_steps)
    u = u_ref[...]
    tm = u.shape[0]
    if hist_period:
        p1, p2 = _shifted_rows(u, p1_ref[...], p2_ref[...], p2_ref[...], hist_period)
    else:
        @pl.when(pl.program_id(0) % tiles_per_seq == 0)
        def _():
            carry_sc[...] = jnp.zeros_like(carry_sc)

        last1 = carry_sc[HALO - 1:HALO, :]
        last2 = carry_sc[HALO - 2:HALO - 1, :]
        p1, p2 = _shifted_rows(u, last1, last2, last1, tm)
        carry_sc[...] = u[tm - HALO:tm, :]
    cy = cw_ref[0:1, :] * p2 + cw_ref[1:2, :] * p1 + cw_ref[2:3, :] * u
    y_conv = (b_ref[...].astype(F32) * cy).astype(BF16)
    if fused_attn:
        conv_branch = []
        first = (pl.program_id(0) % tiles_per_seq) == 0
        _prompt_attention(
            first, sink_ref, q_ref, kc_ref, kp_ref, vc_ref, vp_ref, oa_ref, attn_scratch,
            after_scores=lambda: conv_branch.append(jnp.dot(y_conv, wco_ref[...], preferred_element_type=F32)))
        conv_o = conv_branch[0]
    else:
        conv_o = jnp.dot(y_conv, wco_ref[...], preferred_element_type=F32)
    attn_o = jnp.dot(oa_ref[...], wao_ref[...], preferred_element_type=F32)
    mixed = (jax.nn.sigmoid(ga_ref[...].astype(F32)) * attn_o
             + jax.nn.sigmoid(gc_ref[...].astype(F32)) * conv_o)
    mo = jnp.dot(mixed.astype(BF16), wmo_ref[...], preferred_element_type=F32)
    h_ref[...] = x_ref[...] + _rms_norm(mo, g_ref[...])


def _mix(x, oa, u, b, ga, gc, hist, cw, wao, wco, wmo, g, seq, tm, background=(), attn=None):
    t = x.shape[0]
    row = lambda i: (i, 0)
    fixed = lambda i: (0, 0)
    scratch = []
    if attn is None:
        act_specs = [pl.BlockSpec((tm, D_MODEL), row), pl.BlockSpec((tm, Q_W), row)]
        acts = [x, oa]
    else:
        sinks, q, k, v = attn
        ratio = tm // ATT_BLK
        prev = lambda i: (jnp.maximum(i * ratio - 1, 0), 0)
        act_specs = [
            pl.BlockSpec(memory_space=pltpu.SMEM),
            pl.BlockSpec((tm, D_MODEL), row),
            pl.BlockSpec((tm, Q_W), row),
            pl.BlockSpec((tm, KV_W), row),
            pl.BlockSpec((ATT_BLK, KV_W), prev),
            pl.BlockSpec((tm, KV_W), row),
            pl.BlockSpec((ATT_BLK, KV_W), prev),
        ]
        acts = [sinks, x, q, k, k, v, v]
    act_specs += [
        pl.BlockSpec((tm, D_CONV), row),
        pl.BlockSpec((tm, D_CONV), row),
        pl.BlockSpec((tm, D_MODEL), row),
        pl.BlockSpec((tm, D_MODEL), row),
    ]
    acts += [u, b, ga, gc]
    if hist is None:
        hist_period = 0
        scratch.append(pltpu.VMEM((HALO, D_CONV), F32))
    else:
        hist_period = seq
        act_specs += [pl.BlockSpec((tm, D_CONV), row)] * 2
        acts += list(hist)
    attn_bytes = 0
    if attn is not None:
        scratch.append(pltpu.VMEM((tm, Q_W), BF16))
        scratch += _attn_scratch(tm // ATT_BLK, ATT_BLK, 2 * ATT_BLK)
        attn_bytes = (tm * Q_W * 2 + 4 * (tm + ATT_BLK) * KV_W * 4
                      + _attn_scratch_bytes(tm // ATT_BLK, ATT_BLK, 2 * ATT_BLK))
    weights = (wao, wco, wmo)
    out_specs = [pl.BlockSpec((tm, D_MODEL), row)]
    out_shape = [jax.ShapeDtypeStruct((t, D_MODEL), F32)]
    row_bytes = D_MODEL * 4 * 2 + Q_W * 2 + D_CONV * (4 + 2) + D_MODEL * 2 * 2
    if hist is not None:
        row_bytes += 2 * D_CONV * 4
    nbytes = (2 * tm * row_bytes + (Q_W + D_CONV + D_MODEL) * D_MODEL * 2 + 6 * tm * D_MODEL * 4
              + attn_bytes)
    if background:
        cols = background[0].shape[1]
        out_specs += [pl.BlockSpec(memory_space=pl.ANY)] * len(background)
        out_shape += [jax.ShapeDtypeStruct(bw.shape, BF16) for bw in background]
        scratch += _background_scratch(BG_ROWS_UP, cols)
        nbytes += _background_bytes(BG_ROWS_UP, cols)
    in_specs = (act_specs + [pl.BlockSpec((CONV_W, D_CONV), fixed)]
                + [pl.BlockSpec(w.shape, fixed, pipeline_mode=pl.Buffered(1)) for w in weights]
                + [pl.BlockSpec((1, D_MODEL), fixed)] + [pl.BlockSpec(memory_space=pl.ANY)] * len(background))
    return pl.pallas_call(
        functools.partial(_mix_body, tiles_per_seq=max(seq // tm, 1), hist_period=hist_period,
                          fused_attn=attn is not None, n_bg=len(background), n_steps=t // tm),
        grid=(t // tm,),
        in_specs=in_specs,
        out_specs=tuple(out_specs),
        out_shape=tuple(out_shape),
        scratch_shapes=scratch,
        compiler_params=_params(1, nbytes),
        name="mix",
    )(*acts, cw, wao, wco, wmo, g, *background)


def _ffn_up_body(*refs, tiles_per_seq, hist_period, n_bg, n_steps):
    refs = list(refs)
    h_ref, g_ref, wa_ref, wg_ref, cw_ref = refs[:5]
    del refs[:5]
    if hist_period:
        p1_ref, p2_ref = refs[:2]
        del refs[:2]
    bg_src = refs[:n_bg]
    del refs[:n_bg]
    f_ref, aux_ref = refs[:2]
    bg_dst = refs[2:2 + n_bg]
    del refs[:2 + n_bg]
    xn_sc = refs.pop(0)
    if hist_period:
        a_ref = aux_ref
    else:
        tail_ref, carry_sc = aux_ref, refs.pop(0)
    i = pl.program_id(0)
    j = pl.program_id(1)
    if n_bg:
        _background_cast(list(zip(bg_src, bg_dst)), *refs, i * pl.num_programs(1) + j, n_steps)

    @pl.when(j == 0)
    def _():
        xn_sc[...] = _rms_norm(h_ref[...], g_ref[...]).astype(BF16)
        if not hist_period:
            @pl.when(i % tiles_per_seq == 0)
            def _():
                carry_sc[...] = jnp.zeros_like(carry_sc)

    if not hist_period:
        tail = carry_sc[j]
    tm = xn_sc.shape[0]
    rb = min(tm, ROW_BLK)
    for r in range(tm // rb):
        rows = slice(r * rb, (r + 1) * rb)
        xn = xn_sc[rows, :]
        a = jnp.dot(xn, wa_ref[...], preferred_element_type=F32)
        gate = jnp.dot(xn, wg_ref[...], preferred_element_type=F32)
        if hist_period:
            p1, p2 = _shifted_rows(a, p1_ref[rows, :], p2_ref[rows, :], p2_ref[rows, :], hist_period)
            a_ref[rows, :] = a
        else:
            last1 = tail[HALO - 1:HALO, :]
            p1, p2 = _shifted_rows(a, last1, tail[HALO - 2:HALO - 1, :], last1, rb)
            tail = a[rb - HALO:rb, :]
        ua = cw_ref[0:1, :] * p2 + cw_ref[1:2, :] * p1 + cw_ref[2:3, :] * a
        f_ref[rows, :] = (jax.nn.gelu(ua, approximate=True) * gate).astype(BF16)
    if not hist_period:
        carry_sc[j] = tail
        tail_ref[...] = tail


def _ffn_up(h, g, w_up, cw, hist, seq, tm, background=()):
    t = h.shape[0]
    n_j = D_FF // TN_FF
    grid = (t // tm, n_j)
    in_specs = [
        pl.BlockSpec((tm, D_MODEL), lambda i, j: (i, 0)),
        pl.BlockSpec((1, D_MODEL), lambda i, j: (0, 0)),
        pl.BlockSpec((D_MODEL, TN_FF), lambda i, j: (0, j)),
        pl.BlockSpec((D_MODEL, TN_FF), lambda i, j: (0, j + n_j)),
        pl.BlockSpec((CONV_W, TN_FF), lambda i, j: (0, j)),
    ]
    args = [h, g, w_up, w_up, cw]
    scratch = [pltpu.VMEM((tm, D_MODEL), BF16)]
    f_spec = pl.BlockSpec((tm, TN_FF), lambda i, j: (i, j))
    f_shape = jax.ShapeDtypeStruct((t, D_FF), BF16)
    if hist is None:
        hist_period = 0
        scratch.append(pltpu.VMEM((n_j, HALO, TN_FF), F32))
        out_specs = (f_spec, pl.BlockSpec((None, HALO, TN_FF), lambda i, j: (i, 0, j)))
        out_shape = (f_shape, jax.ShapeDtypeStruct((t // tm, HALO, D_FF), F32))
    else:
        hist_period = seq
        in_specs += [pl.BlockSpec((tm, TN_FF), lambda i, j: (i, j))] * 2
        args += list(hist)
        out_specs = (f_spec, pl.BlockSpec((tm, TN_FF), lambda i, j: (i, j)))
        out_shape = (f_shape, jax.ShapeDtypeStruct((t, D_FF), F32))
    nbytes = (2 * tm * D_MODEL * 4 + tm * D_MODEL * 2 + 4 * D_MODEL * TN_FF * 2
              + 2 * tm * TN_FF * (2 + 4 * 3) + 8 * tm * TN_FF * 4)
    if background:
        in_specs += [pl.BlockSpec(memory_space=pl.ANY)] * len(background)
        args += list(background)
        out_specs += (pl.BlockSpec(memory_space=pl.ANY),) * len(background)
        out_shape += tuple(jax.ShapeDtypeStruct(bw.shape, BF16) for bw in background)
        scratch += _background_scratch(BG_ROWS_2K, D_MODEL)
        nbytes += _background_bytes(BG_ROWS_2K, D_MODEL)
    return pl.pallas_call(
        functools.partial(_ffn_up_body, tiles_per_seq=max(seq // tm, 1), hist_period=hist_period,
                          n_bg=len(background), n_steps=grid[0] * grid[1]),
        grid=grid,
        in_specs=in_specs,
        out_specs=out_specs,
        out_shape=out_shape,
        scratch_shapes=scratch,
        compiler_params=_params(2, nbytes),
        name="ffn_up",
    )(*args)


def _ffn_up_rows_body(*refs, tiles_per_seq, n_bg, n_steps):
    refs = list(refs)
    h_ref, g_ref, w_hbm, cw_ref = refs[:4]
    bg_src = refs[4:4 + n_bg]
    del refs[:4 + n_bg]
    f_ref, tail_ref = refs[:2]
    bg_dst = refs[2:2 + n_bg]
    del refs[:2 + n_bg]
    xn_sc, carry_sc, w_buf, w_sems = refs[:4]
    del refs[:4]
    step = pl.program_id(0)
    if n_bg:
        _background_cast(list(zip(bg_src, bg_dst)), *refs, step, n_steps)
    n_tiles = D_FF // TN_FF

    def fetch(j, part):
        src = w_hbm.at[:, pl.ds(part * D_FF + j * TN_FF, TN_FF)]
        return pltpu.make_async_copy(src, w_buf.at[j % 2, part], w_sems.at[j % 2, part])

    for part in range(2):
        fetch(0, part).start()
    xn_sc[...] = _rms_norm(h_ref[...], g_ref[...]).astype(BF16)

    @pl.when(step % tiles_per_seq == 0)
    def _():
        carry_sc[...] = jnp.zeros_like(carry_sc)

    tm = xn_sc.shape[0]
    rb = min(tm, ROW_BLK)
    for j in range(n_tiles):
        cols = slice(j * TN_FF, (j + 1) * TN_FF)
        for part in range(2):
            if j + 1 < n_tiles:
                fetch(j + 1, part).start()
        for part in range(2):
            fetch(j, part).wait()
        tail = carry_sc[:, cols]
        for r in range(tm // rb):
            rows = slice(r * rb, (r + 1) * rb)
            xn = xn_sc[rows, :]
            a = jnp.dot(xn, w_buf[j % 2, 0], preferred_element_type=F32)
            gate = jnp.dot(xn, w_buf[j % 2, 1], preferred_element_type=F32)
            last1 = tail[HALO - 1:HALO, :]
            p1, p2 = _shifted_rows(a, last1, tail[HALO - 2:HALO - 1, :], last1, rb)
            tail = a[rb - HALO:rb, :]
            ua = cw_ref[0:1, cols] * p2 + cw_ref[1:2, cols] * p1 + cw_ref[2:3, cols] * a
            f_ref[rows, cols] = (jax.nn.gelu(ua, approximate=True) * gate).astype(BF16)
        carry_sc[:, cols] = tail
    tail_ref[...] = carry_sc[...]


def _ffn_up_rows(h, g, w_up, cw, seq, tm, background=()):
    t = h.shape[0]
    n_steps = t // tm
    row = lambda i: (i, 0)
    fixed = lambda i: (0, 0)
    in_specs = [
        pl.BlockSpec((tm, D_MODEL), row),
        pl.BlockSpec((1, D_MODEL), fixed),
        pl.BlockSpec(memory_space=pl.ANY),
        pl.BlockSpec((CONV_W, D_FF), fixed),
    ] + [pl.BlockSpec(memory_space=pl.ANY)] * len(background)
    out_specs = [pl.BlockSpec((tm, D_FF), row), pl.BlockSpec((None, HALO, D_FF), lambda i: (i, 0, 0))]
    out_shape = [jax.ShapeDtypeStruct((t, D_FF), BF16), jax.ShapeDtypeStruct((n_steps, HALO, D_FF), F32)]
    scratch = [
        pltpu.VMEM((tm, D_MODEL), BF16),
        pltpu.VMEM((HALO, D_FF), F32),
        pltpu.VMEM((2, 2, D_MODEL, TN_FF), BF16),
        pltpu.SemaphoreType.DMA((2, 2)),
    ]
    nbytes = (2 * tm * D_MODEL * 4 + tm * D_MODEL * 2 + 2 * tm * D_FF * 2 + 4 * D_MODEL * TN_FF * 2
              + 8 * min(tm, ROW_BLK) * TN_FF * 4)
    if background:
        out_specs += [pl.BlockSpec(memory_space=pl.ANY)] * len(background)
        out_shape += [jax.ShapeDtypeStruct(bw.shape, BF16) for bw in background]
        scratch += _background_scratch(BG_ROWS_FF, D_MODEL)
        nbytes += _background_bytes(BG_ROWS_FF, D_MODEL)
    return pl.pallas_call(
        functools.partial(_ffn_up_rows_body, tiles_per_seq=max(seq // tm, 1), n_bg=len(background),
                          n_steps=n_steps),
        grid=(n_steps,),
        in_specs=in_specs,
        out_specs=tuple(out_specs),
        out_shape=tuple(out_shape),
        scratch_shapes=scratch,
        compiler_params=_params(1, nbytes),
        name="ffn_up_rows",
    )(h, g, w_up, cw, *background)


def _ffn_down_body(f_ref, w_ref, h_ref, g_ref, o_ref):
    tm = f_ref.shape[0]
    rb = min(tm, ROW_BLK)
    for r in range(tm // rb):
        rows = slice(r * rb, (r + 1) * rb)
        acc = jnp.dot(f_ref[rows, :], w_ref[...], preferred_element_type=F32)
        o_ref[rows, :] = h_ref[rows, :] + _rms_norm(acc, g_ref[...])


def _ffn_down(f, w, h, g, tm):
    t = f.shape[0]
    nbytes = (2 * tm * (D_FF * 2 + D_MODEL * 4 * 2) + D_FF * D_MODEL * 2
              + 2 * min(tm, ROW_BLK) * D_MODEL * 4)
    return pl.pallas_call(
        _ffn_down_body,
        grid=(t // tm,),
        in_specs=[
            pl.BlockSpec((tm, D_FF), lambda i: (i, 0)),
            pl.BlockSpec((D_FF, D_MODEL), lambda i: (0, 0), pipeline_mode=pl.Buffered(1)),
            pl.BlockSpec((tm, D_MODEL), lambda i: (i, 0)),
            pl.BlockSpec((1, D_MODEL), lambda i: (0, 0)),
        ],
        out_specs=pl.BlockSpec((tm, D_MODEL), lambda i: (i, 0)),
        out_shape=jax.ShapeDtypeStruct((t, D_MODEL), F32),
        compiler_params=_params(1, nbytes),
        name="ffn_down",
    )(f, w, h, g)


def _rope_tables(pos):
    half = ROT_DIM // 2
    inv = jnp.power(jnp.float32(ROPE_THETA), -jnp.arange(half, dtype=F32) * (2.0 / ROT_DIM))
    ang = pos.astype(F32)[:, None] * inv[None, :]
    cos = jnp.cos(ang)
    sin = jnp.sin(ang)
    rest = HEAD_DIM - ROT_DIM
    cos_h = jnp.concatenate([cos, cos, jnp.ones((pos.shape[0], rest), F32)], axis=1)
    sin_h = jnp.concatenate([-sin, sin, jnp.zeros((pos.shape[0], rest), F32)], axis=1)
    reps = LANES // HEAD_DIM
    return jnp.tile(cos_h, (1, reps)), jnp.tile(sin_h, (1, reps))


def _row_history(state, seq):
    b, _, c = state.shape
    p1 = jnp.concatenate([state[:, 1:2], jnp.zeros((b, seq - 1, c), F32)], axis=1)
    p2 = jnp.concatenate([state[:, 0:1], state[:, 1:2], jnp.zeros((b, seq - 2, c), F32)], axis=1)
    return p1.reshape(b * seq, c), p2.reshape(b * seq, c)


def kernel(x_prompt, x_sample, cache_k, cache_v, state_conv_mix, state_conv_ffn, norm_mix_pre, w_in,
           attn_sinks, conv_mix_w, w_attn_o, w_conv_o, w_mix_out, norm_mix_post, norm_ffn_pre,
           w_ffn_up, conv_ffn_w, w_ffn_down, norm_ffn_post):
    assert w_in.shape == (1, D_MODEL, D_IN), "one layer only"
    batch, seq, _ = x_prompt.shape
    dec_batch, dec_seq, _ = x_sample.shape
    assert seq % FF_UP_TM == 0 and dec_batch * dec_seq == ROW_TILE and dec_seq % 16 == 0
    assert dec_batch % SAMPLE_SEQS_PER_STEP == 0
    assert cache_k.shape[2] == WINDOW

    g_pre = norm_mix_pre[0][None]
    g_post = norm_mix_post[0][None]
    g_ffn_pre = norm_ffn_pre[0][None]
    g_ffn_post = norm_ffn_post[0][None]
    sinks = attn_sinks[0]
    cw_mix = conv_mix_w[0]
    cw_ffn = conv_ffn_w[0]

    tp = batch * seq
    xp = x_prompt.reshape(tp, D_MODEL)
    cos_p, sin_p = _rope_tables(jnp.arange(seq, dtype=F32))
    q, k, v, u, b, ga, gc, w_in_b, wao, wco, wmo = _inproj(
        xp, g_pre, cos_p, sin_p, w_in[0], tm=ROW_TILE, background=(w_attn_o[0], w_conv_o[0], w_mix_out[0]))
    hp, w_up = _mix(xp, None, u, b, ga, gc, None, cw_mix, wao, wco, wmo, g_post, seq, tm=ROW_TILE,
                    background=(w_ffn_up[0],), attn=(sinks, q, k, v))
    f, a_tail, w_down = _ffn_up_rows(hp, g_ffn_pre, w_up, cw_ffn, seq, tm=FF_UP_TM,
                                     background=(w_ffn_down[0],))
    yp = _ffn_down(f, w_down, hp, g_ffn_post, tm=FF_DOWN_TM)

    keep = min(WINDOW, seq)
    new_k_prompt = k.reshape(batch, seq, KV_W)[:, seq - keep:].reshape(1, batch, keep, N_KV_HEADS, HEAD_DIM)
    new_v_prompt = v.reshape(batch, seq, KV_W)[:, seq - keep:].reshape(1, batch, keep, N_KV_HEADS, HEAD_DIM)
    conv_mix_prompt = u.reshape(batch, seq, D_CONV)[:, seq - (CONV_W - 1):][None]
    a_tail = a_tail.reshape(batch, seq // FF_UP_TM, HALO, D_FF)
    conv_ffn_prompt = a_tail[:, -1, HALO - (CONV_W - 1):][None]

    ts = dec_batch * dec_seq
    xs = x_sample.reshape(ts, D_MODEL)
    pos_s = PAST_LEN + jnp.arange(dec_seq, dtype=F32)
    cos_s, sin_s = _rope_tables(jnp.tile(pos_s, dec_batch))
    qs, ks, vs, us, bs, gas, gcs = _inproj(xs, g_pre, cos_s, sin_s, w_in_b, tm=ts)
    n_valid = WINDOW + dec_seq
    n_keys = 2 * WINDOW
    pad = jnp.zeros((dec_batch, n_keys - n_valid, KV_W), F32)
    k_all = jnp.concatenate([cache_k[0].reshape(dec_batch, WINDOW, KV_W),
                             ks.reshape(dec_batch, dec_seq, KV_W), pad], axis=1)
    v_all = jnp.concatenate([cache_v[0].reshape(dec_batch, WINDOW, KV_W),
                             vs.reshape(dec_batch, dec_seq, KV_W), pad], axis=1)
    oas = _attn_sample(sinks, qs, k_all.reshape(dec_batch * n_keys, KV_W),
                       v_all.reshape(dec_batch * n_keys, KV_W), n_valid, n_keys, dec_seq,
                       seqs_per_step=SAMPLE_SEQS_PER_STEP)
    (hs,) = _mix(xs, oas, us, bs, gas, gcs, _row_history(state_conv_mix[0], dec_seq),
                 cw_mix, wao, wco, wmo, g_post, dec_seq, tm=ts)
    fs, a_s = _ffn_up(hs, g_ffn_pre, w_up, cw_ffn, _row_history(state_conv_ffn[0], dec_seq), dec_seq, tm=ts)
    ys = _ffn_down(fs, w_down, hs, g_ffn_post, tm=ts)

    new_k_sample = ks.reshape(1, dec_batch, dec_seq, N_KV_HEADS, HEAD_DIM)
    new_v_sample = vs.reshape(1, dec_batch, dec_seq, N_KV_HEADS, HEAD_DIM)
    conv_mix_sample = us.reshape(dec_batch, dec_seq, D_CONV)[:, dec_seq - (CONV_W - 1):][None]
    conv_ffn_sample = a_s.reshape(dec_batch, dec_seq, D_FF)[:, dec_seq - (CONV_W - 1):][None]

    return (yp.reshape(batch, seq, D_MODEL), ys.reshape(dec_batch, dec_seq, D_MODEL),
            new_k_prompt, new_v_prompt, conv_mix_prompt, conv_ffn_prompt,
            new_k_sample, new_v_sample, conv_mix_sample, conv_ffn_sample)
```
